```python
import math
import jax
import jax.numpy as jnp
from jax import lax
import numpy as np

D_MODEL = 1024
BATCH = 1
SEQ = 16384
DEPTH = 2
DEC_BATCH = 8
DEC_SEQ = 4096
PAST_LEN = 128

A_HEADS = D_MODEL // 256
A_KEY_DIM = 128
A_VAL_DIM = D_MODEL // (2 * A_HEADS)
A_KEY_WIDTH = A_HEADS * A_KEY_DIM
A_VAL_WIDTH = A_HEADS * A_VAL_DIM
HGRN_CHUNK = 64
SGU_CHUNK = 128
B_GROUPS = 4
B_GROUP_DIM = D_MODEL // (2 * B_GROUPS)
B_WIDTH = B_GROUPS * B_GROUP_DIM
EVEN_SIZES = (A_KEY_WIDTH, A_KEY_WIDTH, A_KEY_WIDTH, A_VAL_WIDTH, A_VAL_WIDTH, B_WIDTH, B_WIDTH)
EVEN_IN = 3 * A_KEY_WIDTH + 2 * A_VAL_WIDTH + 2 * B_WIDTH
EVEN_MIX = A_VAL_WIDTH + B_WIDTH

POOL_WINDOWS = (2, 4, 8, 16)
C_GROUPS = 4
C_GROUP_DIM = D_MODEL // (2 * C_GROUPS)
C_WIDTH = C_GROUPS * C_GROUP_DIM
D_HEADS = 4
D_V_DIM = D_MODEL // (2 * D_HEADS)
D_QK_DIM = D_V_DIM // 2
D_QK_WIDTH = D_HEADS * 2 * D_QK_DIM
D_WIDTH = D_HEADS * D_V_DIM
ROPE_DIMS = D_QK_DIM // 4
ROPE_THETA = 500000.0
Q_BLOCK = 128
ODD_SIZES = (C_WIDTH, D_QK_WIDTH, D_QK_WIDTH, D_WIDTH)
ODD_IN = C_WIDTH + 2 * D_QK_WIDTH + D_WIDTH
ODD_MIX = C_WIDTH + D_WIDTH

D_FF = 2816
N_EXPERTS = 8
TOP_K = 2
D_EXPERT = 3584
MOE_BLOCK = 256

N_EVEN = (DEPTH + 1) // 2
N_ODD = DEPTH // 2
NORM_EPS = 1e-6

kernel_name = 'hybrid_hgrn2_sgu_pool_diffattn_encoder'


def rms_norm(x, gain):
    xf = x.astype(jnp.float32)
    y = xf * lax.rsqrt(jnp.mean(jnp.square(xf), axis=-1, keepdims=True) + NORM_EPS)
    return (y * gain.astype(jnp.float32)).astype(x.dtype)


def layer_norm(x, gain, bias):
    xf = x.astype(jnp.float32)
    mu = jnp.mean(xf, axis=-1, keepdims=True)
    var = jnp.mean(jnp.square(xf - mu), axis=-1, keepdims=True)
    y = (xf - mu) * lax.rsqrt(var + NORM_EPS) * gain.astype(jnp.float32) + bias.astype(jnp.float32)
    return y.astype(x.dtype)


def split_cols(z, sizes):
    offsets = []
    acc = 0
    for s in sizes[:-1]:
        acc += s
        offsets.append(acc)
    return jnp.split(z, offsets, axis=-1)


def swiglu(x, w1, w3, w2):
    h = jax.nn.silu(jnp.einsum('bsd,df->bsf', x, w1)) * jnp.einsum('bsd,df->bsf', x, w3)
    return jnp.einsum('bsf,fd->bsd', h, w2)


def gla_one_direction(q, k, v, log_f):
    bsz, s_len, heads, kdim = q.shape
    vdim = v.shape[-1]
    c = HGRN_CHUNK
    n = s_len // c

    def chunks(t):
        return t.reshape(bsz, n, c, heads, t.shape[-1]).transpose(1, 0, 3, 2, 4)

    qc, kc, vc, gc = chunks(q), chunks(k), chunks(v), chunks(log_f)
    b = jnp.cumsum(gc, axis=3)
    b_last = b[:, :, :, -1:, :]
    q_dec = qc * jnp.exp(b)
    k_inv = kc * jnp.exp(-b)
    k_end = kc * jnp.exp(b_last - b)
    chunk_decay = jnp.exp(b_last[:, :, :, 0, :])
    mask = jnp.tril(jnp.ones((c, c), dtype=bool))
    scores = jnp.where(mask, jnp.einsum('nbhck,nbhsk->nbhcs', q_dec, k_inv), 0.0)
    o_intra = jnp.einsum('nbhcs,nbhsv->nbhcv', scores, vc)

    def step(state, xs):
        q_n, k_n, v_n, d_n = xs
        o_n = jnp.einsum('bhck,bhkv->bhcv', q_n, state)
        state = d_n[..., None] * state + jnp.einsum('bhck,bhcv->bhkv', k_n, v_n)
        return state, o_n

    state0 = jnp.zeros((bsz, heads, kdim, vdim), jnp.float32)
    _, o_inter = lax.scan(step, state0, (q_dec, k_end, vc, chunk_decay))
    o = o_intra + o_inter
    return o.transpose(1, 0, 3, 2, 4).reshape(bsz, s_len, heads, vdim)


def hgrn2_bidirectional(q_raw, f_fwd_raw, f_bwd_raw, inp, gate, lb, gnorm):
    bsz, s_len, _ = q_raw.shape
    hk = (bsz, s_len, A_HEADS, A_KEY_DIM)
    hv = (bsz, s_len, A_HEADS, A_VAL_DIM)
    q = jax.nn.silu(q_raw.astype(jnp.float32)).reshape(hk)
    v = inp.astype(jnp.float32).reshape(hv)
    lbh = lb.reshape(A_HEADS, A_KEY_DIM)

    def forget(f_raw):
        f = lbh + (1.0 - lbh) * jax.nn.sigmoid(f_raw.astype(jnp.float32).reshape(hk))
        return 1.0 - f, jnp.log(f)

    k_f, g_f = forget(f_fwd_raw)
    k_b, g_b = forget(f_bwd_raw)
    flip = lambda t: jnp.flip(t, axis=1)
    o = gla_one_direction(q, k_f, v, g_f) + flip(gla_one_direction(flip(q), flip(k_b), flip(v), flip(g_b)))
    o = rms_norm(o, gnorm) * jax.nn.silu(gate.astype(jnp.float32).reshape(hv))
    return o.reshape(bsz, s_len, A_VAL_WIDTH).astype(q_raw.dtype)


def chunked_spatial_gating(u, v, ln_g, ln_b, w_s, b_s):
    bsz, s_len, _ = u.shape
    n = s_len // SGU_CHUNK
    u = jax.nn.gelu(u)
    v = jax.nn.gelu(v).reshape(bsz, s_len, B_GROUPS, B_GROUP_DIM)
    v = layer_norm(v, ln_g.reshape(B_GROUPS, B_GROUP_DIM), ln_b.reshape(B_GROUPS, B_GROUP_DIM))
    v = v.reshape(bsz, n, SGU_CHUNK, B_GROUPS, B_GROUP_DIM)
    mixed = jnp.einsum('gts,bnsgc->bntgc', w_s, v) + jnp.transpose(b_s)[None, None, :, :, None]
    return u * mixed.reshape(bsz, s_len, B_WIDTH).astype(u.dtype)


def multiscale_pool(p, w_c, scale):
    bsz, s_len, _ = p.shape
    pf = p.astype(jnp.float32).reshape(bsz, s_len, C_GROUPS, C_GROUP_DIM)
    cs = jnp.pad(jnp.cumsum(pf, axis=1), ((0, 0), (1, 0), (0, 0), (0, 0)))
    t = jnp.arange(s_len)
    outs = []
    for g, w in enumerate(POOL_WINDOWS):
        lo = jnp.maximum(t - w // 2, 0)
        hi = jnp.minimum(t + w // 2, s_len)
        cs_g = cs[:, :, g]
        mean = (cs_g[:, hi] - cs_g[:, lo]) / (hi - lo).astype(jnp.float32)[None, :, None]
        outs.append(mean - pf[:, :, g])
    d = jnp.stack(outs, axis=2)
    y = jnp.einsum('bsgc,gce->bsge', d, w_c.astype(jnp.float32)).reshape(bsz, s_len, C_WIDTH)
    return (y * scale.astype(jnp.float32)).astype(p.dtype)


def rope_tables(s_len):
    inv_freq = ROPE_THETA ** (-jnp.arange(0, ROPE_DIMS, 2, dtype=jnp.float32) / ROPE_DIMS)
    ang = jnp.arange(s_len, dtype=jnp.float32)[:, None] * inv_freq[None, :]
    return jnp.cos(ang), jnp.sin(ang)


def apply_partial_rope(x, cos, sin):
    half = ROPE_DIMS // 2
    x1 = x[..., :half]
    x2 = x[..., half:ROPE_DIMS]
    rest = x[..., ROPE_DIMS:]
    c = cos[None, :, None, None, :].astype(x.dtype)
    s = sin[None, :, None, None, :].astype(x.dtype)
    return jnp.concatenate([x1 * c - x2 * s, x2 * c + x1 * s, rest], axis=-1)


def diff_attention(q, k, v, lam):
    bsz, s_len = q.shape[:2]
    nb = s_len // Q_BLOCK
    scale = D_QK_DIM ** -0.5
    kf = k.astype(jnp.float32)
    vf = v.astype(jnp.float32)
    qb = q.astype(jnp.float32).reshape(bsz, nb, Q_BLOCK, D_HEADS, 2, D_QK_DIM).transpose(1, 0, 2, 3, 4, 5)

    def block(q_blk):
        s = jnp.einsum('bqhmd,bkhmd->bhmqk', q_blk, kf) * scale
        pr = jax.nn.softmax(s, axis=-1)
        w = pr[:, :, 0] - lam * pr[:, :, 1]
        return jnp.einsum('bhqk,bkhv->bqhv', w, vf)

    o = lax.map(block, qb)
    return o.transpose(1, 0, 2, 3, 4).reshape(bsz, s_len, D_HEADS, D_V_DIM)


def moe_swiglu(x, w_router, w1, w3, w2):
    bsz, s_len, dm = x.shape
    xt = x.reshape(-1, dm)
    n_tok = xt.shape[0]
    n_assign = n_tok * TOP_K
    logits = jnp.einsum('td,de->te', xt, w_router).astype(jnp.float32)
    top_logit, top_idx = lax.top_k(logits, TOP_K)
    gates = jax.nn.softmax(top_logit, axis=-1)
    flat_e = top_idx.reshape(-1)
    flat_tok = jnp.repeat(jnp.arange(n_tok, dtype=jnp.int32), TOP_K)
    order = jnp.argsort(flat_e)
    sorted_e = flat_e[order]
    counts = jnp.bincount(flat_e, length=N_EXPERTS)
    starts = jnp.cumsum(counts) - counts
    padded = ((counts + MOE_BLOCK - 1) // MOE_BLOCK) * MOE_BLOCK
    pends = jnp.cumsum(padded)
    pstarts = pends - padded
    dest = pstarts[sorted_e] + (jnp.arange(n_assign) - starts[sorted_e])
    n_blocks = -(-n_assign // MOE_BLOCK) + N_EXPERTS
    n_slots = n_blocks * MOE_BLOCK
    slot_tok = jnp.zeros((n_slots,), jnp.int32).at[dest].set(flat_tok[order])
    slot_gate = jnp.zeros((n_slots,), jnp.float32).at[dest].set(gates.reshape(-1)[order])
    block_e = jnp.minimum(jnp.searchsorted(pends, jnp.arange(n_blocks) * MOE_BLOCK, side='right'), N_EXPERTS - 1)
    xs = xt[slot_tok].reshape(n_blocks, MOE_BLOCK, dm)

    def expert_block(args):
        xb, e = args
        h = jax.nn.silu(xb @ w1[e]) * (xb @ w3[e])
        return h @ w2[e]

    ys = lax.map(expert_block, (xs, block_e)).reshape(n_slots, dm)
    ys = ys * slot_gate[:, None].astype(ys.dtype)
    out = jax.ops.segment_sum(ys, slot_tok, num_segments=n_tok)
    return out.reshape(bsz, s_len, dm).astype(x.dtype)


def encoder_trunk(x, p):
    bsz, s_len, _ = x.shape
    cos, sin = rope_tables(s_len)
    lb_all = jnp.cumsum(jax.nn.softmax(p['hgrn_lb'].astype(jnp.float32), axis=0), axis=0)
    for layer in range(DEPTH):
        if layer % 2 == 0:
            i = layer // 2
            h = rms_norm(x, p['e_norm1'][i])
            z = jnp.einsum('bsd,de->bse', h, p['e_w_in'][i])
            q, f_fwd, f_bwd, inp, gate, u, v = split_cols(z, EVEN_SIZES)
            a_out = hgrn2_bidirectional(q, f_fwd, f_bwd, inp, gate, lb_all[layer], p['e_hgrn_gnorm'][i])
            b_out = chunked_spatial_gating(u, v, p['e_sgu_ln_g'][i], p['e_sgu_ln_b'][i],
                                           p['e_sgu_w'][i], p['e_sgu_b'][i])
            mix = jnp.einsum('bse,ed->bsd', jnp.concatenate([a_out, b_out], axis=-1), p['e_w_out'][i])
            x = x + mix.astype(x.dtype)
            h = rms_norm(x, p['e_norm2'][i])
            x = x + swiglu(h, p['e_ffn_w1'][i], p['e_ffn_w3'][i], p['e_ffn_w2'][i]).astype(x.dtype)
        else:
            j = layer // 2
            h = rms_norm(x, p['o_norm1'][j])
            z = jnp.einsum('bsd,de->bse', h, p['o_w_in'][j])
            pz, qz, kz, vz = split_cols(z, ODD_SIZES)
            c_out = multiscale_pool(pz, p['o_pool_w'][j], p['o_pool_scale'][j])
            q = qz.reshape(bsz, s_len, D_HEADS, 2, D_QK_DIM)
            k = kz.reshape(bsz, s_len, D_HEADS, 2, D_QK_DIM)
            q = apply_partial_rope(rms_norm(q, p['o_q_norm'][j]), cos, sin)
            k = apply_partial_rope(rms_norm(k, p['o_k_norm'][j]), cos, sin)
            v = vz.reshape(bsz, s_len, D_HEADS, D_V_DIM)
            lam_init = 0.8 - 0.6 * math.exp(-0.3 * layer)
            lam = (jnp.exp(jnp.sum(p['o_lambda_q1'][j].astype(jnp.float32) * p['o_lambda_k1'][j].astype(jnp.float32)))
                   - jnp.exp(jnp.sum(p['o_lambda_q2'][j].astype(jnp.float32) * p['o_lambda_k2'][j].astype(jnp.float32)))
                   + lam_init)
            att = diff_attention(q, k, v, lam)
            att = rms_norm(att, p['o_subln'][j]) * (1.0 - lam_init)
            d_out = att.reshape(bsz, s_len, D_WIDTH).astype(x.dtype)
            mix = jnp.einsum('bse,ed->bsd', jnp.concatenate([c_out, d_out], axis=-1), p['o_w_out'][j])
            x = x + mix.astype(x.dtype)
            h = rms_norm(x, p['o_norm2'][j])
            x = x + moe_swiglu(h, p['o_router'][j], p['o_moe_w1'][j], p['o_moe_w3'][j], p['o_moe_w2'][j])
    return x


def setup_inputs(seed: int = 0) -> dict:
    key = jax.random.key(seed)
    ks = jax.random.split(key, 32)

    def nrm(i, shape, scale):
        return jax.random.normal(ks[i], shape, jnp.float32) * scale

    d = D_MODEL
    return {
        'x_prompt': nrm(0, (BATCH, SEQ, d), 1.0),
        'x_sample': nrm(1, (DEC_BATCH, DEC_SEQ, d), 1.0),
        'hgrn_lb': nrm(2, (DEPTH + 1, A_KEY_WIDTH), 0.1),
        'e_norm1': 1.0 + nrm(3, (N_EVEN, d), 0.02),
        'e_w_in': nrm(4, (N_EVEN, d, EVEN_IN), d ** -0.5),
        'e_hgrn_gnorm': 1.0 + nrm(5, (N_EVEN, A_VAL_DIM), 0.02),
        'e_sgu_ln_g': 1.0 + nrm(6, (N_EVEN, B_WIDTH), 0.02),
        'e_sgu_ln_b': nrm(7, (N_EVEN, B_WIDTH), 0.02),
        'e_sgu_w': nrm(8, (N_EVEN, B_GROUPS, SGU_CHUNK, SGU_CHUNK), SGU_CHUNK ** -0.5),
        'e_sgu_b': 1.0 + nrm(9, (N_EVEN, B_GROUPS, SGU_CHUNK), 0.1),
        'e_w_out': nrm(10, (N_EVEN, EVEN_MIX, d), EVEN_MIX ** -0.5),
        'e_norm2': 1.0 + nrm(11, (N_EVEN, d), 0.02),
        'e_ffn_w1': nrm(12, (N_EVEN, d, D_FF), d ** -0.5),
        'e_ffn_w3': nrm(13, (N_EVEN, d, D_FF), d ** -0.5),
        'e_ffn_w2': nrm(14, (N_EVEN, D_FF, d), D_FF ** -0.5),
        'o_norm1': 1.0 + nrm(15, (N_ODD, d), 0.02),
        'o_w_in': nrm(16, (N_ODD, d, ODD_IN), d ** -0.5),
        'o_pool_w': nrm(17, (N_ODD, C_GROUPS, C_GROUP_DIM, C_GROUP_DIM), C_GROUP_DIM ** -0.5),
        'o_pool_scale': 1.0 + nrm(18, (N_ODD, C_WIDTH), 0.1),
        'o_q_norm': 1.0 + nrm(19, (N_ODD, D_QK_DIM), 0.02),
        'o_k_norm': 1.0 + nrm(20, (N_ODD, D_QK_DIM), 0.02),
        'o_lambda_q1': nrm(21, (N_ODD, D_QK_DIM), 0.1),
        'o_lambda_k1': nrm(22, (N_ODD, D_QK_DIM), 0.1),
        'o_lambda_q2': nrm(23, (N_ODD, D_QK_DIM), 0.1),
        'o_lambda_k2': nrm(24, (N_ODD, D_QK_DIM), 0.1),
        'o_subln': 1.0 + nrm(25, (N_ODD, D_V_DIM), 0.02),
        'o_w_out': nrm(26, (N_ODD, ODD_MIX, d), ODD_MIX ** -0.5),
        'o_norm2': 1.0 + nrm(27, (N_ODD, d), 0.02),
        'o_router': nrm(28, (N_ODD, d, N_EXPERTS), d ** -0.5),
        'o_moe_w1': nrm(29, (N_ODD, N_EXPERTS, d, D_EXPERT), d ** -0.5),
        'o_moe_w3': nrm(30, (N_ODD, N_EXPERTS, d, D_EXPERT), d ** -0.5),
        'o_moe_w2': nrm(31, (N_ODD, N_EXPERTS, D_EXPERT, d), D_EXPERT ** -0.5),
    }


def reference(x_prompt, x_sample, hgrn_lb, e_norm1, e_w_in, e_hgrn_gnorm, e_sgu_ln_g, e_sgu_ln_b,
              e_sgu_w, e_sgu_b, e_w_out, e_norm2, e_ffn_w1, e_ffn_w3, e_ffn_w2,
              o_norm1, o_w_in, o_pool_w, o_pool_scale, o_q_norm, o_k_norm,
              o_lambda_q1, o_lambda_k1, o_lambda_q2, o_lambda_k2, o_subln, o_w_out,
              o_norm2, o_router, o_moe_w1, o_moe_w3, o_moe_w2):
    params = dict(
        hgrn_lb=hgrn_lb, e_norm1=e_norm1, e_w_in=e_w_in, e_hgrn_gnorm=e_hgrn_gnorm,
        e_sgu_ln_g=e_sgu_ln_g, e_sgu_ln_b=e_sgu_ln_b, e_sgu_w=e_sgu_w, e_sgu_b=e_sgu_b,
        e_w_out=e_w_out, e_norm2=e_norm2, e_ffn_w1=e_ffn_w1, e_ffn_w3=e_ffn_w3, e_ffn_w2=e_ffn_w2,
        o_norm1=o_norm1, o_w_in=o_w_in, o_pool_w=o_pool_w, o_pool_scale=o_pool_scale,
        o_q_norm=o_q_norm, o_k_norm=o_k_norm, o_lambda_q1=o_lambda_q1, o_lambda_k1=o_lambda_k1,
        o_lambda_q2=o_lambda_q2, o_lambda_k2=o_lambda_k2, o_subln=o_subln, o_w_out=o_w_out,
        o_norm2=o_norm2, o_router=o_router, o_moe_w1=o_moe_w1, o_moe_w3=o_moe_w3, o_moe_w2=o_moe_w2,
    )
    y_prompt = encoder_trunk(x_prompt, params)
    y_sample = encoder_trunk(x_sample, params)
    return (y_prompt, y_sample)
```

```python
import functools
import math

import numpy as np
import jax
import jax.numpy as jnp
from jax import lax
from jax.experimental import pallas as pl
from jax.experimental.pallas import tpu as pltpu

F32 = jnp.float32
BF = jnp.bfloat16

D_MODEL = 1024
EPS = 1e-6
LANES = 128
HEAD_W = 128
MIX_W = 512
HGRN_CHUNK = 64
SGU_CHUNK = 128
POOL_WINDOWS = (2, 4, 8, 16)
POOL_HALO = 16
D_QK = 64
ROPE_DIMS = 16
ROPE_THETA = 500000.0
N_EXPERTS = 8
VMEM_LIMIT = 56 * 1024 * 1024


def _cparams(n_axes, vmem=VMEM_LIMIT):
    return pltpu.CompilerParams(dimension_semantics=("arbitrary",) * n_axes, vmem_limit_bytes=vmem)


def _dot(a, b):
    return jnp.dot(a, b, preferred_element_type=F32)


def _dot_nt(a, b):
    return lax.dot_general(a, b, (((1,), (1,)), ((), ())), preferred_element_type=F32)


def _dot_tn(a, b):
    return lax.dot_general(a, b, (((0,), (0,)), ((), ())), preferred_element_type=F32)


def _split2(x):
    hi = x.astype(BF)
    lo = (x - hi.astype(F32)).astype(BF)
    return hi, lo


def _dot01(a01, x):
    hi, lo = _split2(x)
    return _dot(a01, hi) + _dot(a01, lo)


def _rms_rows(x, gain):
    ms = jnp.mean(x * x, axis=-1, keepdims=True)
    return x * lax.rsqrt(ms + EPS) * gain


def _sigmoid(x):
    return 1.0 / (1.0 + jnp.exp(-x))


def _silu(x):
    return x * _sigmoid(x)


def _block_tables(groups, t):
    starts, ends, pos = [], [], []
    for (b, s) in groups:
        n = s // t
        for _ in range(b):
            for i in range(n):
                starts.append(int(i == 0))
                ends.append(int(i == n - 1))
                pos.append(i)
    return (np.asarray(starts, np.int32), np.asarray(ends, np.int32), np.asarray(pos, np.int32))


def _attn_tables(groups, tq, tk):
    qb, kb, first, last = [], [], [], []
    off = 0
    for (b, s) in groups:
        for bi in range(b):
            base_q = (off + bi * s) // tq
            base_k = (off + bi * s) // tk
            for qi in range(s // tq):
                nk = s // tk
                for ki in range(nk):
                    qb.append(base_q + qi)
                    kb.append(base_k + ki)
                    first.append(int(ki == 0))
                    last.append(int(ki == nk - 1))
        off += b * s
    return tuple(np.asarray(a, np.int32) for a in (qb, kb, first, last))


def _norm_proj_kernel(x_ref, g_ref, w_ref, o_ref):
    h = _rms_rows(x_ref[...], g_ref[...]).astype(BF)
    o_ref[...] = _dot(h, w_ref[...])


def _norm_proj(x, gain, w, tm):
    n, d = x.shape
    e = w.shape[1]
    return pl.pallas_call(
        _norm_proj_kernel,
        out_shape=jax.ShapeDtypeStruct((n, e), F32),
        grid=(n // tm,),
        in_specs=[
            pl.BlockSpec((tm, d), lambda i: (i, 0)),
            pl.BlockSpec((1, d), lambda i: (0, 0)),
            pl.BlockSpec((d, e), lambda i: (0, 0)),
        ],
        out_specs=pl.BlockSpec((tm, e), lambda i: (i, 0)),
        compiler_params=_cparams(1),
        name="norm_proj",
    )(x, gain, w)


def _hgrn_direction(q_raw, f_raw, v, lb, state_ref, o_ref, reverse, t):
    n_chunks = t // HGRN_CHUNK
    q = _silu(q_raw)
    f = lb + (1.0 - lb) * _sigmoid(f_raw)
    k = 1.0 - f
    g = jnp.log(f)
    row = lax.broadcasted_iota(jnp.int32, (t, t), 0)
    col = lax.broadcasted_iota(jnp.int32, (t, t), 1)
    same_chunk = (row // HGRN_CHUNK) == (col // HGRN_CHUNK)
    causal = (col >= row) if reverse else (col <= row)
    keep = jnp.logical_and(same_chunk, causal)
    b = _dot01(jnp.where(keep, 1.0, 0.0).astype(BF), g)
    q_dec = (q * jnp.exp(b)).astype(BF)
    k_inv = (k * jnp.exp(-b)).astype(BF)
    v_bf = v.astype(BF)
    for h in range(MIX_W // HEAD_W):
        lanes = slice(h * HEAD_W, (h + 1) * HEAD_W)
        scores = jnp.where(keep, _dot_nt(q_dec[:, lanes], k_inv[:, lanes]), 0.0)
        o_ref[:, lanes] = _dot(scores.astype(BF), v_bf[:, lanes])
    order = range(n_chunks - 1, -1, -1) if reverse else range(n_chunks)
    for c in order:
        rows = slice(c * HGRN_CHUNK, (c + 1) * HGRN_CHUNK)
        edge = c * HGRN_CHUNK if reverse else (c + 1) * HGRN_CHUNK - 1
        b_edge = b[edge:edge + 1, :]
        k_end = (k[rows, :] * jnp.exp(b_edge - b[rows, :])).astype(BF)
        decay = jnp.exp(b_edge)
        for h in range(MIX_W // HEAD_W):
            lanes = slice(h * HEAD_W, (h + 1) * HEAD_W)
            s_t = state_ref[h]
            o_ref[rows, lanes] += _dot_nt(q_dec[rows, lanes], s_t.astype(BF))
            state_ref[h] = decay[:, lanes] * s_t + _dot_tn(v_bf[rows, lanes], k_end[:, lanes])


def _hgrn_sgu_kernel(starts_ref, ends_ref,
                     qf_ref, ff_ref, vf_ref, qb_ref, fb_ref, vb_ref, u_ref, v_ref,
                     lbp_ref, lng_ref, lnb_ref, ws_ref, bs_ref,
                     of_ref, ob_ref, sgu_ref, sf_ref, sb_ref, *, layer, t):
    i = pl.program_id(0)
    j = pl.num_programs(0) - 1 - i

    @pl.when(starts_ref[i] == 1)
    def _():
        sf_ref[...] = jnp.zeros_like(sf_ref)

    @pl.when(ends_ref[j] == 1)
    def _():
        sb_ref[...] = jnp.zeros_like(sb_ref)

    lbp = lbp_ref[...]
    e = jnp.exp(lbp - jnp.max(lbp, axis=0, keepdims=True))
    sm = e / jnp.sum(e, axis=0, keepdims=True)
    lb = jnp.sum(sm[:layer + 1, :], axis=0, keepdims=True)

    _hgrn_direction(qf_ref[...], ff_ref[...], vf_ref[...], lb, sf_ref, of_ref, False, t)
    _hgrn_direction(qb_ref[...], fb_ref[...], vb_ref[...], lb, sb_ref, ob_ref, True, t)

    u = jax.nn.gelu(u_ref[...])
    v = jax.nn.gelu(v_ref[...])
    for gi in range(MIX_W // HEAD_W):
        lanes = slice(gi * HEAD_W, (gi + 1) * HEAD_W)
        vg = v[:, lanes]
        mu = jnp.mean(vg, axis=-1, keepdims=True)
        var = jnp.mean(jnp.square(vg - mu), axis=-1, keepdims=True)
        vln = ((vg - mu) * lax.rsqrt(var + EPS) * lng_ref[:, lanes] + lnb_ref[:, lanes]).astype(BF)
        for c in range(t // SGU_CHUNK):
            rows = slice(c * SGU_CHUNK, (c + 1) * SGU_CHUNK)
            mixed = _dot(ws_ref[gi], vln[rows, :]) + bs_ref[gi]
            sgu_ref[rows, lanes] = u[rows, lanes] * mixed


def _hgrn_sgu(z, hgrn_lb, ln_g, ln_b, w_s, b_s, groups, layer, t):
    n = z.shape[0]
    nblk = n // t
    starts, ends, _ = _block_tables(groups, t)
    fwd = lambda c: pl.BlockSpec((t, MIX_W), lambda i, s, e, c=c: (i, c))
    bwd = lambda c: pl.BlockSpec((t, MIX_W), lambda i, s, e, c=c: (nblk - 1 - i, c))
    full = lambda shape: pl.BlockSpec(shape, lambda i, s, e: (0,) * len(shape))
    bs_b = jnp.broadcast_to(b_s[:, :, None], b_s.shape + (HEAD_W,)).astype(F32)
    out = jax.ShapeDtypeStruct((n, MIX_W), F32)
    return pl.pallas_call(
        functools.partial(_hgrn_sgu_kernel, layer=layer, t=t),
        out_shape=(out, out, out),
        grid_spec=pltpu.PrefetchScalarGridSpec(
            num_scalar_prefetch=2,
            grid=(nblk,),
            in_specs=[fwd(0), fwd(1), fwd(3), bwd(0), bwd(2), bwd(3), fwd(5), fwd(6),
                      full(hgrn_lb.shape), full((1, MIX_W)), full((1, MIX_W)),
                      full(w_s.shape), full(bs_b.shape)],
            out_specs=[fwd(0), bwd(0), fwd(0)],
            scratch_shapes=[pltpu.VMEM((MIX_W // HEAD_W, HEAD_W, HEAD_W), F32),
                            pltpu.VMEM((MIX_W // HEAD_W, HEAD_W, HEAD_W), F32)],
        ),
        compiler_params=_cparams(1),
        name="hgrn_sgu",
    )(jnp.asarray(starts), jnp.asarray(ends), z, z, z, z, z, z, z, z,
      hgrn_lb, ln_g.reshape(1, -1), ln_b.reshape(1, -1), w_s.astype(BF), bs_b)


def _mix0_ffn_kernel(x_ref, of_ref, ob_ref, gate_ref, sgu_ref, gn_ref, wo_ref, g2_ref,
                     w1_ref, w3_ref, w2_ref, o_ref, *, f_chunk):
    o = of_ref[...] + ob_ref[...]
    gate = gate_ref[...]
    parts = []
    for h in range(MIX_W // HEAD_W):
        lanes = slice(h * HEAD_W, (h + 1) * HEAD_W)
        parts.append(_rms_rows(o[:, lanes], gn_ref[...]) * _silu(gate[:, lanes]))
    parts.append(sgu_ref[...])
    mix_in = jnp.concatenate(parts, axis=-1).astype(BF)
    x1 = x_ref[...] + _dot(mix_in, wo_ref[...])
    h2 = _rms_rows(x1, g2_ref[...]).astype(BF)
    acc = x1
    d_ff = w1_ref.shape[1]
    for c in range(d_ff // f_chunk):
        cols = slice(c * f_chunk, (c + 1) * f_chunk)
        hh = _silu(_dot(h2, w1_ref[:, cols])) * _dot(h2, w3_ref[:, cols])
        acc = acc + _dot(hh.astype(BF), w2_ref[cols, :])
    o_ref[...] = acc


def _resident(shape):
    return pl.BlockSpec(shape, lambda i: (0,) * len(shape), pipeline_mode=pl.Buffered(1))


def _mix0_ffn(x, z, o_f, o_b, sgu, gnorm, w_out, g2, w1, w3, w2, tm):
    n, d = x.shape
    d_ff = w1.shape[1]
    tok = lambda w, c=0: pl.BlockSpec((tm, w), lambda i, c=c: (i, c))
    return pl.pallas_call(
        functools.partial(_mix0_ffn_kernel, f_chunk=d_ff // 2),
        out_shape=jax.ShapeDtypeStruct((n, d), F32),
        grid=(n // tm,),
        in_specs=[tok(d), tok(MIX_W), tok(MIX_W), tok(MIX_W, 4), tok(MIX_W),
                  _resident((1, HEAD_W)), _resident(w_out.shape), _resident((1, d)),
                  _resident(w1.shape), _resident(w3.shape), _resident(w2.shape)],
        out_specs=tok(d),
        compiler_params=_cparams(1),
        name="mix0_ffn",
    )(x, o_f, o_b, z, sgu, gnorm.reshape(1, -1), w_out, g2.reshape(1, -1), w1, w3, w2)


def _rope_tables(s_max):
    half = ROPE_DIMS // 2
    inv_freq = ROPE_THETA ** (-jnp.arange(0, ROPE_DIMS, 2, dtype=F32) / ROPE_DIMS)
    ang = jnp.arange(s_max, dtype=F32)[:, None] * inv_freq[None, :]
    cos, sin = jnp.cos(ang), jnp.sin(ang)
    ones = jnp.ones((s_max, D_QK - ROPE_DIMS), F32)
    zeros_h = jnp.zeros((s_max, half), F32)
    zeros_r = jnp.zeros((s_max, D_QK - ROPE_DIMS), F32)
    c64 = jnp.concatenate([cos, cos, ones], axis=-1)
    s_lo = jnp.concatenate([-sin, zeros_h, zeros_r], axis=-1)
    s_hi = jnp.concatenate([zeros_h, sin, zeros_r], axis=-1)
    tile = lambda a: jnp.concatenate([a, a], axis=-1)
    return tile(c64), tile(s_lo), tile(s_hi)


def _qk_norm_rope(x, gain, seg, cos, s_lo, s_hi, scale):
    half = ROPE_DIMS // 2
    sq_hi, sq_lo = _split2(x * x)
    ms = _dot(sq_hi, seg) + _dot(sq_lo, seg)
    xn = x * lax.rsqrt(ms + EPS) * gain
    up = pltpu.roll(xn, LANES - half, axis=1)
    down = pltpu.roll(xn, half, axis=1)
    return ((xn * cos + up * s_lo + down * s_hi) * scale).astype(BF)


def _proj1_kernel(pos_ref, x_ref, g_ref, w_ref, qg_ref, kg_ref, cos_ref, slo_ref, shi_ref,
                  p_ref, q_ref, k_ref, v_ref):
    h = _rms_rows(x_ref[...], g_ref[...]).astype(BF)
    z = _dot(h, w_ref[...])
    p_ref[...] = z[:, :MIX_W]
    v_ref[...] = z[:, 3 * MIX_W:].astype(BF)
    r = lax.broadcasted_iota(jnp.int32, (LANES, LANES), 0)
    c = lax.broadcasted_iota(jnp.int32, (LANES, LANES), 1)
    seg = jnp.where((r // D_QK) == (c // D_QK), 1.0 / D_QK, 0.0).astype(BF)
    cos, s_lo, s_hi = cos_ref[...], slo_ref[...], shi_ref[...]
    for hd in range(MIX_W // HEAD_W):
        lanes = slice(hd * HEAD_W, (hd + 1) * HEAD_W)
        q_ref[:, lanes] = _qk_norm_rope(z[:, MIX_W + hd * HEAD_W:MIX_W + (hd + 1) * HEAD_W],
                                        qg_ref[...], seg, cos, s_lo, s_hi, D_QK ** -0.5)
        k_ref[:, lanes] = _qk_norm_rope(z[:, 2 * MIX_W + hd * HEAD_W:2 * MIX_W + (hd + 1) * HEAD_W],
                                        kg_ref[...], seg, cos, s_lo, s_hi, 1.0)


def _proj1(x, gain, w, q_gain, k_gain, groups, tm):
    n, d = x.shape
    e = w.shape[1]
    _, _, pos = _block_tables(groups, tm)
    s_max = max(s for _, s in groups)
    cos, s_lo, s_hi = _rope_tables(s_max)
    tok = lambda width: pl.BlockSpec((tm, width), lambda i, p: (i, 0))
    full = lambda shape: pl.BlockSpec(shape, lambda i, p: (0,) * len(shape))
    rope = pl.BlockSpec((tm, LANES), lambda i, p: (p[i], 0))
    tile2 = lambda g: jnp.concatenate([g, g]).reshape(1, LANES)
    return pl.pallas_call(
        _proj1_kernel,
        out_shape=(jax.ShapeDtypeStruct((n, MIX_W), F32), jax.ShapeDtypeStruct((n, MIX_W), BF),
                   jax.ShapeDtypeStruct((n, MIX_W), BF), jax.ShapeDtypeStruct((n, MIX_W), BF)),
        grid_spec=pltpu.PrefetchScalarGridSpec(
            num_scalar_prefetch=1,
            grid=(n // tm,),
            in_specs=[tok(d), full((1, d)), full((d, e)), full((1, LANES)), full((1, LANES)),
                      rope, rope, rope],
            out_specs=[tok(MIX_W)] * 4,
        ),
        compiler_params=_cparams(1),
        name="proj1_qk_rope",
    )(jnp.asarray(pos), x, gain.reshape(1, -1), w, tile2(q_gain), tile2(k_gain), cos, s_lo, s_hi)


def _pool_kernel(starts_ref, ends_ref, prev_ref, cur_ref, next_ref, wc_ref, scale_ref, o_ref, *, t):
    i = pl.program_id(0)
    is_start = starts_ref[i] == 1
    is_end = ends_ref[i] == 1
    cur = cur_ref[...]
    prev = jnp.where(is_start, 0.0, prev_ref[...])
    nxt = jnp.where(is_end, 0.0, next_ref[...])
    ext = jnp.concatenate([prev, cur, nxt], axis=0)
    ext_hi, ext_lo = _split2(ext)
    r = lax.broadcasted_iota(jnp.int32, (t, t + 2 * POOL_HALO), 0) + POOL_HALO
    c = lax.broadcasted_iota(jnp.int32, (t, t + 2 * POOL_HALO), 1)
    row = lax.broadcasted_iota(jnp.int32, (t, 1), 0)
    for gi, w in enumerate(POOL_WINDOWS):
        lanes = slice(gi * HEAD_W, (gi + 1) * HEAD_W)
        hw = w // 2
        band = jnp.where(jnp.logical_and(c >= r - hw, c < r + hw), 1.0, 0.0).astype(BF)
        win = _dot(band, ext_hi[:, lanes]) + _dot(band, ext_lo[:, lanes])
        lo_cut = jnp.where(is_start, jnp.maximum(hw - row, 0), 0)
        hi_cut = jnp.where(is_end, jnp.maximum(row + hw - t, 0), 0)
        cnt = (w - lo_cut - hi_cut).astype(F32)
        diff = win / cnt - cur[:, lanes]
        o_ref[:, lanes] = _dot(diff.astype(BF), wc_ref[gi]) * scale_ref[:, lanes]


def _pool(p, w_c, scale, groups, t):
    n = p.shape[0]
    nblk = n // t
    starts, ends, _ = _block_tables(groups, t)
    per = t // POOL_HALO
    n_halo = n // POOL_HALO
    return pl.pallas_call(
        functools.partial(_pool_kernel, t=t),
        out_shape=jax.ShapeDtypeStruct((n, MIX_W), F32),
        grid_spec=pltpu.PrefetchScalarGridSpec(
            num_scalar_prefetch=2,
            grid=(nblk,),
            in_specs=[
                pl.BlockSpec((POOL_HALO, MIX_W), lambda i, s, e: (jnp.maximum(i * per - 1, 0), 0)),
                pl.BlockSpec((t, MIX_W), lambda i, s, e: (i, 0)),
                pl.BlockSpec((POOL_HALO, MIX_W), lambda i, s, e: (jnp.minimum((i + 1) * per, n_halo - 1), 0)),
                pl.BlockSpec(w_c.shape, lambda i, s, e: (0, 0, 0)),
                pl.BlockSpec((1, MIX_W), lambda i, s, e: (0, 0)),
            ],
            out_specs=pl.BlockSpec((t, MIX_W), lambda i, s, e: (i, 0)),
        ),
        compiler_params=_cparams(1),
        name="pool",
    )(jnp.asarray(starts), jnp.asarray(ends), p, p, p, w_c.astype(BF), scale.reshape(1, -1))


def _diff_attn_kernel(qb_ref, kb_ref, first_ref, last_ref,
                      q_ref, k_ref, v_ref, lq1_ref, lk1_ref, lq2_ref, lk2_ref, sub_ref,
                      o_ref, qz_ref, m_ref, l_ref, acc_ref, *, tq, lam_init):
    p = pl.program_id(1)

    @pl.when(first_ref[p] == 1)
    def _():
        q = q_ref[...]
        lane = lax.broadcasted_iota(jnp.int32, q.shape, 1)
        qz_ref[:tq, :] = jnp.where(lane < D_QK, q, jnp.zeros_like(q))
        qz_ref[tq:, :] = jnp.where(lane >= D_QK, q, jnp.zeros_like(q))
        m_ref[...] = jnp.full_like(m_ref, -jnp.inf)
        l_ref[...] = jnp.zeros_like(l_ref)
        acc_ref[...] = jnp.zeros_like(acc_ref)

    s = _dot_nt(qz_ref[...], k_ref[...])
    m_old = m_ref[...]
    m_new = jnp.maximum(m_old, jnp.max(s, axis=-1, keepdims=True))
    alpha = jnp.exp(m_old - m_new)
    pr = jnp.exp(s - m_new)
    l_ref[...] = alpha * l_ref[...] + jnp.sum(pr, axis=-1, keepdims=True)
    acc_ref[...] = alpha * acc_ref[...] + _dot(pr.astype(BF), v_ref[...])
    m_ref[...] = m_new

    @pl.when(last_ref[p] == 1)
    def _():
        lam = (jnp.exp(jnp.sum(lq1_ref[...] * lk1_ref[...], keepdims=True))
               - jnp.exp(jnp.sum(lq2_ref[...] * lk2_ref[...], keepdims=True)) + lam_init)
        o = acc_ref[...] / l_ref[...]
        d = o[:tq, :] - lam * o[tq:, :]
        o_ref[...] = _rms_rows(d, sub_ref[...]) * (1.0 - lam_init)


def _diff_attn(q, k, v, lq1, lk1, lq2, lk2, subln, groups, lam_init, tq, tk):
    n = q.shape[0]
    qb, kb, first, last = _attn_tables(groups, tq, tk)
    n_heads = MIX_W // HEAD_W
    vec = lambda a: a.reshape(1, -1)
    full = lambda width: pl.BlockSpec((1, width), lambda h, p, *_: (0, 0))
    return pl.pallas_call(
        functools.partial(_diff_attn_kernel, tq=tq, lam_init=lam_init),
        out_shape=jax.ShapeDtypeStruct((n, MIX_W), F32),
        grid_spec=pltpu.PrefetchScalarGridSpec(
            num_scalar_prefetch=4,
            grid=(n_heads, len(qb)),
            in_specs=[
                pl.BlockSpec((tq, HEAD_W), lambda h, p, qb, kb, f, l: (qb[p], h)),
                pl.BlockSpec((tk, HEAD_W), lambda h, p, qb, kb, f, l: (kb[p], h)),
                pl.BlockSpec((tk, HEAD_W), lambda h, p, qb, kb, f, l: (kb[p], h)),
                full(D_QK), full(D_QK), full(D_QK), full(D_QK), full(HEAD_W),
            ],
            out_specs=pl.BlockSpec((tq, HEAD_W), lambda h, p, qb, kb, f, l: (qb[p], h)),
            scratch_shapes=[pltpu.VMEM((2 * tq, HEAD_W), BF), pltpu.VMEM((2 * tq, 1), F32),
                            pltpu.VMEM((2 * tq, 1), F32), pltpu.VMEM((2 * tq, HEAD_W), F32)],
        ),
        compiler_params=_cparams(2),
        name="diff_attn",
    )(jnp.asarray(qb), jnp.asarray(kb), jnp.asarray(first), jnp.asarray(last),
      q, k, v, vec(lq1), vec(lk1), vec(lq2), vec(lk2), vec(subln))


def _mix1_route_kernel(x_ref, c_ref, a_ref, wo_ref, g_ref, wr_ref,
                       x3_ref, h_ref, eidx_ref, gate_ref, rank_ref, cnt_ref, base_ref, *, tm):
    i = pl.program_id(0)

    @pl.when(i == 0)
    def _():
        base_ref[...] = jnp.zeros_like(base_ref)

    mix_in = jnp.concatenate([c_ref[...], a_ref[...]], axis=-1).astype(BF)
    x3 = x_ref[...] + _dot(mix_in, wo_ref[...])
    x3_ref[...] = x3
    h = _rms_rows(x3, g_ref[...])
    h_ref[...] = h
    logits = lax.dot_general(wr_ref[...], h, (((1,), (1,)), ((), ())),
                             precision=lax.Precision.HIGHEST, preferred_element_type=F32)
    eid = lax.broadcasted_iota(jnp.int32, logits.shape, 0)
    m1 = jnp.max(logits, axis=0, keepdims=True)
    i1 = jnp.min(jnp.where(logits == m1, eid, N_EXPERTS), axis=0, keepdims=True)
    rest = jnp.where(eid == i1, -jnp.inf, logits)
    m2 = jnp.max(rest, axis=0, keepdims=True)
    i2 = jnp.min(jnp.where(rest == m2, eid, N_EXPERTS), axis=0, keepdims=True)
    e2 = jnp.exp(m2 - m1)
    g1 = 1.0 / (1.0 + e2)
    g2 = e2 / (1.0 + e2)
    sel1 = eid == i1
    sel2 = eid == i2
    onehot = jnp.where(jnp.logical_or(sel1, sel2), 1.0, 0.0)
    r = lax.broadcasted_iota(jnp.int32, (tm, tm), 0)
    c = lax.broadcasted_iota(jnp.int32, (tm, tm), 1)
    incl = _dot(onehot.astype(BF), jnp.where(r <= c, 1.0, 0.0).astype(BF))
    before = base_ref[:, :1] + incl - onehot
    eidx_ref[0:1, :] = i1
    eidx_ref[1:2, :] = i2
    gate_ref[0:1, :] = g1
    gate_ref[1:2, :] = g2
    rank_ref[0:1, :] = jnp.sum(jnp.where(sel1, before, 0.0), axis=0, keepdims=True).astype(jnp.int32)
    rank_ref[1:2, :] = jnp.sum(jnp.where(sel2, before, 0.0), axis=0, keepdims=True).astype(jnp.int32)
    base_ref[...] = base_ref[...] + jnp.sum(onehot, axis=1, keepdims=True)
    cnt_ref[...] = base_ref[...].astype(jnp.int32)


def _mix1_route(x, c_out, att, w_out, g2, w_router, tm):
    n, d = x.shape
    tok = lambda width: pl.BlockSpec((tm, width), lambda i: (i, 0))
    lane_tok = pl.BlockSpec((2, tm), lambda i: (0, i))
    return pl.pallas_call(
        functools.partial(_mix1_route_kernel, tm=tm),
        out_shape=(jax.ShapeDtypeStruct((n, d), F32), jax.ShapeDtypeStruct((n, d), F32),
                   jax.ShapeDtypeStruct((2, n), jnp.int32), jax.ShapeDtypeStruct((2, n), F32),
                   jax.ShapeDtypeStruct((2, n), jnp.int32),
                   jax.ShapeDtypeStruct((N_EXPERTS, LANES), jnp.int32)),
        grid=(n // tm,),
        in_specs=[tok(d), tok(MIX_W), tok(MIX_W), _resident(w_out.shape), _resident((1, d)),
                  _resident((N_EXPERTS, d))],
        out_specs=[tok(d), tok(d), lane_tok, lane_tok, lane_tok,
                   pl.BlockSpec((N_EXPERTS, LANES), lambda i: (0, 0))],
        scratch_shapes=[pltpu.VMEM((N_EXPERTS, LANES), F32)],
        compiler_params=_cparams(1),
        name="mix1_route",
    )(x, c_out, att, w_out, g2.reshape(1, -1), w_router.T)


def _experts_kernel(be_ref, nused_ref, tok_ref, h_hbm, w1_hbm, w3_hbm, w2_hbm, ys_ref,
                    xbuf, w1_v, w3_v, w2_v, gsem, wsem, *, mb, f_chunk):
    b = pl.program_id(0)
    e = be_ref[b]
    prev_e = be_ref[jnp.maximum(b - 1, 0)]

    @pl.when(b >= nused_ref[0])
    def _():
        ys_ref[...] = jnp.zeros_like(ys_ref)

    def weight_copies():
        return (pltpu.make_async_copy(w1_hbm.at[e], w1_v, wsem.at[0]),
                pltpu.make_async_copy(w3_hbm.at[e], w3_v, wsem.at[1]),
                pltpu.make_async_copy(w2_hbm.at[e], w2_v, wsem.at[2]))

    @pl.when(b < nused_ref[0])
    def _():
        new_expert = jnp.logical_or(b == 0, e != prev_e)

        @pl.when(new_expert)
        def _():
            for cp in weight_copies():
                cp.start()

        def issue(r, carry):
            pltpu.make_async_copy(h_hbm.at[pl.ds(tok_ref[0, 0, r], 1), :],
                                  xbuf.at[pl.ds(r, 1), :], gsem).start()
            return carry

        lax.fori_loop(0, mb, issue, 0)
        pltpu.make_async_copy(h_hbm.at[pl.ds(0, mb), :], xbuf, gsem).wait()

        @pl.when(new_expert)
        def _():
            for cp in weight_copies():
                cp.wait()

        x = xbuf[...].astype(BF)
        acc = jnp.zeros((mb, D_MODEL), F32)
        d_e = w1_v.shape[1]
        for c in range(d_e // f_chunk):
            cols = slice(c * f_chunk, (c + 1) * f_chunk)
            hh = _silu(_dot(x, w1_v[:, cols])) * _dot(x, w3_v[:, cols])
            acc = acc + _dot(hh.astype(BF), w2_v[cols, :])
        ys_ref[...] = acc


def _experts(h, slot_tok, block_e, n_used, w1, w3, w2, mb):
    n_blocks = block_e.shape[0]
    d = h.shape[1]
    d_e = w1.shape[2]
    any_spec = pl.BlockSpec(memory_space=pl.ANY)
    return pl.pallas_call(
        functools.partial(_experts_kernel, mb=mb, f_chunk=512),
        out_shape=jax.ShapeDtypeStruct((n_blocks * mb, d), F32),
        grid_spec=pltpu.PrefetchScalarGridSpec(
            num_scalar_prefetch=2,
            grid=(n_blocks,),
            in_specs=[pl.BlockSpec((1, 1, mb), lambda b, be, nu: (b, 0, 0), memory_space=pltpu.SMEM),
                      any_spec, any_spec, any_spec, any_spec],
            out_specs=pl.BlockSpec((mb, d), lambda b, be, nu: (b, 0)),
            scratch_shapes=[pltpu.VMEM((mb, d), F32),
                            pltpu.VMEM((d, d_e), BF), pltpu.VMEM((d, d_e), BF), pltpu.VMEM((d_e, d), BF),
                            pltpu.SemaphoreType.DMA, pltpu.SemaphoreType.DMA((3,))],
        ),
        compiler_params=_cparams(1),
        name="experts",
    )(block_e, n_used, slot_tok.reshape(n_blocks, 1, mb), h, w1, w3, w2)


def _combine_kernel(dest_ref, x_ref, gate_ref, ys_hbm, o_ref, ybuf, sem, *, tm):
    def issue(r, carry):
        for kk in range(2):
            pltpu.make_async_copy(ys_hbm.at[pl.ds(dest_ref[0, kk, r], 1), :],
                                  ybuf.at[kk, pl.ds(r, 1), :], sem).start()
        return carry

    lax.fori_loop(0, tm, issue, 0)
    for kk in range(2):
        pltpu.make_async_copy(ys_hbm.at[pl.ds(0, tm), :], ybuf.at[kk], sem).wait()
    g = jnp.concatenate([gate_ref[...], jnp.zeros((6, tm), F32)], axis=0).T
    o_ref[...] = x_ref[...] + g[:, 0:1] * ybuf[0] + g[:, 1:2] * ybuf[1]


def _combine(x, gates, dest, ys, tm):
    n, d = x.shape
    nblk = n // tm
    dest_b = dest.reshape(2, nblk, tm).transpose(1, 0, 2)
    return pl.pallas_call(
        functools.partial(_combine_kernel, tm=tm),
        out_shape=jax.ShapeDtypeStruct((n, d), F32),
        grid=(nblk,),
        in_specs=[pl.BlockSpec((1, 2, tm), lambda i: (i, 0, 0), memory_space=pltpu.SMEM),
                  pl.BlockSpec((tm, d), lambda i: (i, 0)),
                  pl.BlockSpec((2, tm), lambda i: (0, i)),
                  pl.BlockSpec(memory_space=pl.ANY)],
        out_specs=pl.BlockSpec((tm, d), lambda i: (i, 0)),
        scratch_shapes=[pltpu.VMEM((2, tm, d), F32), pltpu.SemaphoreType.DMA],
        compiler_params=_cparams(1),
        name="combine",
    )(dest_b, x, gates, ys)


def _moe(x3, h, eidx, gates, rank, counts, w1, w3, w2, mb, tm):
    n = x3.shape[0]
    n_assign = 2 * n
    n_blocks = -(-n_assign // mb) + N_EXPERTS
    cnt = counts[:, 0]
    padded = ((cnt + mb - 1) // mb) * mb
    pends = jnp.cumsum(padded)
    pstarts = pends - padded
    dest = pstarts[eidx] + rank
    tok_ids = jnp.broadcast_to(jnp.arange(n, dtype=jnp.int32)[None, :], (2, n))
    slot_tok = jnp.zeros((n_blocks * mb,), jnp.int32).at[dest.reshape(-1)].set(tok_ids.reshape(-1))
    blk_start = jnp.arange(n_blocks, dtype=jnp.int32) * mb
    block_e = jnp.minimum(jnp.searchsorted(pends, blk_start, side="right"), N_EXPERTS - 1).astype(jnp.int32)
    n_used = (pends[-1] // mb).astype(jnp.int32).reshape(1)
    ys = _experts(h, slot_tok, block_e, n_used, w1, w3, w2, mb)
    return _combine(x3, gates, dest.astype(jnp.int32), ys, tm)


def _trunk(x, groups, p):
    bf = lambda a: a.astype(BF)
    z = _norm_proj(x, p["e_norm1"][0].reshape(1, -1), bf(p["e_w_in"][0]), tm=512)
    o_f, o_b, sgu = _hgrn_sgu(z, p["hgrn_lb"], p["e_sgu_ln_g"][0], p["e_sgu_ln_b"][0],
                              p["e_sgu_w"][0], p["e_sgu_b"][0], groups, layer=0, t=256)
    x = _mix0_ffn(x, z, o_f, o_b, sgu, p["e_hgrn_gnorm"][0], bf(p["e_w_out"][0]), p["e_norm2"][0],
                  bf(p["e_ffn_w1"][0]), bf(p["e_ffn_w3"][0]), bf(p["e_ffn_w2"][0]), tm=512)
    layer = 1
    lam_init = 0.8 - 0.6 * math.exp(-0.3 * layer)
    pz, q, k, v = _proj1(x, p["o_norm1"][0], bf(p["o_w_in"][0]), p["o_q_norm"][0], p["o_k_norm"][0],
                         groups, tm=512)
    c_out = _pool(pz, p["o_pool_w"][0], p["o_pool_scale"][0], groups, t=256)
    att = _diff_attn(q, k, v, p["o_lambda_q1"][0], p["o_lambda_k1"][0], p["o_lambda_q2"][0],
                     p["o_lambda_k2"][0], p["o_subln"][0], groups, lam_init, tq=512, tk=512)
    x3, h, eidx, gates, rank, counts = _mix1_route(x, c_out, att, bf(p["o_w_out"][0]), p["o_norm2"][0],
                                                   p["o_router"][0], tm=512)
    return _moe(x3, h, eidx, gates, rank, counts, bf(p["o_moe_w1"][0]), bf(p["o_moe_w3"][0]),
                bf(p["o_moe_w2"][0]), mb=512, tm=256)


def kernel(x_prompt, x_sample, hgrn_lb, e_norm1, e_w_in, e_hgrn_gnorm, e_sgu_ln_g, e_sgu_ln_b, e_sgu_w, e_sgu_b, e_w_out, e_norm2, e_ffn_w1, e_ffn_w3, e_ffn_w2, o_norm1, o_w_in, o_pool_w, o_pool_scale, o_q_norm, o_k_norm, o_lambda_q1, o_lambda_k1, o_lambda_q2, o_lambda_k2, o_subln, o_w_out, o_norm2, o_router, o_moe_w1, o_moe_w3, o_moe_w2):
    params = dict(
        hgrn_lb=hgrn_lb, e_norm1=e_norm1, e_w_in=e_w_in, e_hgrn_gnorm=e_hgrn_gnorm,
        e_sgu_ln_g=e_sgu_ln_g, e_sgu_ln_b=e_sgu_ln_b, e_sgu_w=e_sgu_w, e_sgu_b=e_sgu_b,
        e_w_out=e_w_out, e_norm2=e_norm2, e_ffn_w1=e_ffn_w1, e_ffn_w3=e_ffn_w3, e_ffn_w2=e_ffn_w2,
        o_norm1=o_norm1, o_w_in=o_w_in, o_pool_w=o_pool_w, o_pool_scale=o_pool_scale,
        o_q_norm=o_q_norm, o_k_norm=o_k_norm, o_lambda_q1=o_lambda_q1, o_lambda_k1=o_lambda_k1,
        o_lambda_q2=o_lambda_q2, o_lambda_k2=o_lambda_k2, o_subln=o_subln, o_w_out=o_w_out,
        o_norm2=o_norm2, o_router=o_router, o_moe_w1=o_moe_w1, o_moe_w3=o_moe_w3, o_moe_w2=o_moe_w2,
    )
    d = x_prompt.shape[-1]
    groups = (x_prompt.shape[:2], x_sample.shape[:2])
    n_p = x_prompt.shape[0] * x_prompt.shape[1]
    x = jnp.concatenate([x_prompt.reshape(-1, d), x_sample.reshape(-1, d)], axis=0)
    y = _trunk(x, groups, params)
    return (y[:n_p].reshape(x_prompt.shape), y[n_p:].reshape(x_sample.shape))
```

```python
import functools
import math

import numpy as np
import jax
import jax.numpy as jnp
from jax import lax
from jax.experimental import pallas as pl
from jax.experimental.pallas import tpu as pltpu

F32 = jnp.float32
BF = jnp.bfloat16

D_MODEL = 1024
EPS = 1e-6
LANES = 128
HEAD_W = 128
MIX_W = 512
HGRN_CHUNK = 64
SGU_CHUNK = 128
POOL_WINDOWS = (2, 4, 8, 16)
POOL_HALO = 16
D_QK = 64
Q_SCALE = D_QK ** -0.5 * math.log2(math.e)
MAX_FIXED_SHIFT = 40.0
ROPE_DIMS = 16
ROPE_THETA = 500000.0
N_EXPERTS = 8
VMEM_LIMIT = 56 * 1024 * 1024


def _cparams(n_axes, vmem=VMEM_LIMIT):
    return pltpu.CompilerParams(dimension_semantics=("arbitrary",) * n_axes, vmem_limit_bytes=vmem)


def _dot(a, b):
    return jnp.dot(a, b, preferred_element_type=F32)


def _dot_nt(a, b):
    return lax.dot_general(a, b, (((1,), (1,)), ((), ())), preferred_element_type=F32)


def _dot_tn(a, b):
    return lax.dot_general(a, b, (((0,), (0,)), ((), ())), preferred_element_type=F32)


def _split2(x):
    hi = x.astype(BF)
    lo = (x - hi.astype(F32)).astype(BF)
    return hi, lo


def _dot01(a01, x):
    hi, lo = _split2(x)
    return _dot(a01, hi) + _dot(a01, lo)


def _rms_rows(x, gain):
    ms = jnp.mean(x * x, axis=-1, keepdims=True)
    return x * lax.rsqrt(ms + EPS) * gain


def _sigmoid(x):
    return 1.0 / (1.0 + jnp.exp(-x))


def _silu(x):
    return x * _sigmoid(x)


def _block_tables(groups, t):
    starts, ends, pos = [], [], []
    for (b, s) in groups:
        n = s // t
        for _ in range(b):
            for i in range(n):
                starts.append(int(i == 0))
                ends.append(int(i == n - 1))
                pos.append(i)
    return (np.asarray(starts, np.int32), np.asarray(ends, np.int32), np.asarray(pos, np.int32))


def _attn_tables(groups, tq, tk):
    qb, kb, first, last = [], [], [], []
    off = 0
    for (b, s) in groups:
        for bi in range(b):
            base_q = (off + bi * s) // tq
            base_k = (off + bi * s) // tk
            for qi in range(s // tq):
                nk = s // tk
                for ki in range(nk):
                    qb.append(base_q + qi)
                    kb.append(base_k + ki)
                    first.append(int(ki == 0))
                    last.append(int(ki == nk - 1))
        off += b * s
    return tuple(np.asarray(a, np.int32) for a in (qb, kb, first, last))


def _norm_proj_kernel(x_ref, g_ref, w_ref, o_ref):
    h = _rms_rows(x_ref[...], g_ref[...]).astype(BF)
    o_ref[...] = _dot(h, w_ref[...])


def _norm_proj(x, gain, w, tm):
    n, d = x.shape
    e = w.shape[1]
    return pl.pallas_call(
        _norm_proj_kernel,
        out_shape=jax.ShapeDtypeStruct((n, e), F32),
        grid=(n // tm,),
        in_specs=[
            pl.BlockSpec((tm, d), lambda i: (i, 0)),
            pl.BlockSpec((1, d), lambda i: (0, 0)),
            pl.BlockSpec((d, e), lambda i: (0, 0)),
        ],
        out_specs=pl.BlockSpec((tm, e), lambda i: (i, 0)),
        compiler_params=_cparams(1),
        name="norm_proj",
    )(x, gain, w)


def _hgrn_direction(q_raw, f_raw, v, lb, state_ref, o_ref, reverse, t):
    n_chunks = t // HGRN_CHUNK
    q = _silu(q_raw)
    f = lb + (1.0 - lb) * _sigmoid(f_raw)
    k = 1.0 - f
    g = jnp.log(f)
    row = lax.broadcasted_iota(jnp.int32, (t, t), 0)
    col = lax.broadcasted_iota(jnp.int32, (t, t), 1)
    same_chunk = (row // HGRN_CHUNK) == (col // HGRN_CHUNK)
    causal = (col >= row) if reverse else (col <= row)
    keep = jnp.logical_and(same_chunk, causal)
    b = _dot01(jnp.where(keep, 1.0, 0.0).astype(BF), g)
    q_dec = (q * jnp.exp(b)).astype(BF)
    k_inv = (k * jnp.exp(-b)).astype(BF)
    v_bf = v.astype(BF)
    for h in range(MIX_W // HEAD_W):
        lanes = slice(h * HEAD_W, (h + 1) * HEAD_W)
        scores = jnp.where(keep, _dot_nt(q_dec[:, lanes], k_inv[:, lanes]), 0.0)
        o_ref[:, lanes] = _dot(scores.astype(BF), v_bf[:, lanes])
    order = range(n_chunks - 1, -1, -1) if reverse else range(n_chunks)
    for c in order:
        rows = slice(c * HGRN_CHUNK, (c + 1) * HGRN_CHUNK)
        edge = c * HGRN_CHUNK if reverse else (c + 1) * HGRN_CHUNK - 1
        b_edge = b[edge:edge + 1, :]
        k_end = (k[rows, :] * jnp.exp(b_edge - b[rows, :])).astype(BF)
        decay = jnp.exp(b_edge)
        for h in range(MIX_W // HEAD_W):
            lanes = slice(h * HEAD_W, (h + 1) * HEAD_W)
            s_t = state_ref[h]
            o_ref[rows, lanes] += _dot_nt(q_dec[rows, lanes], s_t.astype(BF))
            state_ref[h] = decay[:, lanes] * s_t + _dot_tn(v_bf[rows, lanes], k_end[:, lanes])


def _hgrn_sgu_kernel(starts_ref, ends_ref,
                     qf_ref, ff_ref, vf_ref, qb_ref, fb_ref, vb_ref, u_ref, v_ref,
                     lbp_ref, lng_ref, lnb_ref, ws_ref, bs_ref,
                     of_ref, ob_ref, sgu_ref, sf_ref, sb_ref, *, layer, t):
    i = pl.program_id(0)
    j = pl.num_programs(0) - 1 - i

    @pl.when(starts_ref[i] == 1)
    def _():
        sf_ref[...] = jnp.zeros_like(sf_ref)

    @pl.when(ends_ref[j] == 1)
    def _():
        sb_ref[...] = jnp.zeros_like(sb_ref)

    lbp = lbp_ref[...]
    e = jnp.exp(lbp - jnp.max(lbp, axis=0, keepdims=True))
    sm = e / jnp.sum(e, axis=0, keepdims=True)
    lb = jnp.sum(sm[:layer + 1, :], axis=0, keepdims=True)

    _hgrn_direction(qf_ref[...], ff_ref[...], vf_ref[...], lb, sf_ref, of_ref, False, t)
    _hgrn_direction(qb_ref[...], fb_ref[...], vb_ref[...], lb, sb_ref, ob_ref, True, t)

    u = jax.nn.gelu(u_ref[...])
    v = jax.nn.gelu(v_ref[...])
    for gi in range(MIX_W // HEAD_W):
        lanes = slice(gi * HEAD_W, (gi + 1) * HEAD_W)
        vg = v[:, lanes]
        mu = jnp.mean(vg, axis=-1, keepdims=True)
        var = jnp.mean(jnp.square(vg - mu), axis=-1, keepdims=True)
        vln = ((vg - mu) * lax.rsqrt(var + EPS) * lng_ref[:, lanes] + lnb_ref[:, lanes]).astype(BF)
        for c in range(t // SGU_CHUNK):
            rows = slice(c * SGU_CHUNK, (c + 1) * SGU_CHUNK)
            mixed = _dot(ws_ref[gi], vln[rows, :]) + bs_ref[gi]
            sgu_ref[rows, lanes] = u[rows, lanes] * mixed


def _hgrn_sgu(z, hgrn_lb, ln_g, ln_b, w_s, b_s, groups, layer, t):
    n = z.shape[0]
    nblk = n // t
    starts, ends, _ = _block_tables(groups, t)
    fwd = lambda c: pl.BlockSpec((t, MIX_W), lambda i, s, e, c=c: (i, c))
    bwd = lambda c: pl.BlockSpec((t, MIX_W), lambda i, s, e, c=c: (nblk - 1 - i, c))
    full = lambda shape: pl.BlockSpec(shape, lambda i, s, e: (0,) * len(shape))
    bs_b = jnp.broadcast_to(b_s[:, :, None], b_s.shape + (HEAD_W,)).astype(F32)
    out = jax.ShapeDtypeStruct((n, MIX_W), F32)
    return pl.pallas_call(
        functools.partial(_hgrn_sgu_kernel, layer=layer, t=t),
        out_shape=(out, out, out),
        grid_spec=pltpu.PrefetchScalarGridSpec(
            num_scalar_prefetch=2,
            grid=(nblk,),
            in_specs=[fwd(0), fwd(1), fwd(3), bwd(0), bwd(2), bwd(3), fwd(5), fwd(6),
                      full(hgrn_lb.shape), full((1, MIX_W)), full((1, MIX_W)),
                      full(w_s.shape), full(bs_b.shape)],
            out_specs=[fwd(0), bwd(0), fwd(0)],
            scratch_shapes=[pltpu.VMEM((MIX_W // HEAD_W, HEAD_W, HEAD_W), F32),
                            pltpu.VMEM((MIX_W // HEAD_W, HEAD_W, HEAD_W), F32)],
        ),
        compiler_params=_cparams(1),
        name="hgrn_sgu",
    )(jnp.asarray(starts), jnp.asarray(ends), z, z, z, z, z, z, z, z,
      hgrn_lb, ln_g.reshape(1, -1), ln_b.reshape(1, -1), w_s.astype(BF), bs_b)


def _mix0_ffn_kernel(x_ref, of_ref, ob_ref, gate_ref, sgu_ref, gn_ref, wo_ref, g2_ref,
                     w1_ref, w3_ref, w2_ref, o_ref, *, f_chunk):
    o = of_ref[...] + ob_ref[...]
    gate = gate_ref[...]
    parts = []
    for h in range(MIX_W // HEAD_W):
        lanes = slice(h * HEAD_W, (h + 1) * HEAD_W)
        parts.append(_rms_rows(o[:, lanes], gn_ref[...]) * _silu(gate[:, lanes]))
    parts.append(sgu_ref[...])
    mix_in = jnp.concatenate(parts, axis=-1).astype(BF)
    x1 = x_ref[...] + _dot(mix_in, wo_ref[...])
    h2 = _rms_rows(x1, g2_ref[...]).astype(BF)
    acc = x1
    d_ff = w1_ref.shape[1]
    for c in range(d_ff // f_chunk):
        cols = slice(c * f_chunk, (c + 1) * f_chunk)
        hh = _silu(_dot(h2, w1_ref[:, cols])) * _dot(h2, w3_ref[:, cols])
        acc = acc + _dot(hh.astype(BF), w2_ref[cols, :])
    o_ref[...] = acc


def _resident(shape):
    return pl.BlockSpec(shape, lambda i: (0,) * len(shape), pipeline_mode=pl.Buffered(1))


def _mix0_ffn(x, z, o_f, o_b, sgu, gnorm, w_out, g2, w1, w3, w2, tm):
    n, d = x.shape
    d_ff = w1.shape[1]
    tok = lambda w, c=0: pl.BlockSpec((tm, w), lambda i, c=c: (i, c))
    return pl.pallas_call(
        functools.partial(_mix0_ffn_kernel, f_chunk=d_ff // 2),
        out_shape=jax.ShapeDtypeStruct((n, d), F32),
        grid=(n // tm,),
        in_specs=[tok(d), tok(MIX_W), tok(MIX_W), tok(MIX_W, 4), tok(MIX_W),
                  _resident((1, HEAD_W)), _resident(w_out.shape), _resident((1, d)),
                  _resident(w1.shape), _resident(w3.shape), _resident(w2.shape)],
        out_specs=tok(d),
        compiler_params=_cparams(1),
        name="mix0_ffn",
    )(x, o_f, o_b, z, sgu, gnorm.reshape(1, -1), w_out, g2.reshape(1, -1), w1, w3, w2)


def _rope_tables(s_max):
    half = ROPE_DIMS // 2
    inv_freq = ROPE_THETA ** (-jnp.arange(0, ROPE_DIMS, 2, dtype=F32) / ROPE_DIMS)
    ang = jnp.arange(s_max, dtype=F32)[:, None] * inv_freq[None, :]
    cos, sin = jnp.cos(ang), jnp.sin(ang)
    ones = jnp.ones((s_max, D_QK - ROPE_DIMS), F32)
    zeros_h = jnp.zeros((s_max, half), F32)
    zeros_r = jnp.zeros((s_max, D_QK - ROPE_DIMS), F32)
    c64 = jnp.concatenate([cos, cos, ones], axis=-1)
    s_lo = jnp.concatenate([-sin, zeros_h, zeros_r], axis=-1)
    s_hi = jnp.concatenate([zeros_h, sin, zeros_r], axis=-1)
    tile = lambda a: jnp.concatenate([a, a], axis=-1)
    return tile(c64), tile(s_lo), tile(s_hi)


def _qk_norm_rope(x, gain, seg, cos, s_lo, s_hi, scale):
    half = ROPE_DIMS // 2
    sq_hi, sq_lo = _split2(x * x)
    ms = _dot(sq_hi, seg) + _dot(sq_lo, seg)
    xn = x * lax.rsqrt(ms + EPS) * gain
    up = pltpu.roll(xn, LANES - half, axis=1)
    down = pltpu.roll(xn, half, axis=1)
    return ((xn * cos + up * s_lo + down * s_hi) * scale).astype(BF)


def _proj1_kernel(pos_ref, x_ref, g_ref, w_ref, qg_ref, kg_ref, cos_ref, slo_ref, shi_ref,
                  p_ref, q_ref, k_ref, vt_ref):
    h = _rms_rows(x_ref[...], g_ref[...]).astype(BF)
    z = _dot(h, w_ref[...])
    p_ref[...] = z[:, :MIX_W]
    vt_ref[...] = z[:, 3 * MIX_W:].T.astype(BF)
    r = lax.broadcasted_iota(jnp.int32, (LANES, LANES), 0)
    c = lax.broadcasted_iota(jnp.int32, (LANES, LANES), 1)
    seg = jnp.where((r // D_QK) == (c // D_QK), 1.0 / D_QK, 0.0).astype(BF)
    cos, s_lo, s_hi = cos_ref[...], slo_ref[...], shi_ref[...]
    for hd in range(MIX_W // HEAD_W):
        lanes = slice(hd * HEAD_W, (hd + 1) * HEAD_W)
        q_ref[:, lanes] = _qk_norm_rope(z[:, MIX_W + hd * HEAD_W:MIX_W + (hd + 1) * HEAD_W],
                                        qg_ref[...], seg, cos, s_lo, s_hi, Q_SCALE)
        k_ref[:, lanes] = _qk_norm_rope(z[:, 2 * MIX_W + hd * HEAD_W:2 * MIX_W + (hd + 1) * HEAD_W],
                                        kg_ref[...], seg, cos, s_lo, s_hi, 1.0)


def _proj1(x, gain, w, q_gain, k_gain, groups, tm):
    n, d = x.shape
    e = w.shape[1]
    _, _, pos = _block_tables(groups, tm)
    s_max = max(s for _, s in groups)
    cos, s_lo, s_hi = _rope_tables(s_max)
    tok = lambda width: pl.BlockSpec((tm, width), lambda i, p: (i, 0))
    full = lambda shape: pl.BlockSpec(shape, lambda i, p: (0,) * len(shape))
    rope = pl.BlockSpec((tm, LANES), lambda i, p: (p[i], 0))
    tile2 = lambda g: jnp.concatenate([g, g]).reshape(1, LANES)
    return pl.pallas_call(
        _proj1_kernel,
        out_shape=(jax.ShapeDtypeStruct((n, MIX_W), F32), jax.ShapeDtypeStruct((n, MIX_W), BF),
                   jax.ShapeDtypeStruct((n, MIX_W), BF), jax.ShapeDtypeStruct((MIX_W, n), BF)),
        grid_spec=pltpu.PrefetchScalarGridSpec(
            num_scalar_prefetch=1,
            grid=(n // tm,),
            in_specs=[tok(d), full((1, d)), full((d, e)), full((1, LANES)), full((1, LANES)),
                      rope, rope, rope],
            out_specs=[tok(MIX_W)] * 3 + [pl.BlockSpec((MIX_W, tm), lambda i, p: (0, i))],
        ),
        compiler_params=_cparams(1),
        name="proj1_qk_rope",
    )(jnp.asarray(pos), x, gain.reshape(1, -1), w, tile2(q_gain), tile2(k_gain), cos, s_lo, s_hi)


def _pool_kernel(starts_ref, ends_ref, prev_ref, cur_ref, next_ref, wc_ref, scale_ref, o_ref, *, t):
    i = pl.program_id(0)
    is_start = starts_ref[i] == 1
    is_end = ends_ref[i] == 1
    cur = cur_ref[...]
    prev = jnp.where(is_start, 0.0, prev_ref[...])
    nxt = jnp.where(is_end, 0.0, next_ref[...])
    ext = jnp.concatenate([prev, cur, nxt], axis=0)
    ext_hi, ext_lo = _split2(ext)
    r = lax.broadcasted_iota(jnp.int32, (t, t + 2 * POOL_HALO), 0) + POOL_HALO
    c = lax.broadcasted_iota(jnp.int32, (t, t + 2 * POOL_HALO), 1)
    row = lax.broadcasted_iota(jnp.int32, (t, 1), 0)
    for gi, w in enumerate(POOL_WINDOWS):
        lanes = slice(gi * HEAD_W, (gi + 1) * HEAD_W)
        hw = w // 2
        band = jnp.where(jnp.logical_and(c >= r - hw, c < r + hw), 1.0, 0.0).astype(BF)
        win = _dot(band, ext_hi[:, lanes]) + _dot(band, ext_lo[:, lanes])
        lo_cut = jnp.where(is_start, jnp.maximum(hw - row, 0), 0)
        hi_cut = jnp.where(is_end, jnp.maximum(row + hw - t, 0), 0)
        cnt = (w - lo_cut - hi_cut).astype(F32)
        diff = win / cnt - cur[:, lanes]
        o_ref[:, lanes] = _dot(diff.astype(BF), wc_ref[gi]) * scale_ref[:, lanes]


def _pool(p, w_c, scale, groups, t):
    n = p.shape[0]
    nblk = n // t
    starts, ends, _ = _block_tables(groups, t)
    per = t // POOL_HALO
    n_halo = n // POOL_HALO
    return pl.pallas_call(
        functools.partial(_pool_kernel, t=t),
        out_shape=jax.ShapeDtypeStruct((n, MIX_W), F32),
        grid_spec=pltpu.PrefetchScalarGridSpec(
            num_scalar_prefetch=2,
            grid=(nblk,),
            in_specs=[
                pl.BlockSpec((POOL_HALO, MIX_W), lambda i, s, e: (jnp.maximum(i * per - 1, 0), 0)),
                pl.BlockSpec((t, MIX_W), lambda i, s, e: (i, 0)),
                pl.BlockSpec((POOL_HALO, MIX_W), lambda i, s, e: (jnp.minimum((i + 1) * per, n_halo - 1), 0)),
                pl.BlockSpec(w_c.shape, lambda i, s, e: (0, 0, 0)),
                pl.BlockSpec((1, MIX_W), lambda i, s, e: (0, 0)),
            ],
            out_specs=pl.BlockSpec((t, MIX_W), lambda i, s, e: (i, 0)),
        ),
        compiler_params=_cparams(1),
        name="pool",
    )(jnp.asarray(starts), jnp.asarray(ends), p, p, p, w_c.astype(BF), scale.reshape(1, -1))


def _diff_attn_kernel(qb_ref, kb_ref, first_ref, last_ref,
                      q_ref, k_ref, vt_ref, lq1_ref, lk1_ref, lq2_ref, lk2_ref, sub_ref, bound_ref,
                      o_ref, qzt_ref, v1_ref, m_ref, acc_ref, *, tq, tkc, n_chunks, unroll, lam_init):
    p = pl.program_id(1)

    @pl.when(first_ref[p] == 1)
    def _():
        qt = q_ref[...].astype(F32).T.astype(BF)
        row = lax.broadcasted_iota(jnp.int32, qt.shape, 0)
        qzt_ref[:, :tq] = jnp.where(row < D_QK, qt, jnp.zeros_like(qt))
        qzt_ref[:, tq:] = jnp.where(row >= D_QK, qt, jnp.zeros_like(qt))
        for slot in range(unroll):
            v1_ref[slot, HEAD_W:, :] = jnp.ones((HEAD_W, tkc), BF)
        m_ref[...] = jnp.full_like(m_ref, -jnp.inf)
        acc_ref[...] = jnp.zeros_like(acc_ref)

    def chunk(j, slot):
        off = pl.multiple_of(j * tkc, tkc)
        v1_ref[slot, :HEAD_W, :] = vt_ref[:, pl.ds(off, tkc)]
        st = _dot(k_ref[pl.ds(off, tkc), :], qzt_ref[...])
        m_old = m_ref[...]
        m_new = jnp.maximum(m_old, jnp.max(st, axis=0, keepdims=True))
        alpha = jnp.exp2(m_old - m_new)
        pt = jnp.exp2(st - m_new).astype(BF)
        acc_ref[...] = alpha * acc_ref[...] + _dot(v1_ref[slot], pt)
        m_ref[...] = m_new

    def chunks(i, carry):
        for slot in range(unroll):
            chunk(i * unroll + slot, slot)
        return carry

    bound = bound_ref[0, 0]
    fixed_shift_ok = bound <= MAX_FIXED_SHIFT

    def fixed_chunks(i, carry):
        total = None
        for slot in range(unroll):
            off = pl.multiple_of((i * unroll + slot) * tkc, tkc)
            v1_ref[slot, :HEAD_W, :] = vt_ref[:, pl.ds(off, tkc)]
            st = _dot(k_ref[pl.ds(off, tkc), :], qzt_ref[...])
            part = _dot(v1_ref[slot], jnp.exp2(st - bound).astype(BF))
            total = part if total is None else total + part
        acc_ref[...] += total
        return carry

    @pl.when(fixed_shift_ok)
    def _():
        lax.fori_loop(0, n_chunks // unroll, fixed_chunks, 0)

    @pl.when(jnp.logical_not(fixed_shift_ok))
    def _():
        lax.fori_loop(0, n_chunks // unroll, chunks, 0)

    @pl.when(last_ref[p] == 1)
    def _():
        lam = (jnp.exp(jnp.sum(lq1_ref[...] * lk1_ref[...], keepdims=True))
               - jnp.exp(jnp.sum(lq2_ref[...] * lk2_ref[...], keepdims=True)) + lam_init)
        acc = acc_ref[...]
        o = acc[:HEAD_W, :] / acc[HEAD_W:HEAD_W + 1, :]
        d = o[:, :tq] - lam * o[:, tq:]
        ms = jnp.mean(d * d, axis=0, keepdims=True)
        y = d * lax.rsqrt(ms + EPS) * sub_ref[...] * (1.0 - lam_init)
        o_ref[...] = y.T


def _diff_attn(q, k, vt, q_gain, k_gain, lq1, lk1, lq2, lk2, subln, groups, lam_init, tq, tkb, tkc):
    n = q.shape[0]
    bound = (1.01 * D_QK * Q_SCALE * jnp.max(jnp.abs(q_gain)) * jnp.max(jnp.abs(k_gain))).reshape(1, 1)
    qb, kb, first, last = _attn_tables(groups, tq, tkb)
    n_heads = MIX_W // HEAD_W
    vec = lambda a: a.reshape(1, -1)
    full = lambda shape: pl.BlockSpec(shape, lambda h, p, *_: (0, 0))
    sub_col = jnp.broadcast_to(subln[:, None], (HEAD_W, tq))
    n_chunks = tkb // tkc
    unroll = next(u for u in (4, 2, 1) if n_chunks % u == 0)
    return pl.pallas_call(
        functools.partial(_diff_attn_kernel, tq=tq, tkc=tkc, n_chunks=n_chunks, unroll=unroll,
                          lam_init=lam_init),
        out_shape=jax.ShapeDtypeStruct((n, MIX_W), F32),
        grid_spec=pltpu.PrefetchScalarGridSpec(
            num_scalar_prefetch=4,
            grid=(n_heads, len(qb)),
            in_specs=[
                pl.BlockSpec((tq, HEAD_W), lambda h, p, qb, kb, f, l: (qb[p], h)),
                pl.BlockSpec((tkb, HEAD_W), lambda h, p, qb, kb, f, l: (kb[p], h)),
                pl.BlockSpec((HEAD_W, tkb), lambda h, p, qb, kb, f, l: (h, kb[p])),
                full((1, D_QK)), full((1, D_QK)), full((1, D_QK)), full((1, D_QK)), full((HEAD_W, tq)),
                pl.BlockSpec(memory_space=pltpu.SMEM),
            ],
            out_specs=pl.BlockSpec((tq, HEAD_W), lambda h, p, qb, kb, f, l: (qb[p], h)),
            scratch_shapes=[pltpu.VMEM((HEAD_W, 2 * tq), BF), pltpu.VMEM((unroll, 2 * HEAD_W, tkc), BF),
                            pltpu.VMEM((1, 2 * tq), F32), pltpu.VMEM((2 * HEAD_W, 2 * tq), F32)],
        ),
        compiler_params=_cparams(2),
        name="diff_attn",
    )(jnp.asarray(qb), jnp.asarray(kb), jnp.asarray(first), jnp.asarray(last),
      q, k, vt, vec(lq1), vec(lk1), vec(lq2), vec(lk2), sub_col, bound.astype(F32))


def _mix1_route_kernel(x_ref, c_ref, a_ref, wo_ref, g_ref, wr_ref,
                       x3_ref, h_ref, eidx_ref, gate_ref, rank_ref, cnt_ref, base_ref, *, tm):
    i = pl.program_id(0)

    @pl.when(i == 0)
    def _():
        base_ref[...] = jnp.zeros_like(base_ref)

    mix_in = jnp.concatenate([c_ref[...], a_ref[...]], axis=-1).astype(BF)
    x3 = x_ref[...] + _dot(mix_in, wo_ref[...])
    x3_ref[...] = x3
    h = _rms_rows(x3, g_ref[...])
    h_ref[...] = h
    logits = lax.dot_general(wr_ref[...], h, (((1,), (1,)), ((), ())),
                             precision=lax.Precision.HIGHEST, preferred_element_type=F32)
    eid = lax.broadcasted_iota(jnp.int32, logits.shape, 0)
    m1 = jnp.max(logits, axis=0, keepdims=True)
    i1 = jnp.min(jnp.where(logits == m1, eid, N_EXPERTS), axis=0, keepdims=True)
    rest = jnp.where(eid == i1, -jnp.inf, logits)
    m2 = jnp.max(rest, axis=0, keepdims=True)
    i2 = jnp.min(jnp.where(rest == m2, eid, N_EXPERTS), axis=0, keepdims=True)
    e2 = jnp.exp(m2 - m1)
    g1 = 1.0 / (1.0 + e2)
    g2 = e2 / (1.0 + e2)
    sel1 = eid == i1
    sel2 = eid == i2
    onehot = jnp.where(jnp.logical_or(sel1, sel2), 1.0, 0.0)
    r = lax.broadcasted_iota(jnp.int32, (tm, tm), 0)
    c = lax.broadcasted_iota(jnp.int32, (tm, tm), 1)
    incl = _dot(onehot.astype(BF), jnp.where(r <= c, 1.0, 0.0).astype(BF))
    before = base_ref[:, :1] + incl - onehot
    eidx_ref[0:1, :] = i1
    eidx_ref[1:2, :] = i2
    gate_ref[0:1, :] = g1
    gate_ref[1:2, :] = g2
    rank_ref[0:1, :] = jnp.sum(jnp.where(sel1, before, 0.0), axis=0, keepdims=True).astype(jnp.int32)
    rank_ref[1:2, :] = jnp.sum(jnp.where(sel2, before, 0.0), axis=0, keepdims=True).astype(jnp.int32)
    base_ref[...] = base_ref[...] + jnp.sum(onehot, axis=1, keepdims=True)
    cnt_ref[...] = base_ref[...].astype(jnp.int32)


def _mix1_route(x, c_out, att, w_out, g2, w_router, tm):
    n, d = x.shape
    tok = lambda width: pl.BlockSpec((tm, width), lambda i: (i, 0))
    lane_tok = pl.BlockSpec((2, tm), lambda i: (0, i))
    return pl.pallas_call(
        functools.partial(_mix1_route_kernel, tm=tm),
        out_shape=(jax.ShapeDtypeStruct((n, d), F32), jax.ShapeDtypeStruct((n, d), F32),
                   jax.ShapeDtypeStruct((2, n), jnp.int32), jax.ShapeDtypeStruct((2, n), F32),
                   jax.ShapeDtypeStruct((2, n), jnp.int32),
                   jax.ShapeDtypeStruct((N_EXPERTS, LANES), jnp.int32)),
        grid=(n // tm,),
        in_specs=[tok(d), tok(MIX_W), tok(MIX_W), _resident(w_out.shape), _resident((1, d)),
                  _resident((N_EXPERTS, d))],
        out_specs=[tok(d), tok(d), lane_tok, lane_tok, lane_tok,
                   pl.BlockSpec((N_EXPERTS, LANES), lambda i: (0, 0))],
        scratch_shapes=[pltpu.VMEM((N_EXPERTS, LANES), F32)],
        compiler_params=_cparams(1),
        name="mix1_route",
    )(x, c_out, att, w_out, g2.reshape(1, -1), w_router.T)


def _experts_kernel(be_ref, nused_ref, tok_ref, h_hbm, w1_hbm, w3_hbm, w2_hbm, ys_ref,
                    xbuf, w1_v, w3_v, w2_v, gsem, wsem, *, mb, f_chunk):
    b = pl.program_id(0)
    e = be_ref[b]
    prev_e = be_ref[jnp.maximum(b - 1, 0)]

    @pl.when(b >= nused_ref[0])
    def _():
        ys_ref[...] = jnp.zeros_like(ys_ref)

    def weight_copies():
        return (pltpu.make_async_copy(w1_hbm.at[e], w1_v, wsem.at[0]),
                pltpu.make_async_copy(w3_hbm.at[e], w3_v, wsem.at[1]),
                pltpu.make_async_copy(w2_hbm.at[e], w2_v, wsem.at[2]))

    @pl.when(b < nused_ref[0])
    def _():
        new_expert = jnp.logical_or(b == 0, e != prev_e)

        @pl.when(new_expert)
        def _():
            for cp in weight_copies():
                cp.start()

        def issue(r, carry):
            pltpu.make_async_copy(h_hbm.at[pl.ds(tok_ref[0, 0, r], 1), :],
                                  xbuf.at[pl.ds(r, 1), :], gsem).start()
            return carry

        lax.fori_loop(0, mb, issue, 0)
        pltpu.make_async_copy(h_hbm.at[pl.ds(0, mb), :], xbuf, gsem).wait()

        @pl.when(new_expert)
        def _():
            for cp in weight_copies():
                cp.wait()

        x = xbuf[...].astype(BF)
        acc = jnp.zeros((mb, D_MODEL), F32)
        d_e = w1_v.shape[1]
        for c in range(d_e // f_chunk):
            cols = slice(c * f_chunk, (c + 1) * f_chunk)
            hh = _silu(_dot(x, w1_v[:, cols])) * _dot(x, w3_v[:, cols])
            acc = acc + _dot(hh.astype(BF), w2_v[cols, :])
        ys_ref[...] = acc


def _experts(h, slot_tok, block_e, n_used, w1, w3, w2, mb):
    n_blocks = block_e.shape[0]
    d = h.shape[1]
    d_e = w1.shape[2]
    any_spec = pl.BlockSpec(memory_space=pl.ANY)
    return pl.pallas_call(
        functools.partial(_experts_kernel, mb=mb, f_chunk=512),
        out_shape=jax.ShapeDtypeStruct((n_blocks * mb, d), F32),
        grid_spec=pltpu.PrefetchScalarGridSpec(
            num_scalar_prefetch=2,
            grid=(n_blocks,),
            in_specs=[pl.BlockSpec((1, 1, mb), lambda b, be, nu: (b, 0, 0), memory_space=pltpu.SMEM),
                      any_spec, any_spec, any_spec, any_spec],
            out_specs=pl.BlockSpec((mb, d), lambda b, be, nu: (b, 0)),
            scratch_shapes=[pltpu.VMEM((mb, d), F32),
                            pltpu.VMEM((d, d_e), BF), pltpu.VMEM((d, d_e), BF), pltpu.VMEM((d_e, d), BF),
                            pltpu.SemaphoreType.DMA, pltpu.SemaphoreType.DMA((3,))],
        ),
        compiler_params=_cparams(1),
        name="experts",
    )(block_e, n_used, slot_tok.reshape(n_blocks, 1, mb), h, w1, w3, w2)


def _combine_kernel(dest_ref, x_ref, gate_ref, ys_hbm, o_ref, ybuf, sem, *, tm):
    def issue(r, carry):
        for kk in range(2):
            pltpu.make_async_copy(ys_hbm.at[pl.ds(dest_ref[0, kk, r], 1), :],
                                  ybuf.at[kk, pl.ds(r, 1), :], sem).start()
        return carry

    lax.fori_loop(0, tm, issue, 0)
    for kk in range(2):
        pltpu.make_async_copy(ys_hbm.at[pl.ds(0, tm), :], ybuf.at[kk], sem).wait()
    g = jnp.concatenate([gate_ref[...], jnp.zeros((6, tm), F32)], axis=0).T
    o_ref[...] = x_ref[...] + g[:, 0:1] * ybuf[0] + g[:, 1:2] * ybuf[1]


def _combine(x, gates, dest, ys, tm):
    n, d = x.shape
    nblk = n // tm
    dest_b = dest.reshape(2, nblk, tm).transpose(1, 0, 2)
    return pl.pallas_call(
        functools.partial(_combine_kernel, tm=tm),
        out_shape=jax.ShapeDtypeStruct((n, d), F32),
        grid=(nblk,),
        in_specs=[pl.BlockSpec((1, 2, tm), lambda i: (i, 0, 0), memory_space=pltpu.SMEM),
                  pl.BlockSpec((tm, d), lambda i: (i, 0)),
                  pl.BlockSpec((2, tm), lambda i: (0, i)),
                  pl.BlockSpec(memory_space=pl.ANY)],
        out_specs=pl.BlockSpec((tm, d), lambda i: (i, 0)),
        scratch_shapes=[pltpu.VMEM((2, tm, d), F32), pltpu.SemaphoreType.DMA],
        compiler_params=_cparams(1),
        name="combine",
    )(dest_b, x, gates, ys)


def _moe(x3, h, eidx, gates, rank, counts, w1, w3, w2, mb, tm):
    n = x3.shape[0]
    n_assign = 2 * n
    n_blocks = -(-n_assign // mb) + N_EXPERTS
    cnt = counts[:, 0]
    padded = ((cnt + mb - 1) // mb) * mb
    pends = jnp.cumsum(padded)
    pstarts = pends - padded
    dest = pstarts[eidx] + rank
    tok_ids = jnp.broadcast_to(jnp.arange(n, dtype=jnp.int32)[None, :], (2, n))
    slot_tok = jnp.zeros((n_blocks * mb,), jnp.int32).at[dest.reshape(-1)].set(tok_ids.reshape(-1))
    blk_start = jnp.arange(n_blocks, dtype=jnp.int32) * mb
    block_e = jnp.minimum(jnp.searchsorted(pends, blk_start, side="right"), N_EXPERTS - 1).astype(jnp.int32)
    n_used = (pends[-1] // mb).astype(jnp.int32).reshape(1)
    ys = _experts(h, slot_tok, block_e, n_used, w1, w3, w2, mb)
    return _combine(x3, gates, dest.astype(jnp.int32), ys, tm)


def _trunk(x, groups, p):
    bf = lambda a: a.astype(BF)
    z = _norm_proj(x, p["e_norm1"][0].reshape(1, -1), bf(p["e_w_in"][0]), tm=512)
    o_f, o_b, sgu = _hgrn_sgu(z, p["hgrn_lb"], p["e_sgu_ln_g"][0], p["e_sgu_ln_b"][0],
                              p["e_sgu_w"][0], p["e_sgu_b"][0], groups, layer=0, t=256)
    x = _mix0_ffn(x, z, o_f, o_b, sgu, p["e_hgrn_gnorm"][0], bf(p["e_w_out"][0]), p["e_norm2"][0],
                  bf(p["e_ffn_w1"][0]), bf(p["e_ffn_w3"][0]), bf(p["e_ffn_w2"][0]), tm=512)
    layer = 1
    lam_init = 0.8 - 0.6 * math.exp(-0.3 * layer)
    pz, q, k, vt = _proj1(x, p["o_norm1"][0], bf(p["o_w_in"][0]), p["o_q_norm"][0], p["o_k_norm"][0],
                          groups, tm=512)
    c_out = _pool(pz, p["o_pool_w"][0], p["o_pool_scale"][0], groups, t=256)
    tkb = min(4096, math.gcd(*[s for _, s in groups]))
    att = _diff_attn(q, k, vt, p["o_q_norm"][0], p["o_k_norm"][0], p["o_lambda_q1"][0], p["o_lambda_k1"][0], p["o_lambda_q2"][0],
                     p["o_lambda_k2"][0], p["o_subln"][0], groups, lam_init, tq=512, tkb=tkb,
                     tkc=min(512, tkb))
    x3, h, eidx, gates, rank, counts = _mix1_route(x, c_out, att, bf(p["o_w_out"][0]), p["o_norm2"][0],
                                                   p["o_router"][0], tm=512)
    return _moe(x3, h, eidx, gates, rank, counts, bf(p["o_moe_w1"][0]), bf(p["o_moe_w3"][0]),
                bf(p["o_moe_w2"][0]), mb=512, tm=256)


def kernel(x_prompt, x_sample, hgrn_lb, e_norm1, e_w_in, e_hgrn_gnorm, e_sgu_ln_g, e_sgu_ln_b, e_sgu_w, e_sgu_b, e_w_out, e_norm2, e_ffn_w1, e_ffn_w3, e_ffn_w2, o_norm1, o_w_in, o_pool_w, o_pool_scale, o_q_norm, o_k_norm, o_lambda_q1, o_lambda_k1, o_lambda_q2, o_lambda_k2, o_subln, o_w_out, o_norm2, o_router, o_moe_w1, o_moe_w3, o_moe_w2):
    params = dict(
        hgrn_lb=hgrn_lb, e_norm1=e_norm1, e_w_in=e_w_in, e_hgrn_gnorm=e_hgrn_gnorm,
        e_sgu_ln_g=e_sgu_ln_g, e_sgu_ln_b=e_sgu_ln_b, e_sgu_w=e_sgu_w, e_sgu_b=e_sgu_b,
        e_w_out=e_w_out, e_norm2=e_norm2, e_ffn_w1=e_ffn_w1, e_ffn_w3=e_ffn_w3, e_ffn_w2=e_ffn_w2,
        o_norm1=o_norm1, o_w_in=o_w_in, o_pool_w=o_pool_w, o_pool_scale=o_pool_scale,
        o_q_norm=o_q_norm, o_k_norm=o_k_norm, o_lambda_q1=o_lambda_q1, o_lambda_k1=o_lambda_k1,
        o_lambda_q2=o_lambda_q2, o_lambda_k2=o_lambda_k2, o_subln=o_subln, o_w_out=o_w_out,
        o_norm2=o_norm2, o_router=o_router, o_moe_w1=o_moe_w1, o_moe_w3=o_moe_w3, o_moe_w2=o_moe_w2,
    )
    d = x_prompt.shape[-1]
    groups = (x_prompt.shape[:2], x_sample.shape[:2])
    n_p = x_prompt.shape[0] * x_prompt.shape[1]
    x = jnp.concatenate([x_prompt.reshape(-1, d), x_sample.reshape(-1, d)], axis=0)
    y = _trunk(x, groups, params)
    return (y[:n_p].reshape(x_prompt.shape), y[n_p:].reshape(x_sample.shape))
```

```python
import functools
import math

import numpy as np
import jax
import jax.numpy as jnp
from jax import lax
from jax.experimental import pallas as pl
from jax.experimental.pallas import tpu as pltpu

F32 = jnp.float32
BF = jnp.bfloat16

D_MODEL = 1024
EPS = 1e-6
LANES = 128
HEAD_W = 128
MIX_W = 512
HGRN_CHUNK = 64
SGU_CHUNK = 128
POOL_WINDOWS = (2, 4, 8, 16)
POOL_HALO = 16
D_QK = 64
Q_SCALE = D_QK ** -0.5 * math.log2(math.e)
MAX_FIXED_SHIFT = 40.0
ROPE_DIMS = 16
ROPE_THETA = 500000.0
N_EXPERTS = 8
VMEM_LIMIT = 56 * 1024 * 1024


def _cparams(n_axes, vmem=VMEM_LIMIT):
    return pltpu.CompilerParams(dimension_semantics=("arbitrary",) * n_axes, vmem_limit_bytes=vmem)


def _dot(a, b):
    return jnp.dot(a, b, preferred_element_type=F32)


def _dot_nt(a, b):
    return lax.dot_general(a, b, (((1,), (1,)), ((), ())), preferred_element_type=F32)


def _dot_tn(a, b):
    return lax.dot_general(a, b, (((0,), (0,)), ((), ())), preferred_element_type=F32)


def _split2(x):
    hi = x.astype(BF)
    lo = (x - hi.astype(F32)).astype(BF)
    return hi, lo


def _dot01(a01, x):
    hi, lo = _split2(x)
    return _dot(a01, hi) + _dot(a01, lo)


def _rms_rows(x, gain):
    ms = jnp.mean(x * x, axis=-1, keepdims=True)
    return x * lax.rsqrt(ms + EPS) * gain


def _sigmoid(x):
    return 1.0 / (1.0 + jnp.exp(-x))


def _silu(x):
    return x * _sigmoid(x)


def _block_tables(groups, t):
    starts, ends, pos = [], [], []
    for (b, s) in groups:
        n = s // t
        for _ in range(b):
            for i in range(n):
                starts.append(int(i == 0))
                ends.append(int(i == n - 1))
                pos.append(i)
    return (np.asarray(starts, np.int32), np.asarray(ends, np.int32), np.asarray(pos, np.int32))


def _attn_tables(groups, tq, tk):
    qb, kb, first, last = [], [], [], []
    off = 0
    for (b, s) in groups:
        for bi in range(b):
            base_q = (off + bi * s) // tq
            base_k = (off + bi * s) // tk
            for qi in range(s // tq):
                nk = s // tk
                for ki in range(nk):
                    qb.append(base_q + qi)
                    kb.append(base_k + ki)
                    first.append(int(ki == 0))
                    last.append(int(ki == nk - 1))
        off += b * s
    return tuple(np.asarray(a, np.int32) for a in (qb, kb, first, last))


def _pair_specs(xa, xb, tm):
    d = xa.shape[1]
    na = xa.shape[0] // tm
    return (pl.BlockSpec((tm, d), lambda i: (jnp.minimum(i, na - 1), 0)),
            pl.BlockSpec((tm, d), lambda i: (jnp.maximum(i - na, 0), 0)), na)


def _pair_block(i, na, xa_ref, xb_ref):
    return jnp.where(i < na, xa_ref[...], xb_ref[...])


def _norm_proj_kernel(xa_ref, xb_ref, g_ref, w_ref, o_ref, *, na):
    x = _pair_block(pl.program_id(0), na, xa_ref, xb_ref)
    h = _rms_rows(x, g_ref[...]).astype(BF)
    o_ref[...] = _dot(h, w_ref[...])


def _norm_proj(xa, xb, gain, w, tm):
    n, d = xa.shape[0] + xb.shape[0], xa.shape[1]
    e = w.shape[1]
    spec_a, spec_b, na = _pair_specs(xa, xb, tm)
    return pl.pallas_call(
        functools.partial(_norm_proj_kernel, na=na),
        out_shape=jax.ShapeDtypeStruct((n, e), F32),
        grid=(n // tm,),
        in_specs=[
            spec_a, spec_b,
            pl.BlockSpec((1, d), lambda i: (0, 0)),
            pl.BlockSpec((d, e), lambda i: (0, 0)),
        ],
        out_specs=pl.BlockSpec((tm, e), lambda i: (i, 0)),
        compiler_params=_cparams(1),
        name="norm_proj",
    )(xa, xb, gain, w)


def _hgrn_direction(q_raw, f_raw, v, lb, state_ref, o_ref, reverse, t):
    n_chunks = t // HGRN_CHUNK
    q = _silu(q_raw)
    f = lb + (1.0 - lb) * _sigmoid(f_raw)
    k = 1.0 - f
    g = jnp.log(f)
    row = lax.broadcasted_iota(jnp.int32, (t, t), 0)
    col = lax.broadcasted_iota(jnp.int32, (t, t), 1)
    same_chunk = (row // HGRN_CHUNK) == (col // HGRN_CHUNK)
    causal = (col >= row) if reverse else (col <= row)
    keep = jnp.logical_and(same_chunk, causal)
    b = _dot01(jnp.where(keep, 1.0, 0.0).astype(BF), g)
    q_dec = (q * jnp.exp(b)).astype(BF)
    k_inv = (k * jnp.exp(-b)).astype(BF)
    v_bf = v.astype(BF)
    for h in range(MIX_W // HEAD_W):
        lanes = slice(h * HEAD_W, (h + 1) * HEAD_W)
        scores = jnp.where(keep, _dot_nt(q_dec[:, lanes], k_inv[:, lanes]), 0.0)
        o_ref[:, lanes] = _dot(scores.astype(BF), v_bf[:, lanes])
    order = range(n_chunks - 1, -1, -1) if reverse else range(n_chunks)
    for c in order:
        rows = slice(c * HGRN_CHUNK, (c + 1) * HGRN_CHUNK)
        edge = c * HGRN_CHUNK if reverse else (c + 1) * HGRN_CHUNK - 1
        b_edge = b[edge:edge + 1, :]
        k_end = (k[rows, :] * jnp.exp(b_edge - b[rows, :])).astype(BF)
        decay = jnp.exp(b_edge)
        for h in range(MIX_W // HEAD_W):
            lanes = slice(h * HEAD_W, (h + 1) * HEAD_W)
            s_t = state_ref[h]
            o_ref[rows, lanes] += _dot_nt(q_dec[rows, lanes], s_t.astype(BF))
            state_ref[h] = decay[:, lanes] * s_t + _dot_tn(v_bf[rows, lanes], k_end[:, lanes])


def _hgrn_sgu_kernel(starts_ref, ends_ref,
                     qf_ref, ff_ref, vf_ref, qb_ref, fb_ref, vb_ref, u_ref, v_ref,
                     lbp_ref, lng_ref, lnb_ref, ws_ref, bs_ref,
                     of_ref, ob_ref, sgu_ref, sf_ref, sb_ref, *, layer, t):
    i = pl.program_id(0)
    j = pl.num_programs(0) - 1 - i

    @pl.when(starts_ref[i] == 1)
    def _():
        sf_ref[...] = jnp.zeros_like(sf_ref)

    @pl.when(ends_ref[j] == 1)
    def _():
        sb_ref[...] = jnp.zeros_like(sb_ref)

    lbp = lbp_ref[...]
    e = jnp.exp(lbp - jnp.max(lbp, axis=0, keepdims=True))
    sm = e / jnp.sum(e, axis=0, keepdims=True)
    lb = jnp.sum(sm[:layer + 1, :], axis=0, keepdims=True)

    _hgrn_direction(qf_ref[...], ff_ref[...], vf_ref[...], lb, sf_ref, of_ref, False, t)
    _hgrn_direction(qb_ref[...], fb_ref[...], vb_ref[...], lb, sb_ref, ob_ref, True, t)

    u = jax.nn.gelu(u_ref[...])
    v = jax.nn.gelu(v_ref[...])
    for gi in range(MIX_W // HEAD_W):
        lanes = slice(gi * HEAD_W, (gi + 1) * HEAD_W)
        vg = v[:, lanes]
        mu = jnp.mean(vg, axis=-1, keepdims=True)
        var = jnp.mean(jnp.square(vg - mu), axis=-1, keepdims=True)
        vln = ((vg - mu) * lax.rsqrt(var + EPS) * lng_ref[:, lanes] + lnb_ref[:, lanes]).astype(BF)
        for c in range(t // SGU_CHUNK):
            rows = slice(c * SGU_CHUNK, (c + 1) * SGU_CHUNK)
            mixed = _dot(ws_ref[gi], vln[rows, :]) + bs_ref[gi]
            sgu_ref[rows, lanes] = u[rows, lanes] * mixed


def _hgrn_sgu(z, hgrn_lb, ln_g, ln_b, w_s, b_s, groups, layer, t):
    n = z.shape[0]
    nblk = n // t
    starts, ends, _ = _block_tables(groups, t)
    fwd = lambda c: pl.BlockSpec((t, MIX_W), lambda i, s, e, c=c: (i, c))
    bwd = lambda c: pl.BlockSpec((t, MIX_W), lambda i, s, e, c=c: (nblk - 1 - i, c))
    full = lambda shape: pl.BlockSpec(shape, lambda i, s, e: (0,) * len(shape))
    bs_b = jnp.broadcast_to(b_s[:, :, None], b_s.shape + (HEAD_W,)).astype(F32)
    out = jax.ShapeDtypeStruct((n, MIX_W), F32)
    return pl.pallas_call(
        functools.partial(_hgrn_sgu_kernel, layer=layer, t=t),
        out_shape=(out, out, out),
        grid_spec=pltpu.PrefetchScalarGridSpec(
            num_scalar_prefetch=2,
            grid=(nblk,),
            in_specs=[fwd(0), fwd(1), fwd(3), bwd(0), bwd(2), bwd(3), fwd(5), fwd(6),
                      full(hgrn_lb.shape), full((1, MIX_W)), full((1, MIX_W)),
                      full(w_s.shape), full(bs_b.shape)],
            out_specs=[fwd(0), bwd(0), fwd(0)],
            scratch_shapes=[pltpu.VMEM((MIX_W // HEAD_W, HEAD_W, HEAD_W), F32),
                            pltpu.VMEM((MIX_W // HEAD_W, HEAD_W, HEAD_W), F32)],
        ),
        compiler_params=_cparams(1),
        name="hgrn_sgu",
    )(jnp.asarray(starts), jnp.asarray(ends), z, z, z, z, z, z, z, z,
      hgrn_lb, ln_g.reshape(1, -1), ln_b.reshape(1, -1), w_s.astype(BF), bs_b)


def _mix0_ffn_kernel(xa_ref, xb_ref, of_ref, ob_ref, gate_ref, sgu_ref, gn_ref, wo_ref, g2_ref,
                     w1_ref, w3_ref, w2_ref, o_ref, *, f_chunk, na):
    o = of_ref[...] + ob_ref[...]
    gate = gate_ref[...]
    parts = []
    for h in range(MIX_W // HEAD_W):
        lanes = slice(h * HEAD_W, (h + 1) * HEAD_W)
        parts.append(_rms_rows(o[:, lanes], gn_ref[...]) * _silu(gate[:, lanes]))
    parts.append(sgu_ref[...])
    mix_in = jnp.concatenate(parts, axis=-1).astype(BF)
    x1 = _pair_block(pl.program_id(0), na, xa_ref, xb_ref) + _dot(mix_in, wo_ref[...])
    h2 = _rms_rows(x1, g2_ref[...]).astype(BF)
    acc = x1
    d_ff = w1_ref.shape[1]
    for c in range(d_ff // f_chunk):
        cols = slice(c * f_chunk, (c + 1) * f_chunk)
        hh = _silu(_dot(h2, w1_ref[:, cols])) * _dot(h2, w3_ref[:, cols])
        acc = acc + _dot(hh.astype(BF), w2_ref[cols, :])
    o_ref[...] = acc


def _resident(shape):
    return pl.BlockSpec(shape, lambda i: (0,) * len(shape), pipeline_mode=pl.Buffered(1))


def _mix0_ffn(xa, xb, z, o_f, o_b, sgu, gnorm, w_out, g2, w1, w3, w2, tm):
    n, d = xa.shape[0] + xb.shape[0], xa.shape[1]
    d_ff = w1.shape[1]
    tok = lambda w, c=0: pl.BlockSpec((tm, w), lambda i, c=c: (i, c))
    spec_a, spec_b, na = _pair_specs(xa, xb, tm)
    return pl.pallas_call(
        functools.partial(_mix0_ffn_kernel, f_chunk=d_ff // 2, na=na),
        out_shape=jax.ShapeDtypeStruct((n, d), F32),
        grid=(n // tm,),
        in_specs=[spec_a, spec_b, tok(MIX_W), tok(MIX_W), tok(MIX_W, 4), tok(MIX_W),
                  _resident((1, HEAD_W)), _resident(w_out.shape), _resident((1, d)),
                  _resident(w1.shape), _resident(w3.shape), _resident(w2.shape)],
        out_specs=tok(d),
        compiler_params=_cparams(1),
        name="mix0_ffn",
    )(xa, xb, o_f, o_b, z, sgu, gnorm.reshape(1, -1), w_out, g2.reshape(1, -1), w1, w3, w2)


def _rope_tables(s_max):
    half = ROPE_DIMS // 2
    inv_freq = ROPE_THETA ** (-jnp.arange(0, ROPE_DIMS, 2, dtype=F32) / ROPE_DIMS)
    ang = jnp.arange(s_max, dtype=F32)[:, None] * inv_freq[None, :]
    cos, sin = jnp.cos(ang), jnp.sin(ang)
    ones = jnp.ones((s_max, D_QK - ROPE_DIMS), F32)
    zeros_h = jnp.zeros((s_max, half), F32)
    zeros_r = jnp.zeros((s_max, D_QK - ROPE_DIMS), F32)
    c64 = jnp.concatenate([cos, cos, ones], axis=-1)
    s_lo = jnp.concatenate([-sin, zeros_h, zeros_r], axis=-1)
    s_hi = jnp.concatenate([zeros_h, sin, zeros_r], axis=-1)
    tile = lambda a: jnp.concatenate([a, a], axis=-1)
    return tile(c64), tile(s_lo), tile(s_hi)


def _qk_norm_rope(x, gain, seg, cos, s_lo, s_hi, scale):
    half = ROPE_DIMS // 2
    sq_hi, sq_lo = _split2(x * x)
    ms = _dot(sq_hi, seg) + _dot(sq_lo, seg)
    xn = x * lax.rsqrt(ms + EPS) * gain
    up = pltpu.roll(xn, LANES - half, axis=1)
    down = pltpu.roll(xn, half, axis=1)
    return ((xn * cos + up * s_lo + down * s_hi) * scale).astype(BF)


def _proj1_kernel(pos_ref, x_ref, g_ref, w_ref, qg_ref, kg_ref, cos_ref, slo_ref, shi_ref,
                  p_ref, q_ref, k_ref, vt_ref):
    h = _rms_rows(x_ref[...], g_ref[...]).astype(BF)
    z = _dot(h, w_ref[...])
    p_ref[...] = z[:, :MIX_W]
    vt_ref[...] = z[:, 3 * MIX_W:].T.astype(BF)
    r = lax.broadcasted_iota(jnp.int32, (LANES, LANES), 0)
    c = lax.broadcasted_iota(jnp.int32, (LANES, LANES), 1)
    seg = jnp.where((r // D_QK) == (c // D_QK), 1.0 / D_QK, 0.0).astype(BF)
    cos, s_lo, s_hi = cos_ref[...], slo_ref[...], shi_ref[...]
    for hd in range(MIX_W // HEAD_W):
        lanes = slice(hd * HEAD_W, (hd + 1) * HEAD_W)
        q_ref[:, lanes] = _qk_norm_rope(z[:, MIX_W + hd * HEAD_W:MIX_W + (hd + 1) * HEAD_W],
                                        qg_ref[...], seg, cos, s_lo, s_hi, Q_SCALE)
        k_ref[:, lanes] = _qk_norm_rope(z[:, 2 * MIX_W + hd * HEAD_W:2 * MIX_W + (hd + 1) * HEAD_W],
                                        kg_ref[...], seg, cos, s_lo, s_hi, 1.0)


def _proj1(x, gain, w, q_gain, k_gain, groups, tm):
    n, d = x.shape
    e = w.shape[1]
    _, _, pos = _block_tables(groups, tm)
    s_max = max(s for _, s in groups)
    cos, s_lo, s_hi = _rope_tables(s_max)
    tok = lambda width: pl.BlockSpec((tm, width), lambda i, p: (i, 0))
    full = lambda shape: pl.BlockSpec(shape, lambda i, p: (0,) * len(shape))
    rope = pl.BlockSpec((tm, LANES), lambda i, p: (p[i], 0))
    tile2 = lambda g: jnp.concatenate([g, g]).reshape(1, LANES)
    return pl.pallas_call(
        _proj1_kernel,
        out_shape=(jax.ShapeDtypeStruct((n, MIX_W), F32), jax.ShapeDtypeStruct((n, MIX_W), BF),
                   jax.ShapeDtypeStruct((n, MIX_W), BF), jax.ShapeDtypeStruct((MIX_W, n), BF)),
        grid_spec=pltpu.PrefetchScalarGridSpec(
            num_scalar_prefetch=1,
            grid=(n // tm,),
            in_specs=[tok(d), full((1, d)), full((d, e)), full((1, LANES)), full((1, LANES)),
                      rope, rope, rope],
            out_specs=[tok(MIX_W)] * 3 + [pl.BlockSpec((MIX_W, tm), lambda i, p: (0, i))],
        ),
        compiler_params=_cparams(1),
        name="proj1_qk_rope",
    )(jnp.asarray(pos), x, gain.reshape(1, -1), w, tile2(q_gain), tile2(k_gain), cos, s_lo, s_hi)


def _pool_kernel(starts_ref, ends_ref, prev_ref, cur_ref, next_ref, wc_ref, scale_ref, o_ref, *, t):
    i = pl.program_id(0)
    is_start = starts_ref[i] == 1
    is_end = ends_ref[i] == 1
    cur = cur_ref[...]
    prev = jnp.where(is_start, 0.0, prev_ref[...])
    nxt = jnp.where(is_end, 0.0, next_ref[...])
    ext = jnp.concatenate([prev, cur, nxt], axis=0)
    ext_hi, ext_lo = _split2(ext)
    r = lax.broadcasted_iota(jnp.int32, (t, t + 2 * POOL_HALO), 0) + POOL_HALO
    c = lax.broadcasted_iota(jnp.int32, (t, t + 2 * POOL_HALO), 1)
    row = lax.broadcasted_iota(jnp.int32, (t, 1), 0)
    for gi, w in enumerate(POOL_WINDOWS):
        lanes = slice(gi * HEAD_W, (gi + 1) * HEAD_W)
        hw = w // 2
        band = jnp.where(jnp.logical_and(c >= r - hw, c < r + hw), 1.0, 0.0).astype(BF)
        win = _dot(band, ext_hi[:, lanes]) + _dot(band, ext_lo[:, lanes])
        lo_cut = jnp.where(is_start, jnp.maximum(hw - row, 0), 0)
        hi_cut = jnp.where(is_end, jnp.maximum(row + hw - t, 0), 0)
        cnt = (w - lo_cut - hi_cut).astype(F32)
        diff = win / cnt - cur[:, lanes]
        o_ref[:, lanes] = _dot(diff.astype(BF), wc_ref[gi]) * scale_ref[:, lanes]


def _pool(p, w_c, scale, groups, t):
    n = p.shape[0]
    nblk = n // t
    starts, ends, _ = _block_tables(groups, t)
    per = t // POOL_HALO
    n_halo = n // POOL_HALO
    return pl.pallas_call(
        functools.partial(_pool_kernel, t=t),
        out_shape=jax.ShapeDtypeStruct((n, MIX_W), F32),
        grid_spec=pltpu.PrefetchScalarGridSpec(
            num_scalar_prefetch=2,
            grid=(nblk,),
            in_specs=[
                pl.BlockSpec((POOL_HALO, MIX_W), lambda i, s, e: (jnp.maximum(i * per - 1, 0), 0)),
                pl.BlockSpec((t, MIX_W), lambda i, s, e: (i, 0)),
                pl.BlockSpec((POOL_HALO, MIX_W), lambda i, s, e: (jnp.minimum((i + 1) * per, n_halo - 1), 0)),
                pl.BlockSpec(w_c.shape, lambda i, s, e: (0, 0, 0)),
                pl.BlockSpec((1, MIX_W), lambda i, s, e: (0, 0)),
            ],
            out_specs=pl.BlockSpec((t, MIX_W), lambda i, s, e: (i, 0)),
        ),
        compiler_params=_cparams(1),
        name="pool",
    )(jnp.asarray(starts), jnp.asarray(ends), p, p, p, w_c.astype(BF), scale.reshape(1, -1))


def _diff_attn_kernel(qb_ref, kb_ref, first_ref, last_ref,
                      q_ref, k_ref, vt_ref, lq1_ref, lk1_ref, lq2_ref, lk2_ref, sub_ref, bound_ref,
                      o_ref, qzt_ref, v1_ref, m_ref, acc_ref, *, tq, tkc, n_chunks, unroll, lam_init):
    p = pl.program_id(1)

    @pl.when(first_ref[p] == 1)
    def _():
        qt = q_ref[...].astype(F32).T.astype(BF)
        row = lax.broadcasted_iota(jnp.int32, qt.shape, 0)
        qzt_ref[:, :tq] = jnp.where(row < D_QK, qt, jnp.zeros_like(qt))
        qzt_ref[:, tq:] = jnp.where(row >= D_QK, qt, jnp.zeros_like(qt))
        for slot in range(unroll):
            v1_ref[slot, HEAD_W:, :] = jnp.ones((HEAD_W, tkc), BF)
        m_ref[...] = jnp.full_like(m_ref, -jnp.inf)
        acc_ref[...] = jnp.zeros_like(acc_ref)

    def chunk(j, slot):
        off = pl.multiple_of(j * tkc, tkc)
        v1_ref[slot, :HEAD_W, :] = vt_ref[:, pl.ds(off, tkc)]
        st = _dot(k_ref[pl.ds(off, tkc), :], qzt_ref[...])
        m_old = m_ref[...]
        m_new = jnp.maximum(m_old, jnp.max(st, axis=0, keepdims=True))
        alpha = jnp.exp2(m_old - m_new)
        pt = jnp.exp2(st - m_new).astype(BF)
        acc_ref[...] = alpha * acc_ref[...] + _dot(v1_ref[slot], pt)
        m_ref[...] = m_new

    def chunks(i, carry):
        for slot in range(unroll):
            chunk(i * unroll + slot, slot)
        return carry

    bound = bound_ref[0, 0]
    fixed_shift_ok = bound <= MAX_FIXED_SHIFT

    def fixed_chunks(i, carry):
        total = None
        for slot in range(unroll):
            off = pl.multiple_of((i * unroll + slot) * tkc, tkc)
            v1_ref[slot, :HEAD_W, :] = vt_ref[:, pl.ds(off, tkc)]
            st = _dot(k_ref[pl.ds(off, tkc), :], qzt_ref[...])
            part = _dot(v1_ref[slot], jnp.exp2(st - bound).astype(BF))
            total = part if total is None else total + part
        acc_ref[...] += total
        return carry

    @pl.when(fixed_shift_ok)
    def _():
        lax.fori_loop(0, n_chunks // unroll, fixed_chunks, 0)

    @pl.when(jnp.logical_not(fixed_shift_ok))
    def _():
        lax.fori_loop(0, n_chunks // unroll, chunks, 0)

    @pl.when(last_ref[p] == 1)
    def _():
        lam = (jnp.exp(jnp.sum(lq1_ref[...] * lk1_ref[...], keepdims=True))
               - jnp.exp(jnp.sum(lq2_ref[...] * lk2_ref[...], keepdims=True)) + lam_init)
        acc = acc_ref[...]
        o = acc[:HEAD_W, :] / acc[HEAD_W:HEAD_W + 1, :]
        d = o[:, :tq] - lam * o[:, tq:]
        ms = jnp.mean(d * d, axis=0, keepdims=True)
        y = d * lax.rsqrt(ms + EPS) * sub_ref[...] * (1.0 - lam_init)
        o_ref[...] = y.T


def _diff_attn(q, k, vt, q_gain, k_gain, lq1, lk1, lq2, lk2, subln, groups, lam_init, tq, tkb, tkc):
    n = q.shape[0]
    bound = (1.01 * D_QK * Q_SCALE * jnp.max(jnp.abs(q_gain)) * jnp.max(jnp.abs(k_gain))).reshape(1, 1)
    qb, kb, first, last = _attn_tables(groups, tq, tkb)
    n_heads = MIX_W // HEAD_W
    vec = lambda a: a.reshape(1, -1)
    full = lambda shape: pl.BlockSpec(shape, lambda h, p, *_: (0, 0))
    sub_col = jnp.broadcast_to(subln[:, None], (HEAD_W, tq))
    n_chunks = tkb // tkc
    unroll = next(u for u in (4, 2, 1) if n_chunks % u == 0)
    return pl.pallas_call(
        functools.partial(_diff_attn_kernel, tq=tq, tkc=tkc, n_chunks=n_chunks, unroll=unroll,
                          lam_init=lam_init),
        out_shape=jax.ShapeDtypeStruct((n, MIX_W), F32),
        grid_spec=pltpu.PrefetchScalarGridSpec(
            num_scalar_prefetch=4,
            grid=(n_heads, len(qb)),
            in_specs=[
                pl.BlockSpec((tq, HEAD_W), lambda h, p, qb, kb, f, l: (qb[p], h)),
                pl.BlockSpec((tkb, HEAD_W), lambda h, p, qb, kb, f, l: (kb[p], h)),
                pl.BlockSpec((HEAD_W, tkb), lambda h, p, qb, kb, f, l: (h, kb[p])),
                full((1, D_QK)), full((1, D_QK)), full((1, D_QK)), full((1, D_QK)), full((HEAD_W, tq)),
                pl.BlockSpec(memory_space=pltpu.SMEM),
            ],
            out_specs=pl.BlockSpec((tq, HEAD_W), lambda h, p, qb, kb, f, l: (qb[p], h)),
            scratch_shapes=[pltpu.VMEM((HEAD_W, 2 * tq), BF), pltpu.VMEM((unroll, 2 * HEAD_W, tkc), BF),
                            pltpu.VMEM((1, 2 * tq), F32), pltpu.VMEM((2 * HEAD_W, 2 * tq), F32)],
        ),
        compiler_params=_cparams(2),
        name="diff_attn",
    )(jnp.asarray(qb), jnp.asarray(kb), jnp.asarray(first), jnp.asarray(last),
      q, k, vt, vec(lq1), vec(lk1), vec(lq2), vec(lk2), sub_col, bound.astype(F32))


def _mix1_route_kernel(x_ref, c_ref, a_ref, wo_ref, g_ref, wr_ref,
                       x3_ref, h_ref, eidx_ref, gate_ref, rank_ref, cnt_ref, base_ref, *, tm):
    i = pl.program_id(0)

    @pl.when(i == 0)
    def _():
        base_ref[...] = jnp.zeros_like(base_ref)

    mix_in = jnp.concatenate([c_ref[...], a_ref[...]], axis=-1).astype(BF)
    x3 = x_ref[...] + _dot(mix_in, wo_ref[...])
    x3_ref[...] = x3
    h = _rms_rows(x3, g_ref[...])
    h_ref[...] = h
    logits = lax.dot_general(wr_ref[...], h, (((1,), (1,)), ((), ())),
                             precision=lax.Precision.HIGHEST, preferred_element_type=F32)
    eid = lax.broadcasted_iota(jnp.int32, logits.shape, 0)
    m1 = jnp.max(logits, axis=0, keepdims=True)
    i1 = jnp.min(jnp.where(logits == m1, eid, N_EXPERTS), axis=0, keepdims=True)
    rest = jnp.where(eid == i1, -jnp.inf, logits)
    m2 = jnp.max(rest, axis=0, keepdims=True)
    i2 = jnp.min(jnp.where(rest == m2, eid, N_EXPERTS), axis=0, keepdims=True)
    e2 = jnp.exp(m2 - m1)
    g1 = 1.0 / (1.0 + e2)
    g2 = e2 / (1.0 + e2)
    sel1 = eid == i1
    sel2 = eid == i2
    onehot = jnp.where(jnp.logical_or(sel1, sel2), 1.0, 0.0)
    r = lax.broadcasted_iota(jnp.int32, (tm, tm), 0)
    c = lax.broadcasted_iota(jnp.int32, (tm, tm), 1)
    incl = _dot(onehot.astype(BF), jnp.where(r <= c, 1.0, 0.0).astype(BF))
    before = base_ref[:, :1] + incl - onehot
    eidx_ref[0:1, :] = i1
    eidx_ref[1:2, :] = i2
    gate_ref[0:1, :] = g1
    gate_ref[1:2, :] = g2
    rank_ref[0:1, :] = jnp.sum(jnp.where(sel1, before, 0.0), axis=0, keepdims=True).astype(jnp.int32)
    rank_ref[1:2, :] = jnp.sum(jnp.where(sel2, before, 0.0), axis=0, keepdims=True).astype(jnp.int32)
    base_ref[...] = base_ref[...] + jnp.sum(onehot, axis=1, keepdims=True)
    cnt_ref[...] = base_ref[...].astype(jnp.int32)


def _mix1_route(x, c_out, att, w_out, g2, w_router, tm):
    n, d = x.shape
    tok = lambda width: pl.BlockSpec((tm, width), lambda i: (i, 0))
    lane_tok = pl.BlockSpec((2, tm), lambda i: (0, i))
    return pl.pallas_call(
        functools.partial(_mix1_route_kernel, tm=tm),
        out_shape=(jax.ShapeDtypeStruct((n, d), F32), jax.ShapeDtypeStruct((n, d), F32),
                   jax.ShapeDtypeStruct((2, n), jnp.int32), jax.ShapeDtypeStruct((2, n), F32),
                   jax.ShapeDtypeStruct((2, n), jnp.int32),
                   jax.ShapeDtypeStruct((N_EXPERTS, LANES), jnp.int32)),
        grid=(n // tm,),
        in_specs=[tok(d), tok(MIX_W), tok(MIX_W), _resident(w_out.shape), _resident((1, d)),
                  _resident((N_EXPERTS, d))],
        out_specs=[tok(d), tok(d), lane_tok, lane_tok, lane_tok,
                   pl.BlockSpec((N_EXPERTS, LANES), lambda i: (0, 0))],
        scratch_shapes=[pltpu.VMEM((N_EXPERTS, LANES), F32)],
        compiler_params=_cparams(1),
        name="mix1_route",
    )(x, c_out, att, w_out, g2.reshape(1, -1), w_router.T)


def _experts_kernel(be_ref, nused_ref, tok0_ref, tokn_ref, outp_ref, h_hbm, w1_hbm, w3_hbm, w2_hbm, y_hbm,
                    xbuf, ybuf, w1_v, w3_v, w2_v, gsem, ssem, wsem, *, mb, f_chunk, n_rows):
    b = pl.program_id(0)
    last = pl.num_programs(0) - 1
    cur = b % 2
    oth = 1 - cur
    e = be_ref[b]
    prev_e = be_ref[jnp.maximum(b - 1, 0)]
    d_e = w1_v.shape[1]
    n_chunks = d_e // f_chunk

    def gather_row(tok_ref, r, slot):
        pltpu.make_async_copy(h_hbm.at[pl.ds(tok_ref[0, 0, r], 1), :],
                              xbuf.at[slot, pl.ds(r, 1), :], gsem.at[slot]).start()

    def scatter_row(r, slot):
        pltpu.make_async_copy(ybuf.at[slot, pl.ds(r, 1), :],
                              y_hbm.at[pl.ds(outp_ref[0, 0, r], 1), :], ssem.at[slot]).start()

    def weight_copies():
        return (pltpu.make_async_copy(w1_hbm.at[e], w1_v, wsem.at[0]),
                pltpu.make_async_copy(w3_hbm.at[e], w3_v, wsem.at[1]),
                pltpu.make_async_copy(w2_hbm.at[e], w2_v, wsem.at[2]))

    @pl.when(b == 0)
    def _():
        ybuf[...] = jnp.zeros_like(ybuf)
        zero_tail = pltpu.make_async_copy(ybuf.at[0], y_hbm.at[pl.ds(n_rows, mb), :], wsem.at[0])
        zero_tail.start()
        zero_tail.wait()
        lax.fori_loop(0, mb, lambda r, c: (gather_row(tok0_ref, r, 0), c)[1], 0)

    pltpu.make_async_copy(h_hbm.at[pl.ds(0, mb), :], xbuf.at[cur], gsem.at[cur]).wait()

    @pl.when(b < nused_ref[0])
    def _():
        new_expert = jnp.logical_or(b == 0, e != prev_e)

        @pl.when(new_expert)
        def _():
            for cp in weight_copies():
                cp.start()
            for cp in weight_copies():
                cp.wait()

        x = xbuf[cur].astype(BF)
        acc = jnp.zeros((mb, D_MODEL), F32)
        for c in range(n_chunks):
            cols = slice(c * f_chunk, (c + 1) * f_chunk)
            hh = _silu(_dot(x, w1_v[:, cols])) * _dot(x, w3_v[:, cols])
            acc = acc + _dot(hh.astype(BF), w2_v[cols, :])
            for r in range(c * mb // n_chunks, (c + 1) * mb // n_chunks):
                gather_row(tokn_ref, r, oth)
                scatter_row(r, oth)
        ybuf[cur] = acc

    @pl.when(b >= nused_ref[0])
    def _():
        ybuf[cur] = jnp.zeros((mb, D_MODEL), F32)

        @pl.when(b < last)
        def _():
            lax.fori_loop(0, mb, lambda r, c: (gather_row(tokn_ref, r, oth), c)[1], 0)

        lax.fori_loop(0, mb, lambda r, c: (scatter_row(r, oth), c)[1], 0)

    pltpu.make_async_copy(ybuf.at[oth], y_hbm.at[pl.ds(0, mb), :], ssem.at[oth]).wait()


def _experts(h, slot_tok, slot_out, block_e, n_used, w1, w3, w2, mb, n_rows):
    n_steps = block_e.shape[0]
    d = h.shape[1]
    d_e = w1.shape[2]
    any_spec = pl.BlockSpec(memory_space=pl.ANY)
    smem_blk = lambda fn: pl.BlockSpec((1, 1, mb), fn, memory_space=pltpu.SMEM)
    return pl.pallas_call(
        functools.partial(_experts_kernel, mb=mb, f_chunk=512, n_rows=n_rows),
        out_shape=jax.ShapeDtypeStruct((n_rows + mb, d), F32),
        grid_spec=pltpu.PrefetchScalarGridSpec(
            num_scalar_prefetch=2,
            grid=(n_steps,),
            in_specs=[smem_blk(lambda b, be, nu: (0, 0, 0)),
                      smem_blk(lambda b, be, nu: (jnp.minimum(b + 1, n_steps - 1), 0, 0)),
                      smem_blk(lambda b, be, nu: (b, 0, 0)),
                      any_spec, any_spec, any_spec, any_spec],
            out_specs=any_spec,
            scratch_shapes=[pltpu.VMEM((2, mb, d), F32), pltpu.VMEM((2, mb, d), F32),
                            pltpu.VMEM((d, d_e), BF), pltpu.VMEM((d, d_e), BF), pltpu.VMEM((d_e, d), BF),
                            pltpu.SemaphoreType.DMA((2,)), pltpu.SemaphoreType.DMA((2,)),
                            pltpu.SemaphoreType.DMA((3,))],
        ),
        compiler_params=_cparams(1),
        name="experts",
    )(block_e, n_used, slot_tok, slot_tok, slot_out, h, w1, w3, w2)


def _combine_kernel(x_ref, gate_ref, y0_ref, y1_ref, oa_ref, ob_ref, *, tm, na):
    i = pl.program_id(0)
    g = jnp.concatenate([gate_ref[...], jnp.zeros((6, tm), F32)], axis=0).T
    out = x_ref[...] + g[:, 0:1] * y0_ref[...] + g[:, 1:2] * y1_ref[...]

    @pl.when(i < na)
    def _():
        oa_ref[...] = out

    @pl.when(i >= na)
    def _():
        ob_ref[...] = out


def _combine(x, gates, y, n_first, tm):
    n, d = x.shape
    nblk = n // tm
    na = n_first // tm
    return pl.pallas_call(
        functools.partial(_combine_kernel, tm=tm, na=na),
        out_shape=(jax.ShapeDtypeStruct((n_first, d), F32), jax.ShapeDtypeStruct((n - n_first, d), F32)),
        grid=(nblk,),
        in_specs=[pl.BlockSpec((tm, d), lambda i: (i, 0)),
                  pl.BlockSpec((2, tm), lambda i: (0, i)),
                  pl.BlockSpec((tm, d), lambda i: (i, 0)),
                  pl.BlockSpec((tm, d), lambda i: (i + nblk, 0))],
        out_specs=[pl.BlockSpec((tm, d), lambda i: (jnp.minimum(i, na - 1), 0)),
                   pl.BlockSpec((tm, d), lambda i: (jnp.maximum(i - na, 0), 0))],
        compiler_params=_cparams(1),
        name="combine",
    )(x, gates, y, y)


def _moe(x3, h, eidx, gates, rank, counts, w1, w3, w2, n_first, mb, tm):
    n = x3.shape[0]
    n_blocks = -(-2 * n // mb) + N_EXPERTS
    cnt = counts[:, 0]
    padded = ((cnt + mb - 1) // mb) * mb
    pends = jnp.cumsum(padded)
    pstarts = pends - padded
    start_of = sum(jnp.where(eidx == e, pstarts[e], 0) for e in range(N_EXPERTS))
    dest = (start_of + rank).reshape(-1)
    tok = jnp.tile(jnp.arange(n, dtype=jnp.int32), 2)
    rows = jnp.arange(2 * n, dtype=jnp.int32)
    n_slots = (n_blocks + 1) * mb
    slot_tok = jnp.zeros((n_slots,), jnp.int32).at[dest].set(tok, unique_indices=True)
    spare = 2 * n + jnp.arange(n_slots, dtype=jnp.int32) % mb
    slot_out = spare.at[dest + mb].set(rows, unique_indices=True)
    blk_start = jnp.arange(n_blocks + 1, dtype=jnp.int32) * mb
    block_e = jnp.minimum(jnp.searchsorted(pends, blk_start, side="right"), N_EXPERTS - 1).astype(jnp.int32)
    n_used = (pends[-1] // mb).astype(jnp.int32).reshape(1)
    y = _experts(h, slot_tok.reshape(n_blocks + 1, 1, mb), slot_out.reshape(n_blocks + 1, 1, mb),
                 block_e, n_used, w1, w3, w2, mb, 2 * n)
    return _combine(x3, gates, y, n_first, tm)


def _trunk(xa, xb, groups, p):
    bf = lambda a: a.astype(BF)
    z = _norm_proj(xa, xb, p["e_norm1"][0].reshape(1, -1), bf(p["e_w_in"][0]), tm=512)
    o_f, o_b, sgu = _hgrn_sgu(z, p["hgrn_lb"], p["e_sgu_ln_g"][0], p["e_sgu_ln_b"][0],
                              p["e_sgu_w"][0], p["e_sgu_b"][0], groups, layer=0, t=256)
    x = _mix0_ffn(xa, xb, z, o_f, o_b, sgu, p["e_hgrn_gnorm"][0], bf(p["e_w_out"][0]), p["e_norm2"][0],
                  bf(p["e_ffn_w1"][0]), bf(p["e_ffn_w3"][0]), bf(p["e_ffn_w2"][0]), tm=512)
    layer = 1
    lam_init = 0.8 - 0.6 * math.exp(-0.3 * layer)
    pz, q, k, vt = _proj1(x, p["o_norm1"][0], bf(p["o_w_in"][0]), p["o_q_norm"][0], p["o_k_norm"][0],
                          groups, tm=512)
    c_out = _pool(pz, p["o_pool_w"][0], p["o_pool_scale"][0], groups, t=256)
    tkb = min(4096, math.gcd(*[s for _, s in groups]))
    att = _diff_attn(q, k, vt, p["o_q_norm"][0], p["o_k_norm"][0], p["o_lambda_q1"][0], p["o_lambda_k1"][0], p["o_lambda_q2"][0],
                     p["o_lambda_k2"][0], p["o_subln"][0], groups, lam_init, tq=512, tkb=tkb,
                     tkc=min(512, tkb))
    x3, h, eidx, gates, rank, counts = _mix1_route(x, c_out, att, bf(p["o_w_out"][0]), p["o_norm2"][0],
                                                   p["o_router"][0], tm=512)
    return _moe(x3, h, eidx, gates, rank, counts, bf(p["o_moe_w1"][0]), bf(p["o_moe_w3"][0]),
                bf(p["o_moe_w2"][0]), n_first=xa.shape[0], mb=512, tm=256)


def kernel(x_prompt, x_sample, hgrn_lb, e_norm1, e_w_in, e_hgrn_gnorm, e_sgu_ln_g, e_sgu_ln_b, e_sgu_w, e_sgu_b, e_w_out, e_norm2, e_ffn_w1, e_ffn_w3, e_ffn_w2, o_norm1, o_w_in, o_pool_w, o_pool_scale, o_q_norm, o_k_norm, o_lambda_q1, o_lambda_k1, o_lambda_q2, o_lambda_k2, o_subln, o_w_out, o_norm2, o_router, o_moe_w1, o_moe_w3, o_moe_w2):
    params = dict(
        hgrn_lb=hgrn_lb, e_norm1=e_norm1, e_w_in=e_w_in, e_hgrn_gnorm=e_hgrn_gnorm,
        e_sgu_ln_g=e_sgu_ln_g, e_sgu_ln_b=e_sgu_ln_b, e_sgu_w=e_sgu_w, e_sgu_b=e_sgu_b,
        e_w_out=e_w_out, e_norm2=e_norm2, e_ffn_w1=e_ffn_w1, e_ffn_w3=e_ffn_w3, e_ffn_w2=e_ffn_w2,
        o_norm1=o_norm1, o_w_in=o_w_in, o_pool_w=o_pool_w, o_pool_scale=o_pool_scale,
        o_q_norm=o_q_norm, o_k_norm=o_k_norm, o_lambda_q1=o_lambda_q1, o_lambda_k1=o_lambda_k1,
        o_lambda_q2=o_lambda_q2, o_lambda_k2=o_lambda_k2, o_subln=o_subln, o_w_out=o_w_out,
        o_norm2=o_norm2, o_router=o_router, o_moe_w1=o_moe_w1, o_moe_w3=o_moe_w3, o_moe_w2=o_moe_w2,
    )
    d = x_prompt.shape[-1]
    groups = (x_prompt.shape[:2], x_sample.shape[:2])
    y_p, y_s = _trunk(x_prompt.reshape(-1, d), x_sample.reshape(-1, d), groups, params)
    return (y_p.reshape(x_prompt.shape), y_s.reshape(x_sample.shape))
```

```python
import functools
import math

import numpy as np
import jax
import jax.numpy as jnp
from jax import lax
from jax.experimental import pallas as pl
from jax.experimental.pallas import tpu as pltpu

F32 = jnp.float32
BF = jnp.bfloat16

D_MODEL = 1024
EPS = 1e-6
LANES = 128
HEAD_W = 128
MIX_W = 512
HGRN_CHUNK = 64
SGU_CHUNK = 128
POOL_WINDOWS = (2, 4, 8, 16)
POOL_HALO = 16
D_QK = 64
Q_SCALE = D_QK ** -0.5 * math.log2(math.e)
MAX_FIXED_SHIFT = 40.0
ROPE_DIMS = 16
ROPE_THETA = 500000.0
N_EXPERTS = 8
VMEM_LIMIT = 56 * 1024 * 1024


def _cparams(n_axes, vmem=VMEM_LIMIT):
    return pltpu.CompilerParams(dimension_semantics=("arbitrary",) * n_axes, vmem_limit_bytes=vmem)


def _dot(a, b):
    return jnp.dot(a, b, preferred_element_type=F32)


def _dot_nt(a, b):
    return lax.dot_general(a, b, (((1,), (1,)), ((), ())), preferred_element_type=F32)


def _dot_tn(a, b):
    return lax.dot_general(a, b, (((0,), (0,)), ((), ())), preferred_element_type=F32)


def _split2(x):
    hi = x.astype(BF)
    lo = (x - hi.astype(F32)).astype(BF)
    return hi, lo


def _dot01(a01, x):
    hi, lo = _split2(x)
    return _dot(a01, hi) + _dot(a01, lo)


def _rms_rows(x, gain):
    ms = jnp.mean(x * x, axis=-1, keepdims=True)
    return x * lax.rsqrt(ms + EPS) * gain


def _sigmoid(x):
    return 1.0 / (1.0 + jnp.exp(-x))


def _silu(x):
    return x * _sigmoid(x)


def _block_tables(groups, t):
    starts, ends, pos = [], [], []
    for (b, s) in groups:
        n = s // t
        for _ in range(b):
            for i in range(n):
                starts.append(int(i == 0))
                ends.append(int(i == n - 1))
                pos.append(i)
    return (np.asarray(starts, np.int32), np.asarray(ends, np.int32), np.asarray(pos, np.int32))


def _attn_tables(groups, tq, tk):
    qb, kb, first, last = [], [], [], []
    off = 0
    for (b, s) in groups:
        for bi in range(b):
            base_q = (off + bi * s) // tq
            base_k = (off + bi * s) // tk
            for qi in range(s // tq):
                nk = s // tk
                for ki in range(nk):
                    qb.append(base_q + qi)
                    kb.append(base_k + ki)
                    first.append(int(ki == 0))
                    last.append(int(ki == nk - 1))
        off += b * s
    return tuple(np.asarray(a, np.int32) for a in (qb, kb, first, last))


def _pair_specs(xa, xb, tm):
    d = xa.shape[1]
    na = xa.shape[0] // tm
    return (pl.BlockSpec((tm, d), lambda i: (jnp.minimum(i, na - 1), 0)),
            pl.BlockSpec((tm, d), lambda i: (jnp.maximum(i - na, 0), 0)), na)


def _pair_block(i, na, xa_ref, xb_ref):
    return jnp.where(i < na, xa_ref[...], xb_ref[...])


def _norm_proj_kernel(xa_ref, xb_ref, g_ref, w_ref, o_ref, *, na):
    x = _pair_block(pl.program_id(0), na, xa_ref, xb_ref)
    h = _rms_rows(x, g_ref[...]).astype(BF)
    o_ref[...] = _dot(h, w_ref[...])


def _norm_proj(xa, xb, gain, w, tm):
    n, d = xa.shape[0] + xb.shape[0], xa.shape[1]
    e = w.shape[1]
    spec_a, spec_b, na = _pair_specs(xa, xb, tm)
    return pl.pallas_call(
        functools.partial(_norm_proj_kernel, na=na),
        out_shape=jax.ShapeDtypeStruct((n, e), F32),
        grid=(n // tm,),
        in_specs=[
            spec_a, spec_b,
            pl.BlockSpec((1, d), lambda i: (0, 0)),
            pl.BlockSpec((d, e), lambda i: (0, 0)),
        ],
        out_specs=pl.BlockSpec((tm, e), lambda i: (i, 0)),
        compiler_params=_cparams(1),
        name="norm_proj",
    )(xa, xb, gain, w)


def _hgrn_direction(q_raw, f_raw, v, lb, state_ref, o_ref, reverse, t):
    n_chunks = t // HGRN_CHUNK
    q = _silu(q_raw)
    f = lb + (1.0 - lb) * _sigmoid(f_raw)
    k = 1.0 - f
    g = jnp.log(f)
    row = lax.broadcasted_iota(jnp.int32, (t, t), 0)
    col = lax.broadcasted_iota(jnp.int32, (t, t), 1)
    same_chunk = (row // HGRN_CHUNK) == (col // HGRN_CHUNK)
    causal = (col >= row) if reverse else (col <= row)
    keep = jnp.logical_and(same_chunk, causal)
    b = _dot01(jnp.where(keep, 1.0, 0.0).astype(BF), g)
    q_dec = (q * jnp.exp(b)).astype(BF)
    k_inv = (k * jnp.exp(-b)).astype(BF)
    v_bf = v.astype(BF)
    for h in range(MIX_W // HEAD_W):
        lanes = slice(h * HEAD_W, (h + 1) * HEAD_W)
        scores = jnp.where(keep, _dot_nt(q_dec[:, lanes], k_inv[:, lanes]), 0.0)
        o_ref[:, lanes] = _dot(scores.astype(BF), v_bf[:, lanes])
    order = range(n_chunks - 1, -1, -1) if reverse else range(n_chunks)
    for c in order:
        rows = slice(c * HGRN_CHUNK, (c + 1) * HGRN_CHUNK)
        edge = c * HGRN_CHUNK if reverse else (c + 1) * HGRN_CHUNK - 1
        b_edge = b[edge:edge + 1, :]
        k_end = (k[rows, :] * jnp.exp(b_edge - b[rows, :])).astype(BF)
        decay = jnp.exp(b_edge)
        for h in range(MIX_W // HEAD_W):
            lanes = slice(h * HEAD_W, (h + 1) * HEAD_W)
            s_t = state_ref[h]
            o_ref[rows, lanes] += _dot_nt(q_dec[rows, lanes], s_t.astype(BF))
            state_ref[h] = decay[:, lanes] * s_t + _dot_tn(v_bf[rows, lanes], k_end[:, lanes])


def _hgrn_sgu_kernel(starts_ref, ends_ref,
                     qf_ref, ff_ref, vf_ref, qb_ref, fb_ref, vb_ref, u_ref, v_ref,
                     lbp_ref, lng_ref, lnb_ref, ws_ref, bs_ref,
                     of_ref, ob_ref, sgu_ref, sf_ref, sb_ref, *, layer, t):
    i = pl.program_id(0)
    j = pl.num_programs(0) - 1 - i

    @pl.when(starts_ref[i] == 1)
    def _():
        sf_ref[...] = jnp.zeros_like(sf_ref)

    @pl.when(ends_ref[j] == 1)
    def _():
        sb_ref[...] = jnp.zeros_like(sb_ref)

    lbp = lbp_ref[...]
    e = jnp.exp(lbp - jnp.max(lbp, axis=0, keepdims=True))
    sm = e / jnp.sum(e, axis=0, keepdims=True)
    lb = jnp.sum(sm[:layer + 1, :], axis=0, keepdims=True)

    _hgrn_direction(qf_ref[...], ff_ref[...], vf_ref[...], lb, sf_ref, of_ref, False, t)
    _hgrn_direction(qb_ref[...], fb_ref[...], vb_ref[...], lb, sb_ref, ob_ref, True, t)

    u = jax.nn.gelu(u_ref[...])
    v = jax.nn.gelu(v_ref[...])
    for gi in range(MIX_W // HEAD_W):
        lanes = slice(gi * HEAD_W, (gi + 1) * HEAD_W)
        vg = v[:, lanes]
        mu = jnp.mean(vg, axis=-1, keepdims=True)
        var = jnp.mean(jnp.square(vg - mu), axis=-1, keepdims=True)
        vln = ((vg - mu) * lax.rsqrt(var + EPS) * lng_ref[:, lanes] + lnb_ref[:, lanes]).astype(BF)
        for c in range(t // SGU_CHUNK):
            rows = slice(c * SGU_CHUNK, (c + 1) * SGU_CHUNK)
            mixed = _dot(ws_ref[gi], vln[rows, :]) + bs_ref[gi]
            sgu_ref[rows, lanes] = u[rows, lanes] * mixed


def _hgrn_sgu(z, hgrn_lb, ln_g, ln_b, w_s, b_s, groups, layer, t):
    n = z.shape[0]
    nblk = n // t
    starts, ends, _ = _block_tables(groups, t)
    fwd = lambda c: pl.BlockSpec((t, MIX_W), lambda i, s, e, c=c: (i, c))
    bwd = lambda c: pl.BlockSpec((t, MIX_W), lambda i, s, e, c=c: (nblk - 1 - i, c))
    full = lambda shape: pl.BlockSpec(shape, lambda i, s, e: (0,) * len(shape))
    bs_b = jnp.broadcast_to(b_s[:, :, None], b_s.shape + (HEAD_W,)).astype(F32)
    out = jax.ShapeDtypeStruct((n, MIX_W), F32)
    return pl.pallas_call(
        functools.partial(_hgrn_sgu_kernel, layer=layer, t=t),
        out_shape=(out, out, out),
        grid_spec=pltpu.PrefetchScalarGridSpec(
            num_scalar_prefetch=2,
            grid=(nblk,),
            in_specs=[fwd(0), fwd(1), fwd(3), bwd(0), bwd(2), bwd(3), fwd(5), fwd(6),
                      full(hgrn_lb.shape), full((1, MIX_W)), full((1, MIX_W)),
                      full(w_s.shape), full(bs_b.shape)],
            out_specs=[fwd(0), bwd(0), fwd(0)],
            scratch_shapes=[pltpu.VMEM((MIX_W // HEAD_W, HEAD_W, HEAD_W), F32),
                            pltpu.VMEM((MIX_W // HEAD_W, HEAD_W, HEAD_W), F32)],
        ),
        compiler_params=_cparams(1),
        name="hgrn_sgu",
    )(jnp.asarray(starts), jnp.asarray(ends), z, z, z, z, z, z, z, z,
      hgrn_lb, ln_g.reshape(1, -1), ln_b.reshape(1, -1), w_s.astype(BF), bs_b)


def _mix0_ffn_kernel(xa_ref, xb_ref, of_ref, ob_ref, gate_ref, sgu_ref, gn_ref, wo_ref, g2_ref,
                     w1_ref, w3_ref, w2_ref, o_ref, *, f_chunk, na):
    o = of_ref[...] + ob_ref[...]
    gate = gate_ref[...]
    parts = []
    for h in range(MIX_W // HEAD_W):
        lanes = slice(h * HEAD_W, (h + 1) * HEAD_W)
        parts.append(_rms_rows(o[:, lanes], gn_ref[...]) * _silu(gate[:, lanes]))
    parts.append(sgu_ref[...])
    mix_in = jnp.concatenate(parts, axis=-1).astype(BF)
    x1 = _pair_block(pl.program_id(0), na, xa_ref, xb_ref) + _dot(mix_in, wo_ref[...])
    h2 = _rms_rows(x1, g2_ref[...]).astype(BF)
    acc = x1
    d_ff = w1_ref.shape[1]
    for c in range(d_ff // f_chunk):
        cols = slice(c * f_chunk, (c + 1) * f_chunk)
        hh = _silu(_dot(h2, w1_ref[:, cols])) * _dot(h2, w3_ref[:, cols])
        acc = acc + _dot(hh.astype(BF), w2_ref[cols, :])
    o_ref[...] = acc


def _resident(shape):
    return pl.BlockSpec(shape, lambda i: (0,) * len(shape), pipeline_mode=pl.Buffered(1))


def _mix0_ffn(xa, xb, z, o_f, o_b, sgu, gnorm, w_out, g2, w1, w3, w2, tm):
    n, d = xa.shape[0] + xb.shape[0], xa.shape[1]
    d_ff = w1.shape[1]
    tok = lambda w, c=0: pl.BlockSpec((tm, w), lambda i, c=c: (i, c))
    spec_a, spec_b, na = _pair_specs(xa, xb, tm)
    return pl.pallas_call(
        functools.partial(_mix0_ffn_kernel, f_chunk=d_ff // 2, na=na),
        out_shape=jax.ShapeDtypeStruct((n, d), F32),
        grid=(n // tm,),
        in_specs=[spec_a, spec_b, tok(MIX_W), tok(MIX_W), tok(MIX_W, 4), tok(MIX_W),
                  _resident((1, HEAD_W)), _resident(w_out.shape), _resident((1, d)),
                  _resident(w1.shape), _resident(w3.shape), _resident(w2.shape)],
        out_specs=tok(d),
        compiler_params=_cparams(1),
        name="mix0_ffn",
    )(xa, xb, o_f, o_b, z, sgu, gnorm.reshape(1, -1), w_out, g2.reshape(1, -1), w1, w3, w2)


def _rope_tables(s_max):
    half = ROPE_DIMS // 2
    dim = np.arange(LANES) % D_QK
    lo = jnp.asarray(dim < half)[None, :]
    hi = jnp.asarray((dim >= half) & (dim < ROPE_DIMS))[None, :]
    inv_freq = ROPE_THETA ** (-jnp.asarray(2 * (dim % half), F32) / ROPE_DIMS)
    ang = jnp.arange(s_max, dtype=F32)[:, None] * inv_freq[None, :]
    cos, sin = jnp.cos(ang), jnp.sin(ang)
    return (jnp.where(lo | hi, cos, 1.0), jnp.where(lo, -sin, 0.0), jnp.where(hi, sin, 0.0))


def _qk_norm_rope(x, gain, seg, cos, s_lo, s_hi, scale):
    half = ROPE_DIMS // 2
    sq_hi, sq_lo = _split2(x * x)
    ms = _dot(sq_hi, seg) + _dot(sq_lo, seg)
    xn = x * lax.rsqrt(ms + EPS) * gain
    up = pltpu.roll(xn, LANES - half, axis=1)
    down = pltpu.roll(xn, half, axis=1)
    return ((xn * cos + up * s_lo + down * s_hi) * scale).astype(BF)


def _proj1_kernel(pos_ref, x_ref, g_ref, w_ref, qg_ref, kg_ref, cos_ref, slo_ref, shi_ref,
                  p_ref, q_ref, k_ref, vt_ref):
    h = _rms_rows(x_ref[...], g_ref[...]).astype(BF)
    z = _dot(h, w_ref[...])
    p_ref[...] = z[:, :MIX_W]
    vt_ref[...] = z[:, 3 * MIX_W:].T.astype(BF)
    r = lax.broadcasted_iota(jnp.int32, (LANES, LANES), 0)
    c = lax.broadcasted_iota(jnp.int32, (LANES, LANES), 1)
    seg = jnp.where((r // D_QK) == (c // D_QK), 1.0 / D_QK, 0.0).astype(BF)
    cos, s_lo, s_hi = cos_ref[...], slo_ref[...], shi_ref[...]
    for hd in range(MIX_W // HEAD_W):
        lanes = slice(hd * HEAD_W, (hd + 1) * HEAD_W)
        q_ref[:, lanes] = _qk_norm_rope(z[:, MIX_W + hd * HEAD_W:MIX_W + (hd + 1) * HEAD_W],
                                        qg_ref[...], seg, cos, s_lo, s_hi, Q_SCALE)
        k_ref[:, lanes] = _qk_norm_rope(z[:, 2 * MIX_W + hd * HEAD_W:2 * MIX_W + (hd + 1) * HEAD_W],
                                        kg_ref[...], seg, cos, s_lo, s_hi, 1.0)


def _proj1(x, gain, w, q_gain, k_gain, groups, tm):
    n, d = x.shape
    e = w.shape[1]
    _, _, pos = _block_tables(groups, tm)
    s_max = max(s for _, s in groups)
    cos, s_lo, s_hi = _rope_tables(s_max)
    tok = lambda width: pl.BlockSpec((tm, width), lambda i, p: (i, 0))
    full = lambda shape: pl.BlockSpec(shape, lambda i, p: (0,) * len(shape))
    rope = pl.BlockSpec((tm, LANES), lambda i, p: (p[i], 0))
    tile2 = lambda g: jnp.concatenate([g, g]).reshape(1, LANES)
    return pl.pallas_call(
        _proj1_kernel,
        out_shape=(jax.ShapeDtypeStruct((n, MIX_W), F32), jax.ShapeDtypeStruct((n, MIX_W), BF),
                   jax.ShapeDtypeStruct((n, MIX_W), BF), jax.ShapeDtypeStruct((MIX_W, n), BF)),
        grid_spec=pltpu.PrefetchScalarGridSpec(
            num_scalar_prefetch=1,
            grid=(n // tm,),
            in_specs=[tok(d), full((1, d)), full((d, e)), full((1, LANES)), full((1, LANES)),
                      rope, rope, rope],
            out_specs=[tok(MIX_W)] * 3 + [pl.BlockSpec((MIX_W, tm), lambda i, p: (0, i))],
        ),
        compiler_params=_cparams(1),
        name="proj1_qk_rope",
    )(jnp.asarray(pos), x, gain.reshape(1, -1), w, tile2(q_gain), tile2(k_gain), cos, s_lo, s_hi)


def _pool_kernel(starts_ref, ends_ref, prev_ref, cur_ref, next_ref, wc_ref, scale_ref, o_ref, *, t):
    i = pl.program_id(0)
    is_start = starts_ref[i] == 1
    is_end = ends_ref[i] == 1
    cur = cur_ref[...]
    prev = jnp.where(is_start, 0.0, prev_ref[...])
    nxt = jnp.where(is_end, 0.0, next_ref[...])
    ext = jnp.concatenate([prev, cur, nxt], axis=0)
    ext_hi, ext_lo = _split2(ext)
    r = lax.broadcasted_iota(jnp.int32, (t, t + 2 * POOL_HALO), 0) + POOL_HALO
    c = lax.broadcasted_iota(jnp.int32, (t, t + 2 * POOL_HALO), 1)
    row = lax.broadcasted_iota(jnp.int32, (t, 1), 0)
    for gi, w in enumerate(POOL_WINDOWS):
        lanes = slice(gi * HEAD_W, (gi + 1) * HEAD_W)
        hw = w // 2
        band = jnp.where(jnp.logical_and(c >= r - hw, c < r + hw), 1.0, 0.0).astype(BF)
        win = _dot(band, ext_hi[:, lanes]) + _dot(band, ext_lo[:, lanes])
        lo_cut = jnp.where(is_start, jnp.maximum(hw - row, 0), 0)
        hi_cut = jnp.where(is_end, jnp.maximum(row + hw - t, 0), 0)
        cnt = (w - lo_cut - hi_cut).astype(F32)
        diff = win / cnt - cur[:, lanes]
        o_ref[:, lanes] = _dot(diff.astype(BF), wc_ref[gi]) * scale_ref[:, lanes]


def _pool(p, w_c, scale, groups, t):
    n = p.shape[0]
    nblk = n // t
    starts, ends, _ = _block_tables(groups, t)
    per = t // POOL_HALO
    n_halo = n // POOL_HALO
    return pl.pallas_call(
        functools.partial(_pool_kernel, t=t),
        out_shape=jax.ShapeDtypeStruct((n, MIX_W), F32),
        grid_spec=pltpu.PrefetchScalarGridSpec(
            num_scalar_prefetch=2,
            grid=(nblk,),
            in_specs=[
                pl.BlockSpec((POOL_HALO, MIX_W), lambda i, s, e: (jnp.maximum(i * per - 1, 0), 0)),
                pl.BlockSpec((t, MIX_W), lambda i, s, e: (i, 0)),
                pl.BlockSpec((POOL_HALO, MIX_W), lambda i, s, e: (jnp.minimum((i + 1) * per, n_halo - 1), 0)),
                pl.BlockSpec(w_c.shape, lambda i, s, e: (0, 0, 0)),
                pl.BlockSpec((1, MIX_W), lambda i, s, e: (0, 0)),
            ],
            out_specs=pl.BlockSpec((t, MIX_W), lambda i, s, e: (i, 0)),
        ),
        compiler_params=_cparams(1),
        name="pool",
    )(jnp.asarray(starts), jnp.asarray(ends), p, p, p, w_c.astype(BF), scale.reshape(1, -1))


def _diff_attn_kernel(qb_ref, kb_ref, first_ref, last_ref,
                      q_ref, k_ref, vt_ref, lq1_ref, lk1_ref, lq2_ref, lk2_ref, sub_ref, bound_ref,
                      o_ref, qzt_ref, v1_ref, m_ref, acc_ref, *, tq, tkc, n_chunks, unroll, lam_init):
    p = pl.program_id(1)

    @pl.when(first_ref[p] == 1)
    def _():
        qt = q_ref[...].astype(F32).T.astype(BF)
        row = lax.broadcasted_iota(jnp.int32, qt.shape, 0)
        qzt_ref[:, :tq] = jnp.where(row < D_QK, qt, jnp.zeros_like(qt))
        qzt_ref[:, tq:] = jnp.where(row >= D_QK, qt, jnp.zeros_like(qt))
        for slot in range(unroll):
            v1_ref[slot, HEAD_W:, :] = jnp.ones((HEAD_W, tkc), BF)
        m_ref[...] = jnp.full_like(m_ref, -jnp.inf)
        acc_ref[...] = jnp.zeros_like(acc_ref)

    def chunk(j, slot):
        off = pl.multiple_of(j * tkc, tkc)
        v1_ref[slot, :HEAD_W, :] = vt_ref[:, pl.ds(off, tkc)]
        st = _dot(k_ref[pl.ds(off, tkc), :], qzt_ref[...])
        m_old = m_ref[...]
        m_new = jnp.maximum(m_old, jnp.max(st, axis=0, keepdims=True))
        alpha = jnp.exp2(m_old - m_new)
        pt = jnp.exp2(st - m_new).astype(BF)
        acc_ref[...] = alpha * acc_ref[...] + _dot(v1_ref[slot], pt)
        m_ref[...] = m_new

    def chunks(i, carry):
        for slot in range(unroll):
            chunk(i * unroll + slot, slot)
        return carry

    bound = bound_ref[0, 0]
    fixed_shift_ok = bound <= MAX_FIXED_SHIFT

    def fixed_chunks(i, carry):
        total = None
        for slot in range(unroll):
            off = pl.multiple_of((i * unroll + slot) * tkc, tkc)
            v1_ref[slot, :HEAD_W, :] = vt_ref[:, pl.ds(off, tkc)]
            st = _dot(k_ref[pl.ds(off, tkc), :], qzt_ref[...])
            part = _dot(v1_ref[slot], jnp.exp2(st - bound).astype(BF))
            total = part if total is None else total + part
        acc_ref[...] += total
        return carry

    @pl.when(fixed_shift_ok)
    def _():
        lax.fori_loop(0, n_chunks // unroll, fixed_chunks, 0)

    @pl.when(jnp.logical_not(fixed_shift_ok))
    def _():
        lax.fori_loop(0, n_chunks // unroll, chunks, 0)

    @pl.when(last_ref[p] == 1)
    def _():
        lam = (jnp.exp(jnp.sum(lq1_ref[...] * lk1_ref[...], keepdims=True))
               - jnp.exp(jnp.sum(lq2_ref[...] * lk2_ref[...], keepdims=True)) + lam_init)
        acc = acc_ref[...]
        o = acc[:HEAD_W, :] / acc[HEAD_W:HEAD_W + 1, :]
        d = o[:, :tq] - lam * o[:, tq:]
        ms = jnp.mean(d * d, axis=0, keepdims=True)
        y = d * lax.rsqrt(ms + EPS) * sub_ref[...] * (1.0 - lam_init)
        o_ref[...] = y.T


def _diff_attn(q, k, vt, q_gain, k_gain, lq1, lk1, lq2, lk2, subln, groups, lam_init, tq, tkb, tkc):
    n = q.shape[0]
    bound = (1.01 * D_QK * Q_SCALE * jnp.max(jnp.abs(q_gain)) * jnp.max(jnp.abs(k_gain))).reshape(1, 1)
    qb, kb, first, last = _attn_tables(groups, tq, tkb)
    n_heads = MIX_W // HEAD_W
    vec = lambda a: a.reshape(1, -1)
    full = lambda shape: pl.BlockSpec(shape, lambda h, p, *_: (0, 0))
    sub_col = jnp.broadcast_to(subln[:, None], (HEAD_W, tq))
    n_chunks = tkb // tkc
    unroll = next(u for u in (4, 2, 1) if n_chunks % u == 0)
    return pl.pallas_call(
        functools.partial(_diff_attn_kernel, tq=tq, tkc=tkc, n_chunks=n_chunks, unroll=unroll,
                          lam_init=lam_init),
        out_shape=jax.ShapeDtypeStruct((n, MIX_W), F32),
        grid_spec=pltpu.PrefetchScalarGridSpec(
            num_scalar_prefetch=4,
            grid=(n_heads, len(qb)),
            in_specs=[
                pl.BlockSpec((tq, HEAD_W), lambda h, p, qb, kb, f, l: (qb[p], h)),
                pl.BlockSpec((tkb, HEAD_W), lambda h, p, qb, kb, f, l: (kb[p], h)),
                pl.BlockSpec((HEAD_W, tkb), lambda h, p, qb, kb, f, l: (h, kb[p])),
                full((1, D_QK)), full((1, D_QK)), full((1, D_QK)), full((1, D_QK)), full((HEAD_W, tq)),
                pl.BlockSpec(memory_space=pltpu.SMEM),
            ],
            out_specs=pl.BlockSpec((tq, HEAD_W), lambda h, p, qb, kb, f, l: (qb[p], h)),
            scratch_shapes=[pltpu.VMEM((HEAD_W, 2 * tq), BF), pltpu.VMEM((unroll, 2 * HEAD_W, tkc), BF),
                            pltpu.VMEM((1, 2 * tq), F32), pltpu.VMEM((2 * HEAD_W, 2 * tq), F32)],
        ),
        compiler_params=_cparams(2),
        name="diff_attn",
    )(jnp.asarray(qb), jnp.asarray(kb), jnp.asarray(first), jnp.asarray(last),
      q, k, vt, vec(lq1), vec(lk1), vec(lq2), vec(lk2), sub_col, bound.astype(F32))


def _mix1_route_kernel(x_ref, c_ref, a_ref, wo_ref, g_ref, wr_ref,
                       x3_ref, h_ref, eidx_ref, gate_ref, rank_ref, cnt_ref, base_ref, *, tm):
    i = pl.program_id(0)

    @pl.when(i == 0)
    def _():
        base_ref[...] = jnp.zeros_like(base_ref)

    mix_in = jnp.concatenate([c_ref[...], a_ref[...]], axis=-1).astype(BF)
    x3 = x_ref[...] + _dot(mix_in, wo_ref[...])
    x3_ref[...] = x3
    h = _rms_rows(x3, g_ref[...])
    h_ref[...] = h
    wr = wr_ref[...]
    wr_hi = wr.astype(BF).astype(F32)
    wr_parts = jnp.concatenate([wr_hi, wr - wr_hi], axis=0).astype(BF)
    h_hi, h_lo = _split2(h)
    by_hi = _dot_nt(wr_parts, h_hi)
    logits = by_hi[:N_EXPERTS] + by_hi[N_EXPERTS:] + _dot_nt(wr_hi.astype(BF), h_lo)
    eid = lax.broadcasted_iota(jnp.int32, logits.shape, 0)
    m1 = jnp.max(logits, axis=0, keepdims=True)
    i1 = jnp.min(jnp.where(logits == m1, eid, N_EXPERTS), axis=0, keepdims=True)
    rest = jnp.where(eid == i1, -jnp.inf, logits)
    m2 = jnp.max(rest, axis=0, keepdims=True)
    i2 = jnp.min(jnp.where(rest == m2, eid, N_EXPERTS), axis=0, keepdims=True)
    e2 = jnp.exp(m2 - m1)
    g1 = 1.0 / (1.0 + e2)
    g2 = e2 / (1.0 + e2)
    sel1 = eid == i1
    sel2 = eid == i2
    onehot = jnp.where(jnp.logical_or(sel1, sel2), 1.0, 0.0)
    r = lax.broadcasted_iota(jnp.int32, (tm, tm), 0)
    c = lax.broadcasted_iota(jnp.int32, (tm, tm), 1)
    incl = _dot(onehot.astype(BF), jnp.where(r <= c, 1.0, 0.0).astype(BF))
    before = base_ref[:, :1] + incl - onehot
    eidx_ref[0:1, :] = i1
    eidx_ref[1:2, :] = i2
    gate_ref[0:1, :] = g1
    gate_ref[1:2, :] = g2
    rank_ref[0:1, :] = jnp.sum(jnp.where(sel1, before, 0.0), axis=0, keepdims=True).astype(jnp.int32)
    rank_ref[1:2, :] = jnp.sum(jnp.where(sel2, before, 0.0), axis=0, keepdims=True).astype(jnp.int32)
    base_ref[...] = base_ref[...] + jnp.sum(onehot, axis=1, keepdims=True)
    cnt_ref[...] = base_ref[...].astype(jnp.int32)


def _mix1_route(x, c_out, att, w_out, g2, w_router, tm):
    n, d = x.shape
    tok = lambda width: pl.BlockSpec((tm, width), lambda i: (i, 0))
    lane_tok = pl.BlockSpec((2, tm), lambda i: (0, i))
    return pl.pallas_call(
        functools.partial(_mix1_route_kernel, tm=tm),
        out_shape=(jax.ShapeDtypeStruct((n, d), F32), jax.ShapeDtypeStruct((n, d), F32),
                   jax.ShapeDtypeStruct((2, n), jnp.int32), jax.ShapeDtypeStruct((2, n), F32),
                   jax.ShapeDtypeStruct((2, n), jnp.int32),
                   jax.ShapeDtypeStruct((N_EXPERTS, LANES), jnp.int32)),
        grid=(n // tm,),
        in_specs=[tok(d), tok(MIX_W), tok(MIX_W), _resident(w_out.shape), _resident((1, d)),
                  _resident((N_EXPERTS, d))],
        out_specs=[tok(d), tok(d), lane_tok, lane_tok, lane_tok,
                   pl.BlockSpec((N_EXPERTS, LANES), lambda i: (0, 0))],
        scratch_shapes=[pltpu.VMEM((N_EXPERTS, LANES), F32)],
        compiler_params=_cparams(1),
        name="mix1_route",
    )(x, c_out, att, w_out, g2.reshape(1, -1), w_router.T)


def _experts_kernel(be_ref, nused_ref, row0_ref, rown_ref, rowp_ref, h_hbm, w1_hbm, w3_hbm, w2_hbm, y_hbm,
                    xbuf, ybuf, w1_v, w3_v, w2_v, gsem, ssem, wsem, *, mb, f_chunk, n_rows):
    b = pl.program_id(0)
    last = pl.num_programs(0) - 1
    cur = b % 2
    oth = 1 - cur
    e = be_ref[b]
    prev_e = be_ref[jnp.maximum(b - 1, 0)]
    d_e = w1_v.shape[1]
    n_chunks = d_e // f_chunk

    n_tok = n_rows // 2

    def gather_row(row_ref, r, slot):
        v = row_ref[0, 0, r]
        tok = v - jnp.where(v >= n_rows, n_rows, jnp.where(v >= n_tok, n_tok, 0))
        pltpu.make_async_copy(h_hbm.at[pl.ds(tok, 1), :],
                              xbuf.at[slot, pl.ds(r, 1), :], gsem.at[slot]).start()

    def scatter_row(r, slot):
        pltpu.make_async_copy(ybuf.at[slot, pl.ds(r, 1), :],
                              y_hbm.at[pl.ds(rowp_ref[0, 0, r], 1), :], ssem.at[slot]).start()

    def weight_copies():
        return (pltpu.make_async_copy(w1_hbm.at[e], w1_v, wsem.at[0]),
                pltpu.make_async_copy(w3_hbm.at[e], w3_v, wsem.at[1]),
                pltpu.make_async_copy(w2_hbm.at[e], w2_v, wsem.at[2]))

    @pl.when(b == 0)
    def _():
        ybuf[...] = jnp.zeros_like(ybuf)
        zero_tail = pltpu.make_async_copy(ybuf.at[0], y_hbm.at[pl.ds(n_rows, mb), :], wsem.at[0])
        zero_tail.start()
        zero_tail.wait()
        lax.fori_loop(0, mb, lambda r, c: (gather_row(row0_ref, r, 0), c)[1], 0)

    pltpu.make_async_copy(h_hbm.at[pl.ds(0, mb), :], xbuf.at[cur], gsem.at[cur]).wait()

    @pl.when(b < nused_ref[0])
    def _():
        new_expert = jnp.logical_or(b == 0, e != prev_e)

        @pl.when(new_expert)
        def _():
            for cp in weight_copies():
                cp.start()
            for cp in weight_copies():
                cp.wait()

        x = xbuf[cur].astype(BF)
        acc = jnp.zeros((mb, D_MODEL), F32)
        for c in range(n_chunks):
            cols = slice(c * f_chunk, (c + 1) * f_chunk)
            hh = _silu(_dot(x, w1_v[:, cols])) * _dot(x, w3_v[:, cols])
            acc = acc + _dot(hh.astype(BF), w2_v[cols, :])
            n_dma = n_chunks - 1
            for r in range(min(c, n_dma) * mb // n_dma, min(c + 1, n_dma) * mb // n_dma):
                gather_row(rown_ref, r, oth)
                scatter_row(r, oth)
        ybuf[cur] = acc

    @pl.when(b >= nused_ref[0])
    def _():
        ybuf[cur] = jnp.zeros((mb, D_MODEL), F32)

        @pl.when(b < last)
        def _():
            lax.fori_loop(0, mb, lambda r, c: (gather_row(rown_ref, r, oth), c)[1], 0)

        lax.fori_loop(0, mb, lambda r, c: (scatter_row(r, oth), c)[1], 0)

    pltpu.make_async_copy(ybuf.at[oth], y_hbm.at[pl.ds(0, mb), :], ssem.at[oth]).wait()


def _experts(h, slot_row, block_e, n_used, w1, w3, w2, mb, n_rows):
    n_steps = block_e.shape[0]
    d = h.shape[1]
    d_e = w1.shape[2]
    any_spec = pl.BlockSpec(memory_space=pl.ANY)
    smem_blk = lambda fn: pl.BlockSpec((1, 1, mb), fn, memory_space=pltpu.SMEM)
    return pl.pallas_call(
        functools.partial(_experts_kernel, mb=mb, f_chunk=512, n_rows=n_rows),
        out_shape=jax.ShapeDtypeStruct((n_rows + mb, d), F32),
        grid_spec=pltpu.PrefetchScalarGridSpec(
            num_scalar_prefetch=2,
            grid=(n_steps,),
            in_specs=[smem_blk(lambda b, be, nu: (0, 0, 0)),
                      smem_blk(lambda b, be, nu: (jnp.minimum(b + 1, n_steps - 1), 0, 0)),
                      smem_blk(lambda b, be, nu: (jnp.maximum(b - 1, 0), 0, 0)),
                      any_spec, any_spec, any_spec, any_spec],
            out_specs=any_spec,
            scratch_shapes=[pltpu.VMEM((2, mb, d), F32), pltpu.VMEM((2, mb, d), F32),
                            pltpu.VMEM((d, d_e), BF), pltpu.VMEM((d, d_e), BF), pltpu.VMEM((d_e, d), BF),
                            pltpu.SemaphoreType.DMA((2,)), pltpu.SemaphoreType.DMA((2,)),
                            pltpu.SemaphoreType.DMA((3,))],
        ),
        compiler_params=_cparams(1),
        name="experts",
    )(block_e, n_used, slot_row, slot_row, slot_row, h, w1, w3, w2)


def _combine_kernel(x_ref, gate_ref, y0_ref, y1_ref, oa_ref, ob_ref, *, tm, na):
    i = pl.program_id(0)
    g = jnp.concatenate([gate_ref[...], jnp.zeros((6, tm), F32)], axis=0).T
    out = x_ref[...] + g[:, 0:1] * y0_ref[...] + g[:, 1:2] * y1_ref[...]

    @pl.when(i < na)
    def _():
        oa_ref[...] = out

    @pl.when(i >= na)
    def _():
        ob_ref[...] = out


def _combine(x, gates, y, n_first, tm):
    n, d = x.shape
    nblk = n // tm
    na = n_first // tm
    return pl.pallas_call(
        functools.partial(_combine_kernel, tm=tm, na=na),
        out_shape=(jax.ShapeDtypeStruct((n_first, d), F32), jax.ShapeDtypeStruct((n - n_first, d), F32)),
        grid=(nblk,),
        in_specs=[pl.BlockSpec((tm, d), lambda i: (i, 0)),
                  pl.BlockSpec((2, tm), lambda i: (0, i)),
                  pl.BlockSpec((tm, d), lambda i: (i, 0)),
                  pl.BlockSpec((tm, d), lambda i: (i + nblk, 0))],
        out_specs=[pl.BlockSpec((tm, d), lambda i: (jnp.minimum(i, na - 1), 0)),
                   pl.BlockSpec((tm, d), lambda i: (jnp.maximum(i - na, 0), 0))],
        compiler_params=_cparams(1),
        name="combine",
    )(x, gates, y, y)


def _moe(x3, h, eidx, gates, rank, counts, w1, w3, w2, n_first, mb, tm):
    n = x3.shape[0]
    n_blocks = -(-2 * n // mb) + N_EXPERTS
    cnt = counts[:, 0]
    padded = ((cnt + mb - 1) // mb) * mb
    pends = jnp.cumsum(padded)
    pstarts = pends - padded
    start_of = sum(jnp.where(eidx == e, pstarts[e], 0) for e in range(N_EXPERTS))
    dest = (start_of + rank).reshape(-1)
    rows = jnp.arange(2 * n, dtype=jnp.int32)
    n_slots = (n_blocks + 1) * mb
    spare = 2 * n + jnp.arange(n_slots, dtype=jnp.int32) % mb
    slot_row = spare.at[dest].set(rows, unique_indices=True)
    blk_start = jnp.arange(n_blocks + 1, dtype=jnp.int32) * mb
    block_e = jnp.minimum(jnp.searchsorted(pends, blk_start, side="right"), N_EXPERTS - 1).astype(jnp.int32)
    n_used = (pends[-1] // mb).astype(jnp.int32).reshape(1)
    y = _experts(h, slot_row.reshape(n_blocks + 1, 1, mb), block_e, n_used, w1, w3, w2, mb, 2 * n)
    return _combine(x3, gates, y, n_first, tm)


def _trunk(xa, xb, groups, p):
    bf = lambda a: a.astype(BF)
    z = _norm_proj(xa, xb, p["e_norm1"][0].reshape(1, -1), bf(p["e_w_in"][0]), tm=512)
    o_f, o_b, sgu = _hgrn_sgu(z, p["hgrn_lb"], p["e_sgu_ln_g"][0], p["e_sgu_ln_b"][0],
                              p["e_sgu_w"][0], p["e_sgu_b"][0], groups, layer=0, t=256)
    x = _mix0_ffn(xa, xb, z, o_f, o_b, sgu, p["e_hgrn_gnorm"][0], bf(p["e_w_out"][0]), p["e_norm2"][0],
                  bf(p["e_ffn_w1"][0]), bf(p["e_ffn_w3"][0]), bf(p["e_ffn_w2"][0]), tm=512)
    layer = 1
    lam_init = 0.8 - 0.6 * math.exp(-0.3 * layer)
    pz, q, k, vt = _proj1(x, p["o_norm1"][0], bf(p["o_w_in"][0]), p["o_q_norm"][0], p["o_k_norm"][0],
                          groups, tm=512)
    c_out = _pool(pz, p["o_pool_w"][0], p["o_pool_scale"][0], groups, t=256)
    tkb = min(4096, math.gcd(*[s for _, s in groups]))
    att = _diff_attn(q, k, vt, p["o_q_norm"][0], p["o_k_norm"][0], p["o_lambda_q1"][0], p["o_lambda_k1"][0], p["o_lambda_q2"][0],
                     p["o_lambda_k2"][0], p["o_subln"][0], groups, lam_init, tq=min(1024, tkb), tkb=tkb,
                     tkc=min(512, tkb))
    x3, h, eidx, gates, rank, counts = _mix1_route(x, c_out, att, bf(p["o_w_out"][0]), p["o_norm2"][0],
                                                   p["o_router"][0], tm=512)
    return _moe(x3, h, eidx, gates, rank, counts, bf(p["o_moe_w1"][0]), bf(p["o_moe_w3"][0]),
                bf(p["o_moe_w2"][0]), n_first=xa.shape[0], mb=512, tm=256)


def kernel(x_prompt, x_sample, hgrn_lb, e_norm1, e_w_in, e_hgrn_gnorm, e_sgu_ln_g, e_sgu_ln_b, e_sgu_w, e_sgu_b, e_w_out, e_norm2, e_ffn_w1, e_ffn_w3, e_ffn_w2, o_norm1, o_w_in, o_pool_w, o_pool_scale, o_q_norm, o_k_norm, o_lambda_q1, o_lambda_k1, o_lambda_q2, o_lambda_k2, o_subln, o_w_out, o_norm2, o_router, o_moe_w1, o_moe_w3, o_moe_w2):
    params = dict(
        hgrn_lb=hgrn_lb, e_norm1=e_norm1, e_w_in=e_w_in, e_hgrn_gnorm=e_hgrn_gnorm,
        e_sgu_ln_g=e_sgu_ln_g, e_sgu_ln_b=e_sgu_ln_b, e_sgu_w=e_sgu_w, e_sgu_b=e_sgu_b,
        e_w_out=e_w_out, e_norm2=e_norm2, e_ffn_w1=e_ffn_w1, e_ffn_w3=e_ffn_w3, e_ffn_w2=e_ffn_w2,
        o_norm1=o_norm1, o_w_in=o_w_in, o_pool_w=o_pool_w, o_pool_scale=o_pool_scale,
        o_q_norm=o_q_norm, o_k_norm=o_k_norm, o_lambda_q1=o_lambda_q1, o_lambda_k1=o_lambda_k1,
        o_lambda_q2=o_lambda_q2, o_lambda_k2=o_lambda_k2, o_subln=o_subln, o_w_out=o_w_out,
        o_norm2=o_norm2, o_router=o_router, o_moe_w1=o_moe_w1, o_moe_w3=o_moe_w3, o_moe_w2=o_moe_w2,
    )
    d = x_prompt.shape[-1]
    groups = (x_prompt.shape[:2], x_sample.shape[:2])
    y_p, y_s = _trunk(x_prompt.reshape(-1, d), x_sample.reshape(-1, d), groups, params)
    return (y_p.reshape(x_prompt.shape), y_s.reshape(x_sample.shape))
```

```python
import functools
import math

import numpy as np
import jax
import jax.numpy as jnp
from jax import lax
from jax.experimental import pallas as pl
from jax.experimental.pallas import tpu as pltpu

F32 = jnp.float32
BF = jnp.bfloat16

D_MODEL = 1024
EPS = 1e-6
LANES = 128
HEAD_W = 128
MIX_W = 512
HGRN_CHUNK = 64
SGU_CHUNK = 128
POOL_WINDOWS = (2, 4, 8, 16)
POOL_HALO = 16
D_QK = 64
Q_SCALE = D_QK ** -0.5 * math.log2(math.e)
ONES_ROWS = 16
MAX_FIXED_SHIFT = 40.0
ROPE_DIMS = 16
ROPE_THETA = 500000.0
N_EXPERTS = 8
VMEM_LIMIT = 56 * 1024 * 1024


def _cparams(n_axes, vmem=VMEM_LIMIT):
    return pltpu.CompilerParams(dimension_semantics=("arbitrary",) * n_axes, vmem_limit_bytes=vmem)


def _dot(a, b):
    return jnp.dot(a, b, preferred_element_type=F32)


def _dot_nt(a, b):
    return lax.dot_general(a, b, (((1,), (1,)), ((), ())), preferred_element_type=F32)


def _dot_tn(a, b):
    return lax.dot_general(a, b, (((0,), (0,)), ((), ())), preferred_element_type=F32)


def _split2(x):
    hi = x.astype(BF)
    lo = (x - hi.astype(F32)).astype(BF)
    return hi, lo


def _dot01(a01, x):
    hi, lo = _split2(x)
    return _dot(a01, hi) + _dot(a01, lo)


def _rms_rows(x, gain):
    ms = jnp.mean(x * x, axis=-1, keepdims=True)
    return x * lax.rsqrt(ms + EPS) * gain


def _sigmoid(x):
    return 0.5 * jnp.tanh(0.5 * x) + 0.5


def _silu(x):
    return x * _sigmoid(x)


def _block_tables(groups, t):
    starts, ends, pos = [], [], []
    for (b, s) in groups:
        n = s // t
        for _ in range(b):
            for i in range(n):
                starts.append(int(i == 0))
                ends.append(int(i == n - 1))
                pos.append(i)
    return (np.asarray(starts, np.int32), np.asarray(ends, np.int32), np.asarray(pos, np.int32))


def _attn_tables(groups, tq, tk):
    qb, kb, first, last = [], [], [], []
    off = 0
    for (b, s) in groups:
        for bi in range(b):
            base_q = (off + bi * s) // tq
            base_k = (off + bi * s) // tk
            for qi in range(s // tq):
                nk = s // tk
                for ki in range(nk):
                    qb.append(base_q + qi)
                    kb.append(base_k + ki)
                    first.append(int(ki == 0))
                    last.append(int(ki == nk - 1))
        off += b * s
    return tuple(np.asarray(a, np.int32) for a in (qb, kb, first, last))


def _pair_specs(xa, xb, tm):
    d = xa.shape[1]
    na = xa.shape[0] // tm
    return (pl.BlockSpec((tm, d), lambda i: (jnp.minimum(i, na - 1), 0)),
            pl.BlockSpec((tm, d), lambda i: (jnp.maximum(i - na, 0), 0)), na)


def _pair_block(i, na, xa_ref, xb_ref):
    return jnp.where(i < na, xa_ref[...], xb_ref[...])


def _norm_proj_kernel(xa_ref, xb_ref, g_ref, w_ref, o_ref, *, na):
    x = _pair_block(pl.program_id(0), na, xa_ref, xb_ref)
    h = _rms_rows(x, g_ref[...]).astype(BF)
    o_ref[...] = _dot(h, w_ref[...])


def _norm_proj(xa, xb, gain, w, tm):
    n, d = xa.shape[0] + xb.shape[0], xa.shape[1]
    e = w.shape[1]
    spec_a, spec_b, na = _pair_specs(xa, xb, tm)
    return pl.pallas_call(
        functools.partial(_norm_proj_kernel, na=na),
        out_shape=jax.ShapeDtypeStruct((n, e), F32),
        grid=(n // tm,),
        in_specs=[
            spec_a, spec_b,
            pl.BlockSpec((1, d), lambda i: (0, 0)),
            pl.BlockSpec((d, e), lambda i: (0, 0)),
        ],
        out_specs=pl.BlockSpec((tm, e), lambda i: (i, 0)),
        compiler_params=_cparams(1),
        name="norm_proj",
    )(xa, xb, gain, w)


def _hgrn_direction(q_raw, f_raw, v, lb, state_ref, o_ref, reverse, t):
    n_chunks = t // HGRN_CHUNK
    q = _silu(q_raw)
    f = lb + (1.0 - lb) * _sigmoid(f_raw)
    k = 1.0 - f
    g = jnp.log(f)
    row = lax.broadcasted_iota(jnp.int32, (t, t), 0)
    col = lax.broadcasted_iota(jnp.int32, (t, t), 1)
    same_chunk = (row // HGRN_CHUNK) == (col // HGRN_CHUNK)
    causal = (col >= row) if reverse else (col <= row)
    keep = jnp.logical_and(same_chunk, causal)
    b = _dot01(jnp.where(keep, 1.0, 0.0).astype(BF), g)
    q_dec = (q * jnp.exp(b)).astype(BF)
    k_inv = (k * jnp.exp(-b)).astype(BF)
    v_bf = v.astype(BF)
    for h in range(MIX_W // HEAD_W):
        lanes = slice(h * HEAD_W, (h + 1) * HEAD_W)
        scores = jnp.where(keep, _dot_nt(q_dec[:, lanes], k_inv[:, lanes]), 0.0)
        o_ref[:, lanes] = _dot(scores.astype(BF), v_bf[:, lanes])
    order = range(n_chunks - 1, -1, -1) if reverse else range(n_chunks)
    for c in order:
        rows = slice(c * HGRN_CHUNK, (c + 1) * HGRN_CHUNK)
        edge = c * HGRN_CHUNK if reverse else (c + 1) * HGRN_CHUNK - 1
        b_edge = b[edge:edge + 1, :]
        k_end = (k[rows, :] * jnp.exp(b_edge - b[rows, :])).astype(BF)
        decay = jnp.exp(b_edge)
        for h in range(MIX_W // HEAD_W):
            lanes = slice(h * HEAD_W, (h + 1) * HEAD_W)
            s_t = state_ref[h]
            o_ref[rows, lanes] += _dot_nt(q_dec[rows, lanes], s_t.astype(BF))
            state_ref[h] = decay[:, lanes] * s_t + _dot_tn(v_bf[rows, lanes], k_end[:, lanes])


def _hgrn_sgu_kernel(starts_ref, ends_ref,
                     qf_ref, ff_ref, vf_ref, qb_ref, fb_ref, vb_ref, u_ref, v_ref,
                     lbp_ref, lng_ref, lnb_ref, ws_ref, bs_ref,
                     of_ref, ob_ref, sgu_ref, sf_ref, sb_ref, *, layer, t):
    i = pl.program_id(0)
    j = pl.num_programs(0) - 1 - i

    @pl.when(starts_ref[i] == 1)
    def _():
        sf_ref[...] = jnp.zeros_like(sf_ref)

    @pl.when(ends_ref[j] == 1)
    def _():
        sb_ref[...] = jnp.zeros_like(sb_ref)

    lbp = lbp_ref[...]
    e = jnp.exp(lbp - jnp.max(lbp, axis=0, keepdims=True))
    sm = e / jnp.sum(e, axis=0, keepdims=True)
    lb = jnp.sum(sm[:layer + 1, :], axis=0, keepdims=True)

    _hgrn_direction(qf_ref[...], ff_ref[...], vf_ref[...], lb, sf_ref, of_ref, False, t)
    _hgrn_direction(qb_ref[...], fb_ref[...], vb_ref[...], lb, sb_ref, ob_ref, True, t)

    u = jax.nn.gelu(u_ref[...])
    v = jax.nn.gelu(v_ref[...])
    for gi in range(MIX_W // HEAD_W):
        lanes = slice(gi * HEAD_W, (gi + 1) * HEAD_W)
        vg = v[:, lanes]
        mu = jnp.mean(vg, axis=-1, keepdims=True)
        var = jnp.mean(jnp.square(vg - mu), axis=-1, keepdims=True)
        vln = ((vg - mu) * lax.rsqrt(var + EPS) * lng_ref[:, lanes] + lnb_ref[:, lanes]).astype(BF)
        for c in range(t // SGU_CHUNK):
            rows = slice(c * SGU_CHUNK, (c + 1) * SGU_CHUNK)
            mixed = _dot(ws_ref[gi], vln[rows, :]) + bs_ref[gi]
            sgu_ref[rows, lanes] = u[rows, lanes] * mixed


def _hgrn_sgu(z, hgrn_lb, ln_g, ln_b, w_s, b_s, groups, layer, t):
    n = z.shape[0]
    nblk = n // t
    starts, ends, _ = _block_tables(groups, t)
    fwd = lambda c: pl.BlockSpec((t, MIX_W), lambda i, s, e, c=c: (i, c))
    bwd = lambda c: pl.BlockSpec((t, MIX_W), lambda i, s, e, c=c: (nblk - 1 - i, c))
    full = lambda shape: pl.BlockSpec(shape, lambda i, s, e: (0,) * len(shape))
    bs_b = jnp.broadcast_to(b_s[:, :, None], b_s.shape + (HEAD_W,)).astype(F32)
    out = jax.ShapeDtypeStruct((n, MIX_W), F32)
    return pl.pallas_call(
        functools.partial(_hgrn_sgu_kernel, layer=layer, t=t),
        out_shape=(out, out, out),
        grid_spec=pltpu.PrefetchScalarGridSpec(
            num_scalar_prefetch=2,
            grid=(nblk,),
            in_specs=[fwd(0), fwd(1), fwd(3), bwd(0), bwd(2), bwd(3), fwd(5), fwd(6),
                      full(hgrn_lb.shape), full((1, MIX_W)), full((1, MIX_W)),
                      full(w_s.shape), full(bs_b.shape)],
            out_specs=[fwd(0), bwd(0), fwd(0)],
            scratch_shapes=[pltpu.VMEM((MIX_W // HEAD_W, HEAD_W, HEAD_W), F32),
                            pltpu.VMEM((MIX_W // HEAD_W, HEAD_W, HEAD_W), F32)],
        ),
        compiler_params=_cparams(1),
        name="hgrn_sgu",
    )(jnp.asarray(starts), jnp.asarray(ends), z, z, z, z, z, z, z, z,
      hgrn_lb, ln_g.reshape(1, -1), ln_b.reshape(1, -1), w_s.astype(BF), bs_b)


def _mix0_ffn_kernel(xa_ref, xb_ref, of_ref, ob_ref, gate_ref, sgu_ref, gn_ref, wo_ref, g2_ref,
                     w1_ref, w3_ref, w2_ref, o_ref, hh_ref, *, f_chunk, na):
    o = of_ref[...] + ob_ref[...]
    gate = gate_ref[...]
    parts = []
    for h in range(MIX_W // HEAD_W):
        lanes = slice(h * HEAD_W, (h + 1) * HEAD_W)
        parts.append(_rms_rows(o[:, lanes], gn_ref[...]) * _silu(gate[:, lanes]))
    parts.append(sgu_ref[...])
    mix_in = jnp.concatenate(parts, axis=-1).astype(BF)
    x1 = _pair_block(pl.program_id(0), na, xa_ref, xb_ref) + _dot(mix_in, wo_ref[...])
    h2 = _rms_rows(x1, g2_ref[...]).astype(BF)
    d_ff = w1_ref.shape[1]
    for c in range(d_ff // f_chunk):
        cols = slice(c * f_chunk, (c + 1) * f_chunk)
        hh_ref[:, cols] = (_silu(_dot(h2, w1_ref[:, cols])) * _dot(h2, w3_ref[:, cols])).astype(BF)
    o_ref[...] = x1 + _dot(hh_ref[...], w2_ref[...])


def _resident(shape):
    return pl.BlockSpec(shape, lambda i: (0,) * len(shape), pipeline_mode=pl.Buffered(1))


def _mix0_ffn(xa, xb, z, o_f, o_b, sgu, gnorm, w_out, g2, w1, w3, w2, tm):
    n, d = xa.shape[0] + xb.shape[0], xa.shape[1]
    d_ff = w1.shape[1]
    tok = lambda w, c=0: pl.BlockSpec((tm, w), lambda i, c=c: (i, c))
    spec_a, spec_b, na = _pair_specs(xa, xb, tm)
    return pl.pallas_call(
        functools.partial(_mix0_ffn_kernel, f_chunk=d_ff // 2, na=na),
        out_shape=jax.ShapeDtypeStruct((n, d), F32),
        grid=(n // tm,),
        in_specs=[spec_a, spec_b, tok(MIX_W), tok(MIX_W), tok(MIX_W, 4), tok(MIX_W),
                  _resident((1, HEAD_W)), _resident(w_out.shape), _resident((1, d)),
                  _resident(w1.shape), _resident(w3.shape), _resident(w2.shape)],
        out_specs=tok(d),
        scratch_shapes=[pltpu.VMEM((tm, d_ff), BF)],
        compiler_params=_cparams(1),
        name="mix0_ffn",
    )(xa, xb, o_f, o_b, z, sgu, gnorm.reshape(1, -1), w_out, g2.reshape(1, -1), w1, w3, w2)


def _rope_tables(s_max):
    half = ROPE_DIMS // 2
    dim = np.arange(LANES) % D_QK
    lo = jnp.asarray(dim < half)[None, :]
    hi = jnp.asarray((dim >= half) & (dim < ROPE_DIMS))[None, :]
    inv_freq = ROPE_THETA ** (-jnp.asarray(2 * (dim % half), F32) / ROPE_DIMS)
    ang = jnp.arange(s_max, dtype=F32)[:, None] * inv_freq[None, :]
    cos, sin = jnp.cos(ang), jnp.sin(ang)
    return (jnp.where(lo | hi, cos, 1.0), jnp.where(lo, -sin, 0.0), jnp.where(hi, sin, 0.0))


def _qk_norm_rope(x, gain, seg, cos, s_lo, s_hi, scale):
    half = ROPE_DIMS // 2
    sq_hi, sq_lo = _split2(x * x)
    ms = _dot(sq_hi, seg) + _dot(sq_lo, seg)
    xn = x * lax.rsqrt(ms + EPS) * gain
    up = pltpu.roll(xn, LANES - half, axis=1)
    down = pltpu.roll(xn, half, axis=1)
    return ((xn * cos + up * s_lo + down * s_hi) * scale).astype(BF)


def _proj1_kernel(pos_ref, x_ref, g_ref, w_ref, qg_ref, kg_ref, cos_ref, slo_ref, shi_ref,
                  p_ref, q_ref, k_ref, vt_ref):
    h = _rms_rows(x_ref[...], g_ref[...]).astype(BF)
    z = _dot(h, w_ref[...])
    p_ref[...] = z[:, :MIX_W]
    vt_ref[...] = z[:, 3 * MIX_W:].T.astype(BF)
    r = lax.broadcasted_iota(jnp.int32, (LANES, LANES), 0)
    c = lax.broadcasted_iota(jnp.int32, (LANES, LANES), 1)
    seg = jnp.where((r // D_QK) == (c // D_QK), 1.0 / D_QK, 0.0).astype(BF)
    cos, s_lo, s_hi = cos_ref[...], slo_ref[...], shi_ref[...]
    for hd in range(MIX_W // HEAD_W):
        lanes = slice(hd * HEAD_W, (hd + 1) * HEAD_W)
        q_ref[:, lanes] = _qk_norm_rope(z[:, MIX_W + hd * HEAD_W:MIX_W + (hd + 1) * HEAD_W],
                                        qg_ref[...], seg, cos, s_lo, s_hi, Q_SCALE)
        k_ref[:, lanes] = _qk_norm_rope(z[:, 2 * MIX_W + hd * HEAD_W:2 * MIX_W + (hd + 1) * HEAD_W],
                                        kg_ref[...], seg, cos, s_lo, s_hi, 1.0)


def _proj1(x, gain, w, q_gain, k_gain, groups, tm):
    n, d = x.shape
    e = w.shape[1]
    _, _, pos = _block_tables(groups, tm)
    s_max = max(s for _, s in groups)
    cos, s_lo, s_hi = _rope_tables(s_max)
    tok = lambda width: pl.BlockSpec((tm, width), lambda i, p: (i, 0))
    full = lambda shape: pl.BlockSpec(shape, lambda i, p: (0,) * len(shape))
    rope = pl.BlockSpec((tm, LANES), lambda i, p: (p[i], 0))
    tile2 = lambda g: jnp.concatenate([g, g]).reshape(1, LANES)
    return pl.pallas_call(
        _proj1_kernel,
        out_shape=(jax.ShapeDtypeStruct((n, MIX_W), F32), jax.ShapeDtypeStruct((n, MIX_W), BF),
                   jax.ShapeDtypeStruct((n, MIX_W), BF), jax.ShapeDtypeStruct((MIX_W, n), BF)),
        grid_spec=pltpu.PrefetchScalarGridSpec(
            num_scalar_prefetch=1,
            grid=(n // tm,),
            in_specs=[tok(d), full((1, d)), full((d, e)), full((1, LANES)), full((1, LANES)),
                      rope, rope, rope],
            out_specs=[tok(MIX_W)] * 3 + [pl.BlockSpec((MIX_W, tm), lambda i, p: (0, i))],
        ),
        compiler_params=_cparams(1),
        name="proj1_qk_rope",
    )(jnp.asarray(pos), x, gain.reshape(1, -1), w, tile2(q_gain), tile2(k_gain), cos, s_lo, s_hi)


def _pool_kernel(starts_ref, ends_ref, prev_ref, cur_ref, next_ref, wc_ref, scale_ref, o_ref, *, t):
    i = pl.program_id(0)
    is_start = starts_ref[i] == 1
    is_end = ends_ref[i] == 1
    cur = cur_ref[...]
    prev = jnp.where(is_start, 0.0, prev_ref[...])
    nxt = jnp.where(is_end, 0.0, next_ref[...])
    ext = jnp.concatenate([prev, cur, nxt], axis=0)
    ext_hi, ext_lo = _split2(ext)
    r = lax.broadcasted_iota(jnp.int32, (t, t + 2 * POOL_HALO), 0) + POOL_HALO
    c = lax.broadcasted_iota(jnp.int32, (t, t + 2 * POOL_HALO), 1)
    row = lax.broadcasted_iota(jnp.int32, (t, 1), 0)
    for gi, w in enumerate(POOL_WINDOWS):
        lanes = slice(gi * HEAD_W, (gi + 1) * HEAD_W)
        hw = w // 2
        band = jnp.where(jnp.logical_and(c >= r - hw, c < r + hw), 1.0, 0.0).astype(BF)
        win = _dot(band, ext_hi[:, lanes]) + _dot(band, ext_lo[:, lanes])
        lo_cut = jnp.where(is_start, jnp.maximum(hw - row, 0), 0)
        hi_cut = jnp.where(is_end, jnp.maximum(row + hw - t, 0), 0)
        cnt = (w - lo_cut - hi_cut).astype(F32)
        diff = win / cnt - cur[:, lanes]
        o_ref[:, lanes] = _dot(diff.astype(BF), wc_ref[gi]) * scale_ref[:, lanes]


def _pool(p, w_c, scale, groups, t):
    n = p.shape[0]
    nblk = n // t
    starts, ends, _ = _block_tables(groups, t)
    per = t // POOL_HALO
    n_halo = n // POOL_HALO
    return pl.pallas_call(
        functools.partial(_pool_kernel, t=t),
        out_shape=jax.ShapeDtypeStruct((n, MIX_W), F32),
        grid_spec=pltpu.PrefetchScalarGridSpec(
            num_scalar_prefetch=2,
            grid=(nblk,),
            in_specs=[
                pl.BlockSpec((POOL_HALO, MIX_W), lambda i, s, e: (jnp.maximum(i * per - 1, 0), 0)),
                pl.BlockSpec((t, MIX_W), lambda i, s, e: (i, 0)),
                pl.BlockSpec((POOL_HALO, MIX_W), lambda i, s, e: (jnp.minimum((i + 1) * per, n_halo - 1), 0)),
                pl.BlockSpec(w_c.shape, lambda i, s, e: (0, 0, 0)),
                pl.BlockSpec((1, MIX_W), lambda i, s, e: (0, 0)),
            ],
            out_specs=pl.BlockSpec((t, MIX_W), lambda i, s, e: (i, 0)),
        ),
        compiler_params=_cparams(1),
        name="pool",
    )(jnp.asarray(starts), jnp.asarray(ends), p, p, p, w_c.astype(BF), scale.reshape(1, -1))


def _diff_attn_kernel(qb_ref, kb_ref, first_ref, last_ref,
                      q_ref, k_ref, vt_ref, lq1_ref, lk1_ref, lq2_ref, lk2_ref, sub_ref, bound_ref,
                      o_ref, qzt_ref, v1_ref, m_ref, acc_ref, *, tq, tkc, n_chunks, unroll, lam_init):
    p = pl.program_id(1)

    @pl.when(first_ref[p] == 1)
    def _():
        qt = q_ref[...].astype(F32).T.astype(BF)
        row = lax.broadcasted_iota(jnp.int32, qt.shape, 0)
        qzt_ref[:, :tq] = jnp.where(row < D_QK, qt, jnp.zeros_like(qt))
        qzt_ref[:, tq:] = jnp.where(row >= D_QK, qt, jnp.zeros_like(qt))
        for slot in range(unroll):
            v1_ref[slot, HEAD_W:, :] = jnp.ones((ONES_ROWS, tkc), BF)
        m_ref[...] = jnp.full_like(m_ref, -jnp.inf)
        acc_ref[...] = jnp.zeros_like(acc_ref)

    def chunk(j, slot):
        off = pl.multiple_of(j * tkc, tkc)
        v1_ref[slot, :HEAD_W, :] = vt_ref[:, pl.ds(off, tkc)]
        st = _dot(k_ref[pl.ds(off, tkc), :], qzt_ref[...])
        m_old = m_ref[...]
        m_new = jnp.maximum(m_old, jnp.max(st, axis=0, keepdims=True))
        alpha = jnp.exp2(m_old - m_new)
        pt = jnp.exp2(st - m_new).astype(BF)
        acc_ref[...] = alpha * acc_ref[...] + _dot(v1_ref[slot], pt)
        m_ref[...] = m_new

    def chunks(i, carry):
        for slot in range(unroll):
            chunk(i * unroll + slot, slot)
        return carry

    bound = bound_ref[0, 0]
    fixed_shift_ok = bound <= MAX_FIXED_SHIFT

    def fixed_chunks(i, carry):
        total = None
        for slot in range(unroll):
            off = pl.multiple_of((i * unroll + slot) * tkc, tkc)
            v1_ref[slot, :HEAD_W, :] = vt_ref[:, pl.ds(off, tkc)]
            st = _dot(k_ref[pl.ds(off, tkc), :], qzt_ref[...])
            part = _dot(v1_ref[slot], jnp.exp2(st - bound).astype(BF))
            total = part if total is None else total + part
        acc_ref[...] += total
        return carry

    @pl.when(fixed_shift_ok)
    def _():
        lax.fori_loop(0, n_chunks // unroll, fixed_chunks, 0)

    @pl.when(jnp.logical_not(fixed_shift_ok))
    def _():
        lax.fori_loop(0, n_chunks // unroll, chunks, 0)

    @pl.when(last_ref[p] == 1)
    def _():
        lam = (jnp.exp(jnp.sum(lq1_ref[...] * lk1_ref[...], keepdims=True))
               - jnp.exp(jnp.sum(lq2_ref[...] * lk2_ref[...], keepdims=True)) + lam_init)
        acc = acc_ref[...]
        o = acc[:HEAD_W, :] / acc[HEAD_W:HEAD_W + 1, :]
        d = o[:, :tq] - lam * o[:, tq:]
        ms = jnp.mean(d * d, axis=0, keepdims=True)
        y = d * lax.rsqrt(ms + EPS) * sub_ref[...] * (1.0 - lam_init)
        o_ref[...] = y.T


def _diff_attn(q, k, vt, q_gain, k_gain, lq1, lk1, lq2, lk2, subln, groups, lam_init, tq, tkb, tkc):
    n = q.shape[0]
    bound = (1.01 * D_QK * Q_SCALE * jnp.max(jnp.abs(q_gain)) * jnp.max(jnp.abs(k_gain))).reshape(1, 1)
    qb, kb, first, last = _attn_tables(groups, tq, tkb)
    n_heads = MIX_W // HEAD_W
    vec = lambda a: a.reshape(1, -1)
    full = lambda shape: pl.BlockSpec(shape, lambda h, p, *_: (0, 0))
    sub_col = jnp.broadcast_to(subln[:, None], (HEAD_W, tq))
    n_chunks = tkb // tkc
    unroll = next(u for u in (4, 2, 1) if n_chunks % u == 0)
    return pl.pallas_call(
        functools.partial(_diff_attn_kernel, tq=tq, tkc=tkc, n_chunks=n_chunks, unroll=unroll,
                          lam_init=lam_init),
        out_shape=jax.ShapeDtypeStruct((n, MIX_W), F32),
        grid_spec=pltpu.PrefetchScalarGridSpec(
            num_scalar_prefetch=4,
            grid=(n_heads, len(qb)),
            in_specs=[
                pl.BlockSpec((tq, HEAD_W), lambda h, p, qb, kb, f, l: (qb[p], h)),
                pl.BlockSpec((tkb, HEAD_W), lambda h, p, qb, kb, f, l: (kb[p], h)),
                pl.BlockSpec((HEAD_W, tkb), lambda h, p, qb, kb, f, l: (h, kb[p])),
                full((1, D_QK)), full((1, D_QK)), full((1, D_QK)), full((1, D_QK)), full((HEAD_W, tq)),
                pl.BlockSpec(memory_space=pltpu.SMEM),
            ],
            out_specs=pl.BlockSpec((tq, HEAD_W), lambda h, p, qb, kb, f, l: (qb[p], h)),
            scratch_shapes=[pltpu.VMEM((HEAD_W, 2 * tq), BF),
                            pltpu.VMEM((unroll, HEAD_W + ONES_ROWS, tkc), BF),
                            pltpu.VMEM((1, 2 * tq), F32), pltpu.VMEM((HEAD_W + ONES_ROWS, 2 * tq), F32)],
        ),
        compiler_params=_cparams(2),
        name="diff_attn",
    )(jnp.asarray(qb), jnp.asarray(kb), jnp.asarray(first), jnp.asarray(last),
      q, k, vt, vec(lq1), vec(lk1), vec(lq2), vec(lk2), sub_col, bound.astype(F32))


def _mix1_route_kernel(x_ref, c_ref, a_ref, wo_ref, g_ref, wr_ref,
                       x3_ref, h_ref, eidx_ref, gate_ref, rank_ref, cnt_ref, base_ref, *, tm):
    i = pl.program_id(0)

    @pl.when(i == 0)
    def _():
        base_ref[...] = jnp.zeros_like(base_ref)

    mix_in = jnp.concatenate([c_ref[...], a_ref[...]], axis=-1).astype(BF)
    x3 = x_ref[...] + _dot(mix_in, wo_ref[...])
    x3_ref[...] = x3
    h = _rms_rows(x3, g_ref[...])
    h_ref[...] = h
    wr = wr_ref[...]
    wr_hi = wr.astype(BF).astype(F32)
    wr_parts = jnp.concatenate([wr_hi, wr - wr_hi], axis=0).astype(BF)
    h_hi, h_lo = _split2(h)
    by_hi = _dot_nt(wr_parts, h_hi)
    logits = by_hi[:N_EXPERTS] + by_hi[N_EXPERTS:] + _dot_nt(wr_hi.astype(BF), h_lo)
    eid = lax.broadcasted_iota(jnp.int32, logits.shape, 0)
    m1 = jnp.max(logits, axis=0, keepdims=True)
    i1 = jnp.min(jnp.where(logits == m1, eid, N_EXPERTS), axis=0, keepdims=True)
    rest = jnp.where(eid == i1, -jnp.inf, logits)
    m2 = jnp.max(rest, axis=0, keepdims=True)
    i2 = jnp.min(jnp.where(rest == m2, eid, N_EXPERTS), axis=0, keepdims=True)
    e2 = jnp.exp(m2 - m1)
    g1 = 1.0 / (1.0 + e2)
    g2 = e2 / (1.0 + e2)
    sel1 = eid == i1
    sel2 = eid == i2
    onehot = jnp.where(jnp.logical_or(sel1, sel2), 1.0, 0.0)
    r = lax.broadcasted_iota(jnp.int32, (tm, tm), 0)
    c = lax.broadcasted_iota(jnp.int32, (tm, tm), 1)
    incl = _dot(onehot.astype(BF), jnp.where(r <= c, 1.0, 0.0).astype(BF))
    before = base_ref[:, :1] + incl - onehot
    eidx_ref[0:1, :] = i1
    eidx_ref[1:2, :] = i2
    gate_ref[0:1, :] = g1
    gate_ref[1:2, :] = g2
    rank_ref[0:1, :] = jnp.sum(jnp.where(sel1, before, 0.0), axis=0, keepdims=True).astype(jnp.int32)
    rank_ref[1:2, :] = jnp.sum(jnp.where(sel2, before, 0.0), axis=0, keepdims=True).astype(jnp.int32)
    base_ref[...] = base_ref[...] + jnp.sum(onehot, axis=1, keepdims=True)
    cnt_ref[...] = base_ref[...].astype(jnp.int32)


def _mix1_route(x, c_out, att, w_out, g2, w_router, tm):
    n, d = x.shape
    tok = lambda width: pl.BlockSpec((tm, width), lambda i: (i, 0))
    lane_tok = pl.BlockSpec((2, tm), lambda i: (0, i))
    return pl.pallas_call(
        functools.partial(_mix1_route_kernel, tm=tm),
        out_shape=(jax.ShapeDtypeStruct((n, d), F32), jax.ShapeDtypeStruct((n, d), F32),
                   jax.ShapeDtypeStruct((2, n), jnp.int32), jax.ShapeDtypeStruct((2, n), F32),
                   jax.ShapeDtypeStruct((2, n), jnp.int32),
                   jax.ShapeDtypeStruct((N_EXPERTS, LANES), jnp.int32)),
        grid=(n // tm,),
        in_specs=[tok(d), tok(MIX_W), tok(MIX_W), _resident(w_out.shape), _resident((1, d)),
                  _resident((N_EXPERTS, d))],
        out_specs=[tok(d), tok(d), lane_tok, lane_tok, lane_tok,
                   pl.BlockSpec((N_EXPERTS, LANES), lambda i: (0, 0))],
        scratch_shapes=[pltpu.VMEM((N_EXPERTS, LANES), F32)],
        compiler_params=_cparams(1),
        name="mix1_route",
    )(x, c_out, att, w_out, g2.reshape(1, -1), w_router.T)


def _experts_kernel(be_ref, nused_ref, row0_ref, rown_ref, rowp_ref, h_hbm, w1_hbm, w3_hbm, w2_hbm, y_hbm,
                    xbuf, ybuf, w1_v, w3_v, w2_v, gsem, ssem, wsem, *, mb, f_chunk, n_rows):
    b = pl.program_id(0)
    last = pl.num_programs(0) - 1
    cur = b % 2
    oth = 1 - cur
    e = be_ref[b]
    prev_e = be_ref[jnp.maximum(b - 1, 0)]
    d_e = w1_v.shape[1]
    n_chunks = d_e // f_chunk

    n_tok = n_rows // 2

    def gather_row(row_ref, r, slot):
        v = row_ref[0, 0, r]
        tok = v - jnp.where(v >= n_rows, n_rows, jnp.where(v >= n_tok, n_tok, 0))
        pltpu.make_async_copy(h_hbm.at[pl.ds(tok, 1), :],
                              xbuf.at[slot, pl.ds(r, 1), :], gsem.at[slot]).start()

    def scatter_row(r, slot):
        pltpu.make_async_copy(ybuf.at[slot, pl.ds(r, 1), :],
                              y_hbm.at[pl.ds(rowp_ref[0, 0, r], 1), :], ssem.at[slot]).start()

    def weight_copies():
        return (pltpu.make_async_copy(w1_hbm.at[e], w1_v, wsem.at[0]),
                pltpu.make_async_copy(w3_hbm.at[e], w3_v, wsem.at[1]),
                pltpu.make_async_copy(w2_hbm.at[e], w2_v, wsem.at[2]))

    @pl.when(b == 0)
    def _():
        ybuf[...] = jnp.zeros_like(ybuf)
        zero_tail = pltpu.make_async_copy(ybuf.at[0], y_hbm.at[pl.ds(n_rows, mb), :], wsem.at[0])
        zero_tail.start()
        zero_tail.wait()
        lax.fori_loop(0, mb, lambda r, c: (gather_row(row0_ref, r, 0), c)[1], 0)

    pltpu.make_async_copy(h_hbm.at[pl.ds(0, mb), :], xbuf.at[cur], gsem.at[cur]).wait()

    @pl.when(b < nused_ref[0])
    def _():
        new_expert = jnp.logical_or(b == 0, e != prev_e)

        @pl.when(new_expert)
        def _():
            for cp in weight_copies():
                cp.start()
            for cp in weight_copies():
                cp.wait()

        x = xbuf[cur].astype(BF)
        acc = jnp.zeros((mb, D_MODEL), F32)
        for c in range(n_chunks):
            cols = slice(c * f_chunk, (c + 1) * f_chunk)
            hh = _silu(_dot(x, w1_v[:, cols])) * _dot(x, w3_v[:, cols])
            acc = acc + _dot(hh.astype(BF), w2_v[cols, :])
            n_dma = n_chunks - 1
            for r in range(min(c, n_dma) * mb // n_dma, min(c + 1, n_dma) * mb // n_dma):
                gather_row(rown_ref, r, oth)
                scatter_row(r, oth)
        ybuf[cur] = acc

    @pl.when(b >= nused_ref[0])
    def _():
        ybuf[cur] = jnp.zeros((mb, D_MODEL), F32)

        @pl.when(b < last)
        def _():
            lax.fori_loop(0, mb, lambda r, c: (gather_row(rown_ref, r, oth), c)[1], 0)

        lax.fori_loop(0, mb, lambda r, c: (scatter_row(r, oth), c)[1], 0)

    pltpu.make_async_copy(ybuf.at[oth], y_hbm.at[pl.ds(0, mb), :], ssem.at[oth]).wait()


def _experts(h, slot_row, block_e, n_used, w1, w3, w2, mb, n_rows):
    n_steps = block_e.shape[0]
    d = h.shape[1]
    d_e = w1.shape[2]
    any_spec = pl.BlockSpec(memory_space=pl.ANY)
    smem_blk = lambda fn: pl.BlockSpec((1, 1, mb), fn, memory_space=pltpu.SMEM)
    return pl.pallas_call(
        functools.partial(_experts_kernel, mb=mb, f_chunk=512, n_rows=n_rows),
        out_shape=jax.ShapeDtypeStruct((n_rows + mb, d), F32),
        grid_spec=pltpu.PrefetchScalarGridSpec(
            num_scalar_prefetch=2,
            grid=(n_steps,),
            in_specs=[smem_blk(lambda b, be, nu: (0, 0, 0)),
                      smem_blk(lambda b, be, nu: (jnp.minimum(b + 1, n_steps - 1), 0, 0)),
                      smem_blk(lambda b, be, nu: (jnp.maximum(b - 1, 0), 0, 0)),
                      any_spec, any_spec, any_spec, any_spec],
            out_specs=any_spec,
            scratch_shapes=[pltpu.VMEM((2, mb, d), F32), pltpu.VMEM((2, mb, d), F32),
                            pltpu.VMEM((d, d_e), BF), pltpu.VMEM((d, d_e), BF), pltpu.VMEM((d_e, d), BF),
                            pltpu.SemaphoreType.DMA((2,)), pltpu.SemaphoreType.DMA((2,)),
                            pltpu.SemaphoreType.DMA((3,))],
        ),
        compiler_params=_cparams(1),
        name="experts",
    )(block_e, n_used, slot_row, slot_row, slot_row, h, w1, w3, w2)


def _combine_kernel(x_ref, gate_ref, y0_ref, y1_ref, oa_ref, ob_ref, *, tm, na):
    i = pl.program_id(0)
    g = jnp.concatenate([gate_ref[...], jnp.zeros((6, tm), F32)], axis=0).T
    out = x_ref[...] + g[:, 0:1] * y0_ref[...] + g[:, 1:2] * y1_ref[...]

    @pl.when(i < na)
    def _():
        oa_ref[...] = out

    @pl.when(i >= na)
    def _():
        ob_ref[...] = out


def _combine(x, gates, y, n_first, tm):
    n, d = x.shape
    nblk = n // tm
    na = n_first // tm
    return pl.pallas_call(
        functools.partial(_combine_kernel, tm=tm, na=na),
        out_shape=(jax.ShapeDtypeStruct((n_first, d), F32), jax.ShapeDtypeStruct((n - n_first, d), F32)),
        grid=(nblk,),
        in_specs=[pl.BlockSpec((tm, d), lambda i: (i, 0)),
                  pl.BlockSpec((2, tm), lambda i: (0, i)),
                  pl.BlockSpec((tm, d), lambda i: (i, 0)),
                  pl.BlockSpec((tm, d), lambda i: (i + nblk, 0))],
        out_specs=[pl.BlockSpec((tm, d), lambda i: (jnp.minimum(i, na - 1), 0)),
                   pl.BlockSpec((tm, d), lambda i: (jnp.maximum(i - na, 0), 0))],
        compiler_params=_cparams(1),
        name="combine",
    )(x, gates, y, y)


def _moe(x3, h, eidx, gates, rank, counts, w1, w3, w2, n_first, mb, tm):
    n = x3.shape[0]
    n_blocks = -(-2 * n // mb) + N_EXPERTS
    cnt = counts[:, 0]
    padded = ((cnt + mb - 1) // mb) * mb
    pends = jnp.cumsum(padded)
    pstarts = pends - padded
    start_of = sum(jnp.where(eidx == e, pstarts[e], 0) for e in range(N_EXPERTS))
    dest = (start_of + rank).reshape(-1)
    rows = jnp.arange(2 * n, dtype=jnp.int32)
    n_slots = (n_blocks + 1) * mb
    spare = 2 * n + jnp.arange(n_slots, dtype=jnp.int32) % mb
    slot_row = spare.at[dest].set(rows, unique_indices=True)
    blk_start = jnp.arange(n_blocks + 1, dtype=jnp.int32) * mb
    block_e = jnp.minimum(jnp.searchsorted(pends, blk_start, side="right"), N_EXPERTS - 1).astype(jnp.int32)
    n_used = (pends[-1] // mb).astype(jnp.int32).reshape(1)
    y = _experts(h, slot_row.reshape(n_blocks + 1, 1, mb), block_e, n_used, w1, w3, w2, mb, 2 * n)
    return _combine(x3, gates, y, n_first, tm)


def _trunk(xa, xb, groups, p):
    bf = lambda a: a.astype(BF)
    z = _norm_proj(xa, xb, p["e_norm1"][0].reshape(1, -1), bf(p["e_w_in"][0]), tm=512)
    o_f, o_b, sgu = _hgrn_sgu(z, p["hgrn_lb"], p["e_sgu_ln_g"][0], p["e_sgu_ln_b"][0],
                              p["e_sgu_w"][0], p["e_sgu_b"][0], groups, layer=0, t=256)
    x = _mix0_ffn(xa, xb, z, o_f, o_b, sgu, p["e_hgrn_gnorm"][0], bf(p["e_w_out"][0]), p["e_norm2"][0],
                  bf(p["e_ffn_w1"][0]), bf(p["e_ffn_w3"][0]), bf(p["e_ffn_w2"][0]), tm=512)
    layer = 1
    lam_init = 0.8 - 0.6 * math.exp(-0.3 * layer)
    pz, q, k, vt = _proj1(x, p["o_norm1"][0], bf(p["o_w_in"][0]), p["o_q_norm"][0], p["o_k_norm"][0],
                          groups, tm=512)
    c_out = _pool(pz, p["o_pool_w"][0], p["o_pool_scale"][0], groups, t=256)
    tkb = min(4096, math.gcd(*[s for _, s in groups]))
    att = _diff_attn(q, k, vt, p["o_q_norm"][0], p["o_k_norm"][0], p["o_lambda_q1"][0], p["o_lambda_k1"][0], p["o_lambda_q2"][0],
                     p["o_lambda_k2"][0], p["o_subln"][0], groups, lam_init, tq=min(1024, tkb), tkb=tkb,
                     tkc=min(512, tkb))
    x3, h, eidx, gates, rank, counts = _mix1_route(x, c_out, att, bf(p["o_w_out"][0]), p["o_norm2"][0],
                                                   p["o_router"][0], tm=512)
    return _moe(x3, h, eidx, gates, rank, counts, bf(p["o_moe_w1"][0]), bf(p["o_moe_w3"][0]),
                bf(p["o_moe_w2"][0]), n_first=xa.shape[0], mb=512, tm=256)


def kernel(x_prompt, x_sample, hgrn_lb, e_norm1, e_w_in, e_hgrn_gnorm, e_sgu_ln_g, e_sgu_ln_b, e_sgu_w, e_sgu_b, e_w_out, e_norm2, e_ffn_w1, e_ffn_w3, e_ffn_w2, o_norm1, o_w_in, o_pool_w, o_pool_scale, o_q_norm, o_k_norm, o_lambda_q1, o_lambda_k1, o_lambda_q2, o_lambda_k2, o_subln, o_w_out, o_norm2, o_router, o_moe_w1, o_moe_w3, o_moe_w2):
    params = dict(
        hgrn_lb=hgrn_lb, e_norm1=e_norm1, e_w_in=e_w_in, e_hgrn_gnorm=e_hgrn_gnorm,
        e_sgu_ln_g=e_sgu_ln_g, e_sgu_ln_b=e_sgu_ln_b, e_sgu_w=e_sgu_w, e_sgu_b=e_sgu_b,
        e_w_out=e_w_out, e_norm2=e_norm2, e_ffn_w1=e_ffn_w1, e_ffn_w3=e_ffn_w3, e_ffn_w2=e_ffn_w2,
        o_norm1=o_norm1, o_w_in=o_w_in, o_pool_w=o_pool_w, o_pool_scale=o_pool_scale,
        o_q_norm=o_q_norm, o_k_norm=o_k_norm, o_lambda_q1=o_lambda_q1, o_lambda_k1=o_lambda_k1,
        o_lambda_q2=o_lambda_q2, o_lambda_k2=o_lambda_k2, o_subln=o_subln, o_w_out=o_w_out,
        o_norm2=o_norm2, o_router=o_router, o_moe_w1=o_moe_w1, o_moe_w3=o_moe_w3, o_moe_w2=o_moe_w2,
    )
    d = x_prompt.shape[-1]
    groups = (x_prompt.shape[:2], x_sample.shape[:2])
    y_p, y_s = _trunk(x_prompt.reshape(-1, d), x_sample.reshape(-1, d), groups, params)
    return (y_p.reshape(x_prompt.shape), y_s.reshape(x_sample.shape))
```

```python
import functools
import math

import numpy as np
import jax
import jax.numpy as jnp
from jax import lax
from jax.experimental import pallas as pl
from jax.experimental.pallas import tpu as pltpu

F32 = jnp.float32
BF = jnp.bfloat16

D_MODEL = 1024
EPS = 1e-6
LANES = 128
HEAD_W = 128
MIX_W = 512
HGRN_CHUNK = 64
HGRN_SUB = 256
SGU_CHUNK = 128
POOL_WINDOWS = (2, 4, 8, 16)
POOL_HALO = 16
D_QK = 64
Q_SCALE = D_QK ** -0.5 * math.log2(math.e)
ONES_ROWS = 16
MAX_FIXED_SHIFT = 40.0
ROPE_DIMS = 16
ROPE_THETA = 500000.0
N_EXPERTS = 8
VMEM_LIMIT = 56 * 1024 * 1024


def _cparams(n_axes, vmem=VMEM_LIMIT):
    return pltpu.CompilerParams(dimension_semantics=("arbitrary",) * n_axes, vmem_limit_bytes=vmem)


def _dot(a, b):
    return jnp.dot(a, b, preferred_element_type=F32)


def _dot_nt(a, b):
    return lax.dot_general(a, b, (((1,), (1,)), ((), ())), preferred_element_type=F32)


def _dot_tn(a, b):
    return lax.dot_general(a, b, (((0,), (0,)), ((), ())), preferred_element_type=F32)


def _split2(x):
    hi = x.astype(BF)
    lo = (x - hi.astype(F32)).astype(BF)
    return hi, lo


def _dot01(a01, x):
    hi, lo = _split2(x)
    return _dot(a01, hi) + _dot(a01, lo)


def _rms_rows(x, gain):
    ms = jnp.mean(x * x, axis=-1, keepdims=True)
    return x * lax.rsqrt(ms + EPS) * gain


def _sigmoid(x):
    return 0.5 * jnp.tanh(0.5 * x) + 0.5


def _silu(x):
    return x * _sigmoid(x)


def _block_tables(groups, t):
    starts, ends, pos = [], [], []
    for (b, s) in groups:
        n = s // t
        for _ in range(b):
            for i in range(n):
                starts.append(int(i == 0))
                ends.append(int(i == n - 1))
                pos.append(i)
    return (np.asarray(starts, np.int32), np.asarray(ends, np.int32), np.asarray(pos, np.int32))


def _attn_tables(groups, tq, tk):
    qb, kb, first, last = [], [], [], []
    off = 0
    for (b, s) in groups:
        for bi in range(b):
            base_q = (off + bi * s) // tq
            base_k = (off + bi * s) // tk
            for qi in range(s // tq):
                nk = s // tk
                for ki in range(nk):
                    qb.append(base_q + qi)
                    kb.append(base_k + ki)
                    first.append(int(ki == 0))
                    last.append(int(ki == nk - 1))
        off += b * s
    return tuple(np.asarray(a, np.int32) for a in (qb, kb, first, last))


def _pair_specs(xa, xb, tm):
    d = xa.shape[1]
    na = xa.shape[0] // tm
    return (pl.BlockSpec((tm, d), lambda i: (jnp.minimum(i, na - 1), 0)),
            pl.BlockSpec((tm, d), lambda i: (jnp.maximum(i - na, 0), 0)), na)


def _pair_block(i, na, xa_ref, xb_ref):
    return jnp.where(i < na, xa_ref[...], xb_ref[...])


def _norm_proj_kernel(xa_ref, xb_ref, g_ref, w_ref, o_ref, *, na):
    x = _pair_block(pl.program_id(0), na, xa_ref, xb_ref)
    h = _rms_rows(x, g_ref[...]).astype(BF)
    o_ref[...] = _dot(h, w_ref[...])


def _norm_proj(xa, xb, gain, w, tm):
    n, d = xa.shape[0] + xb.shape[0], xa.shape[1]
    e = w.shape[1]
    spec_a, spec_b, na = _pair_specs(xa, xb, tm)
    return pl.pallas_call(
        functools.partial(_norm_proj_kernel, na=na),
        out_shape=jax.ShapeDtypeStruct((n, e), F32),
        grid=(n // tm,),
        in_specs=[
            spec_a, spec_b,
            pl.BlockSpec((1, d), lambda i: (0, 0)),
            pl.BlockSpec((d, e), lambda i: (0, 0)),
        ],
        out_specs=pl.BlockSpec((tm, e), lambda i: (i, 0)),
        compiler_params=_cparams(1),
        name="norm_proj",
    )(xa, xb, gain, w)


def _hgrn_direction(q_raw, f_raw, v, lb, state_ref, o_ref, reverse, t):
    n_chunks = t // HGRN_CHUNK
    q = _silu(q_raw)
    f = lb + (1.0 - lb) * _sigmoid(f_raw)
    k = 1.0 - f
    g = jnp.log(f)
    row = lax.broadcasted_iota(jnp.int32, (t, t), 0)
    col = lax.broadcasted_iota(jnp.int32, (t, t), 1)
    same_chunk = (row // HGRN_CHUNK) == (col // HGRN_CHUNK)
    causal = (col >= row) if reverse else (col <= row)
    keep = jnp.logical_and(same_chunk, causal)
    b = _dot01(jnp.where(keep, 1.0, 0.0).astype(BF), g)
    q_dec = (q * jnp.exp(b)).astype(BF)
    k_inv = (k * jnp.exp(-b)).astype(BF)
    v_bf = v.astype(BF)
    for h in range(MIX_W // HEAD_W):
        lanes = slice(h * HEAD_W, (h + 1) * HEAD_W)
        scores = jnp.where(keep, _dot_nt(q_dec[:, lanes], k_inv[:, lanes]), 0.0)
        o_ref[:, lanes] = _dot(scores.astype(BF), v_bf[:, lanes])
    order = range(n_chunks - 1, -1, -1) if reverse else range(n_chunks)
    for c in order:
        rows = slice(c * HGRN_CHUNK, (c + 1) * HGRN_CHUNK)
        edge = c * HGRN_CHUNK if reverse else (c + 1) * HGRN_CHUNK - 1
        b_edge = b[edge:edge + 1, :]
        k_end = (k[rows, :] * jnp.exp(b_edge - b[rows, :])).astype(BF)
        decay = jnp.exp(b_edge)
        for h in range(MIX_W // HEAD_W):
            lanes = slice(h * HEAD_W, (h + 1) * HEAD_W)
            s_t = state_ref[h]
            o_ref[rows, lanes] += _dot_nt(q_dec[rows, lanes], s_t.astype(BF))
            state_ref[h] = decay[:, lanes] * s_t + _dot_tn(v_bf[rows, lanes], k_end[:, lanes])


def _hgrn_sgu_kernel(starts_ref, ends_ref,
                     qf_ref, ff_ref, vf_ref, qb_ref, fb_ref, vb_ref, u_ref, v_ref,
                     lbp_ref, lng_ref, lnb_ref, ws_ref, bs_ref,
                     of_ref, ob_ref, sgu_ref, sf_ref, sb_ref, *, layer, t):
    i = pl.program_id(0)
    j = pl.num_programs(0) - 1 - i

    @pl.when(starts_ref[i] == 1)
    def _():
        sf_ref[...] = jnp.zeros_like(sf_ref)

    @pl.when(ends_ref[j] == 1)
    def _():
        sb_ref[...] = jnp.zeros_like(sb_ref)

    lbp = lbp_ref[...]
    e = jnp.exp(lbp - jnp.max(lbp, axis=0, keepdims=True))
    sm = e / jnp.sum(e, axis=0, keepdims=True)
    lb = jnp.sum(sm[:layer + 1, :], axis=0, keepdims=True)

    n_sub = t // HGRN_SUB
    for sb in range(n_sub):
        rows = slice(sb * HGRN_SUB, (sb + 1) * HGRN_SUB)
        _hgrn_direction(qf_ref[rows, :], ff_ref[rows, :], vf_ref[rows, :], lb, sf_ref, of_ref.at[rows, :],
                        False, HGRN_SUB)
    for sb in range(n_sub - 1, -1, -1):
        rows = slice(sb * HGRN_SUB, (sb + 1) * HGRN_SUB)
        _hgrn_direction(qb_ref[rows, :], fb_ref[rows, :], vb_ref[rows, :], lb, sb_ref, ob_ref.at[rows, :],
                        True, HGRN_SUB)

    u = jax.nn.gelu(u_ref[...])
    v = jax.nn.gelu(v_ref[...])
    for gi in range(MIX_W // HEAD_W):
        lanes = slice(gi * HEAD_W, (gi + 1) * HEAD_W)
        vg = v[:, lanes]
        mu = jnp.mean(vg, axis=-1, keepdims=True)
        var = jnp.mean(jnp.square(vg - mu), axis=-1, keepdims=True)
        vln = ((vg - mu) * lax.rsqrt(var + EPS) * lng_ref[:, lanes] + lnb_ref[:, lanes]).astype(BF)
        for c in range(t // SGU_CHUNK):
            rows = slice(c * SGU_CHUNK, (c + 1) * SGU_CHUNK)
            mixed = _dot(ws_ref[gi], vln[rows, :]) + bs_ref[gi]
            sgu_ref[rows, lanes] = u[rows, lanes] * mixed


def _hgrn_sgu(z, hgrn_lb, ln_g, ln_b, w_s, b_s, groups, layer, t):
    n = z.shape[0]
    nblk = n // t
    starts, ends, _ = _block_tables(groups, t)
    fwd = lambda c: pl.BlockSpec((t, MIX_W), lambda i, s, e, c=c: (i, c))
    bwd = lambda c: pl.BlockSpec((t, MIX_W), lambda i, s, e, c=c: (nblk - 1 - i, c))
    full = lambda shape: pl.BlockSpec(shape, lambda i, s, e: (0,) * len(shape))
    bs_b = jnp.broadcast_to(b_s[:, :, None], b_s.shape + (HEAD_W,)).astype(F32)
    out = jax.ShapeDtypeStruct((n, MIX_W), F32)
    return pl.pallas_call(
        functools.partial(_hgrn_sgu_kernel, layer=layer, t=t),
        out_shape=(out, out, out),
        grid_spec=pltpu.PrefetchScalarGridSpec(
            num_scalar_prefetch=2,
            grid=(nblk,),
            in_specs=[fwd(0), fwd(1), fwd(3), bwd(0), bwd(2), bwd(3), fwd(5), fwd(6),
                      full(hgrn_lb.shape), full((1, MIX_W)), full((1, MIX_W)),
                      full(w_s.shape), full(bs_b.shape)],
            out_specs=[fwd(0), bwd(0), fwd(0)],
            scratch_shapes=[pltpu.VMEM((MIX_W // HEAD_W, HEAD_W, HEAD_W), F32),
                            pltpu.VMEM((MIX_W // HEAD_W, HEAD_W, HEAD_W), F32)],
        ),
        compiler_params=_cparams(1),
        name="hgrn_sgu",
    )(jnp.asarray(starts), jnp.asarray(ends), z, z, z, z, z, z, z, z,
      hgrn_lb, ln_g.reshape(1, -1), ln_b.reshape(1, -1), w_s.astype(BF), bs_b)


def _mix0_ffn_kernel(xa_ref, xb_ref, of_ref, ob_ref, gate_ref, sgu_ref, gn_ref, wo_ref, g2_ref,
                     w1_ref, w3_ref, w2_ref, o_ref, hh_ref, *, f_chunk, na):
    o = of_ref[...] + ob_ref[...]
    gate = gate_ref[...]
    parts = []
    for h in range(MIX_W // HEAD_W):
        lanes = slice(h * HEAD_W, (h + 1) * HEAD_W)
        parts.append(_rms_rows(o[:, lanes], gn_ref[...]) * _silu(gate[:, lanes]))
    parts.append(sgu_ref[...])
    mix_in = jnp.concatenate(parts, axis=-1).astype(BF)
    x1 = _pair_block(pl.program_id(0), na, xa_ref, xb_ref) + _dot(mix_in, wo_ref[...])
    h2 = _rms_rows(x1, g2_ref[...]).astype(BF)
    d_ff = w1_ref.shape[1]
    for c in range(d_ff // f_chunk):
        cols = slice(c * f_chunk, (c + 1) * f_chunk)
        hh_ref[:, cols] = (_silu(_dot(h2, w1_ref[:, cols])) * _dot(h2, w3_ref[:, cols])).astype(BF)
    o_ref[...] = x1 + _dot(hh_ref[...], w2_ref[...])


def _resident(shape):
    return pl.BlockSpec(shape, lambda i: (0,) * len(shape), pipeline_mode=pl.Buffered(1))


def _mix0_ffn(xa, xb, z, o_f, o_b, sgu, gnorm, w_out, g2, w1, w3, w2, tm):
    n, d = xa.shape[0] + xb.shape[0], xa.shape[1]
    d_ff = w1.shape[1]
    tok = lambda w, c=0: pl.BlockSpec((tm, w), lambda i, c=c: (i, c))
    spec_a, spec_b, na = _pair_specs(xa, xb, tm)
    return pl.pallas_call(
        functools.partial(_mix0_ffn_kernel, f_chunk=d_ff // 2, na=na),
        out_shape=jax.ShapeDtypeStruct((n, d), F32),
        grid=(n // tm,),
        in_specs=[spec_a, spec_b, tok(MIX_W), tok(MIX_W), tok(MIX_W, 4), tok(MIX_W),
                  _resident((1, HEAD_W)), _resident(w_out.shape), _resident((1, d)),
                  _resident(w1.shape), _resident(w3.shape), _resident(w2.shape)],
        out_specs=tok(d),
        scratch_shapes=[pltpu.VMEM((tm, d_ff), BF)],
        compiler_params=_cparams(1),
        name="mix0_ffn",
    )(xa, xb, o_f, o_b, z, sgu, gnorm.reshape(1, -1), w_out, g2.reshape(1, -1), w1, w3, w2)


def _rope_tables(s_max):
    half = ROPE_DIMS // 2
    dim = np.arange(LANES) % D_QK
    lo = jnp.asarray(dim < half)[None, :]
    hi = jnp.asarray((dim >= half) & (dim < ROPE_DIMS))[None, :]
    inv_freq = ROPE_THETA ** (-jnp.asarray(2 * (dim % half), F32) / ROPE_DIMS)
    ang = jnp.arange(s_max, dtype=F32)[:, None] * inv_freq[None, :]
    cos, sin = jnp.cos(ang), jnp.sin(ang)
    return (jnp.where(lo | hi, cos, 1.0), jnp.where(lo, -sin, 0.0), jnp.where(hi, sin, 0.0))


def _qk_norm_rope(x, gain, seg, cos, s_lo, s_hi, scale):
    half = ROPE_DIMS // 2
    sq_hi, sq_lo = _split2(x * x)
    ms = _dot(sq_hi, seg) + _dot(sq_lo, seg)
    xn = x * lax.rsqrt(ms + EPS) * gain
    up = pltpu.roll(xn, LANES - half, axis=1)
    down = pltpu.roll(xn, half, axis=1)
    return ((xn * cos + up * s_lo + down * s_hi) * scale).astype(BF)


def _proj1_kernel(pos_ref, x_ref, g_ref, w_ref, qg_ref, kg_ref, cos_ref, slo_ref, shi_ref,
                  p_ref, q_ref, k_ref, vt_ref):
    h = _rms_rows(x_ref[...], g_ref[...]).astype(BF)
    z = _dot(h, w_ref[...])
    p_ref[...] = z[:, :MIX_W]
    vt_ref[...] = z[:, 3 * MIX_W:].T.astype(BF)
    r = lax.broadcasted_iota(jnp.int32, (LANES, LANES), 0)
    c = lax.broadcasted_iota(jnp.int32, (LANES, LANES), 1)
    seg = jnp.where((r // D_QK) == (c // D_QK), 1.0 / D_QK, 0.0).astype(BF)
    cos, s_lo, s_hi = cos_ref[...], slo_ref[...], shi_ref[...]
    for hd in range(MIX_W // HEAD_W):
        lanes = slice(hd * HEAD_W, (hd + 1) * HEAD_W)
        q_ref[:, lanes] = _qk_norm_rope(z[:, MIX_W + hd * HEAD_W:MIX_W + (hd + 1) * HEAD_W],
                                        qg_ref[...], seg, cos, s_lo, s_hi, Q_SCALE)
        k_ref[:, lanes] = _qk_norm_rope(z[:, 2 * MIX_W + hd * HEAD_W:2 * MIX_W + (hd + 1) * HEAD_W],
                                        kg_ref[...], seg, cos, s_lo, s_hi, 1.0)


def _proj1(x, gain, w, q_gain, k_gain, groups, tm):
    n, d = x.shape
    e = w.shape[1]
    _, _, pos = _block_tables(groups, tm)
    s_max = max(s for _, s in groups)
    cos, s_lo, s_hi = _rope_tables(s_max)
    tok = lambda width: pl.BlockSpec((tm, width), lambda i, p: (i, 0))
    full = lambda shape: pl.BlockSpec(shape, lambda i, p: (0,) * len(shape))
    rope = pl.BlockSpec((tm, LANES), lambda i, p: (p[i], 0))
    tile2 = lambda g: jnp.concatenate([g, g]).reshape(1, LANES)
    return pl.pallas_call(
        _proj1_kernel,
        out_shape=(jax.ShapeDtypeStruct((n, MIX_W), F32), jax.ShapeDtypeStruct((n, MIX_W), BF),
                   jax.ShapeDtypeStruct((n, MIX_W), BF), jax.ShapeDtypeStruct((MIX_W, n), BF)),
        grid_spec=pltpu.PrefetchScalarGridSpec(
            num_scalar_prefetch=1,
            grid=(n // tm,),
            in_specs=[tok(d), full((1, d)), full((d, e)), full((1, LANES)), full((1, LANES)),
                      rope, rope, rope],
            out_specs=[tok(MIX_W)] * 3 + [pl.BlockSpec((MIX_W, tm), lambda i, p: (0, i))],
        ),
        compiler_params=_cparams(1),
        name="proj1_qk_rope",
    )(jnp.asarray(pos), x, gain.reshape(1, -1), w, tile2(q_gain), tile2(k_gain), cos, s_lo, s_hi)


def _pool_block(is_start, is_end, prev_ref, cur_ref, next_ref, wc_ref, scale_ref, t):
    cur = cur_ref[...]
    prev = jnp.where(is_start, 0.0, prev_ref[...])
    nxt = jnp.where(is_end, 0.0, next_ref[...])
    ext = jnp.concatenate([prev, cur, nxt], axis=0)
    ext_hi, ext_lo = _split2(ext)
    r = lax.broadcasted_iota(jnp.int32, (t, t + 2 * POOL_HALO), 0) + POOL_HALO
    c = lax.broadcasted_iota(jnp.int32, (t, t + 2 * POOL_HALO), 1)
    row = lax.broadcasted_iota(jnp.int32, (t, 1), 0)
    parts = []
    for gi, w in enumerate(POOL_WINDOWS):
        lanes = slice(gi * HEAD_W, (gi + 1) * HEAD_W)
        hw = w // 2
        band = jnp.where(jnp.logical_and(c >= r - hw, c < r + hw), 1.0, 0.0).astype(BF)
        win = _dot(band, ext_hi[:, lanes]) + _dot(band, ext_lo[:, lanes])
        lo_cut = jnp.where(is_start, jnp.maximum(hw - row, 0), 0)
        hi_cut = jnp.where(is_end, jnp.maximum(row + hw - t, 0), 0)
        cnt = (w - lo_cut - hi_cut).astype(F32)
        diff = win / cnt - cur[:, lanes]
        parts.append(_dot(diff.astype(BF), wc_ref[gi]) * scale_ref[:, lanes])
    return jnp.concatenate(parts, axis=-1)


def _diff_attn_kernel(qb_ref, kb_ref, first_ref, last_ref,
                      q_ref, k_ref, vt_ref, lq1_ref, lk1_ref, lq2_ref, lk2_ref, sub_ref, bound_ref,
                      o_ref, qzt_ref, v1_ref, m_ref, acc_ref, *, tq, tkc, n_chunks, unroll, lam_init):
    p = pl.program_id(1)

    @pl.when(first_ref[p] == 1)
    def _():
        qt = q_ref[...].astype(F32).T.astype(BF)
        row = lax.broadcasted_iota(jnp.int32, qt.shape, 0)
        qzt_ref[:, :tq] = jnp.where(row < D_QK, qt, jnp.zeros_like(qt))
        qzt_ref[:, tq:] = jnp.where(row >= D_QK, qt, jnp.zeros_like(qt))
        for slot in range(unroll):
            v1_ref[slot, HEAD_W:, :] = jnp.ones((ONES_ROWS, tkc), BF)
        m_ref[...] = jnp.full_like(m_ref, -jnp.inf)
        acc_ref[...] = jnp.zeros_like(acc_ref)

    def chunk(j, slot):
        off = pl.multiple_of(j * tkc, tkc)
        v1_ref[slot, :HEAD_W, :] = vt_ref[:, pl.ds(off, tkc)]
        st = _dot(k_ref[pl.ds(off, tkc), :], qzt_ref[...])
        m_old = m_ref[...]
        m_new = jnp.maximum(m_old, jnp.max(st, axis=0, keepdims=True))
        alpha = jnp.exp2(m_old - m_new)
        pt = jnp.exp2(st - m_new).astype(BF)
        acc_ref[...] = alpha * acc_ref[...] + _dot(v1_ref[slot], pt)
        m_ref[...] = m_new

    def chunks(i, carry):
        for slot in range(unroll):
            chunk(i * unroll + slot, slot)
        return carry

    bound = bound_ref[0, 0]
    fixed_shift_ok = bound <= MAX_FIXED_SHIFT

    def fixed_chunks(i, carry):
        total = None
        for slot in range(unroll):
            off = pl.multiple_of((i * unroll + slot) * tkc, tkc)
            v1_ref[slot, :HEAD_W, :] = vt_ref[:, pl.ds(off, tkc)]
            st = _dot(k_ref[pl.ds(off, tkc), :], qzt_ref[...])
            part = _dot(v1_ref[slot], jnp.exp2(st - bound).astype(BF))
            total = part if total is None else total + part
        acc_ref[...] += total
        return carry

    @pl.when(fixed_shift_ok)
    def _():
        lax.fori_loop(0, n_chunks // unroll, fixed_chunks, 0)

    @pl.when(jnp.logical_not(fixed_shift_ok))
    def _():
        lax.fori_loop(0, n_chunks // unroll, chunks, 0)

    @pl.when(last_ref[p] == 1)
    def _():
        lam = (jnp.exp(jnp.sum(lq1_ref[...] * lk1_ref[...], keepdims=True))
               - jnp.exp(jnp.sum(lq2_ref[...] * lk2_ref[...], keepdims=True)) + lam_init)
        acc = acc_ref[...]
        o = acc[:HEAD_W, :] / acc[HEAD_W:HEAD_W + 1, :]
        d = o[:, :tq] - lam * o[:, tq:]
        ms = jnp.mean(d * d, axis=0, keepdims=True)
        y = d * lax.rsqrt(ms + EPS) * sub_ref[...] * (1.0 - lam_init)
        o_ref[...] = y.T


def _diff_attn(q, k, vt, q_gain, k_gain, lq1, lk1, lq2, lk2, subln, groups, lam_init, tq, tkb, tkc):
    n = q.shape[0]
    bound = (1.01 * D_QK * Q_SCALE * jnp.max(jnp.abs(q_gain)) * jnp.max(jnp.abs(k_gain))).reshape(1, 1)
    qb, kb, first, last = _attn_tables(groups, tq, tkb)
    n_heads = MIX_W // HEAD_W
    vec = lambda a: a.reshape(1, -1)
    full = lambda shape: pl.BlockSpec(shape, lambda h, p, *_: (0, 0))
    sub_col = jnp.broadcast_to(subln[:, None], (HEAD_W, tq))
    n_chunks = tkb // tkc
    unroll = next(u for u in (4, 2, 1) if n_chunks % u == 0)
    return pl.pallas_call(
        functools.partial(_diff_attn_kernel, tq=tq, tkc=tkc, n_chunks=n_chunks, unroll=unroll,
                          lam_init=lam_init),
        out_shape=jax.ShapeDtypeStruct((n, MIX_W), F32),
        grid_spec=pltpu.PrefetchScalarGridSpec(
            num_scalar_prefetch=4,
            grid=(n_heads, len(qb)),
            in_specs=[
                pl.BlockSpec((tq, HEAD_W), lambda h, p, qb, kb, f, l: (qb[p], h)),
                pl.BlockSpec((tkb, HEAD_W), lambda h, p, qb, kb, f, l: (kb[p], h)),
                pl.BlockSpec((HEAD_W, tkb), lambda h, p, qb, kb, f, l: (h, kb[p])),
                full((1, D_QK)), full((1, D_QK)), full((1, D_QK)), full((1, D_QK)), full((HEAD_W, tq)),
                pl.BlockSpec(memory_space=pltpu.SMEM),
            ],
            out_specs=pl.BlockSpec((tq, HEAD_W), lambda h, p, qb, kb, f, l: (qb[p], h)),
            scratch_shapes=[pltpu.VMEM((HEAD_W, 2 * tq), BF),
                            pltpu.VMEM((unroll, HEAD_W + ONES_ROWS, tkc), BF),
                            pltpu.VMEM((1, 2 * tq), F32), pltpu.VMEM((HEAD_W + ONES_ROWS, 2 * tq), F32)],
        ),
        compiler_params=_cparams(2),
        name="diff_attn",
    )(jnp.asarray(qb), jnp.asarray(kb), jnp.asarray(first), jnp.asarray(last),
      q, k, vt, vec(lq1), vec(lk1), vec(lq2), vec(lk2), sub_col, bound.astype(F32))


def _mix1_route_kernel(starts_ref, ends_ref, x_ref, pprev_ref, p_ref, pnext_ref, a_ref, wc_ref, ps_ref,
                       wo_ref, g_ref, wr_ref,
                       x3_ref, h_ref, eidx_ref, gate_ref, rank_ref, cnt_ref, base_ref, *, tm):
    i = pl.program_id(0)

    @pl.when(i == 0)
    def _():
        base_ref[...] = jnp.zeros_like(base_ref)

    c_out = _pool_block(starts_ref[i] == 1, ends_ref[i] == 1, pprev_ref, p_ref, pnext_ref, wc_ref, ps_ref, tm)
    mix_in = jnp.concatenate([c_out, a_ref[...]], axis=-1).astype(BF)
    x3 = x_ref[...] + _dot(mix_in, wo_ref[...])
    x3_ref[...] = x3
    h = _rms_rows(x3, g_ref[...])
    h_ref[...] = h
    wr = wr_ref[...]
    wr_hi = wr.astype(BF).astype(F32)
    wr_parts = jnp.concatenate([wr_hi, wr - wr_hi], axis=0).astype(BF)
    h_hi, h_lo = _split2(h)
    by_hi = _dot_nt(wr_parts, h_hi)
    logits = by_hi[:N_EXPERTS] + by_hi[N_EXPERTS:] + _dot_nt(wr_hi.astype(BF), h_lo)
    eid = lax.broadcasted_iota(jnp.int32, logits.shape, 0)
    m1 = jnp.max(logits, axis=0, keepdims=True)
    i1 = jnp.min(jnp.where(logits == m1, eid, N_EXPERTS), axis=0, keepdims=True)
    rest = jnp.where(eid == i1, -jnp.inf, logits)
    m2 = jnp.max(rest, axis=0, keepdims=True)
    i2 = jnp.min(jnp.where(rest == m2, eid, N_EXPERTS), axis=0, keepdims=True)
    e2 = jnp.exp(m2 - m1)
    g1 = 1.0 / (1.0 + e2)
    g2 = e2 / (1.0 + e2)
    sel1 = eid == i1
    sel2 = eid == i2
    onehot = jnp.where(jnp.logical_or(sel1, sel2), 1.0, 0.0)
    r = lax.broadcasted_iota(jnp.int32, (tm, tm), 0)
    c = lax.broadcasted_iota(jnp.int32, (tm, tm), 1)
    incl = _dot(onehot.astype(BF), jnp.where(r <= c, 1.0, 0.0).astype(BF))
    before = base_ref[:, :1] + incl - onehot
    eidx_ref[0:1, :] = i1
    eidx_ref[1:2, :] = i2
    gate_ref[0:1, :] = g1
    gate_ref[1:2, :] = g2
    rank_ref[0:1, :] = jnp.sum(jnp.where(sel1, before, 0.0), axis=0, keepdims=True).astype(jnp.int32)
    rank_ref[1:2, :] = jnp.sum(jnp.where(sel2, before, 0.0), axis=0, keepdims=True).astype(jnp.int32)
    base_ref[...] = base_ref[...] + jnp.sum(onehot, axis=1, keepdims=True)
    cnt_ref[...] = base_ref[...].astype(jnp.int32)


def _mix1_route(x, pz, att, pool_w, pool_scale, w_out, g2, w_router, groups, tm):
    n, d = x.shape
    starts, ends, _ = _block_tables(groups, tm)
    per = tm // POOL_HALO
    n_halo = n // POOL_HALO
    tok = lambda width: pl.BlockSpec((tm, width), lambda i, s, e: (i, 0))
    lane_tok = pl.BlockSpec((2, tm), lambda i, s, e: (0, i))
    full = lambda shape: pl.BlockSpec(shape, lambda i, s, e: (0,) * len(shape), pipeline_mode=pl.Buffered(1))
    return pl.pallas_call(
        functools.partial(_mix1_route_kernel, tm=tm),
        out_shape=(jax.ShapeDtypeStruct((n, d), F32), jax.ShapeDtypeStruct((n, d), F32),
                   jax.ShapeDtypeStruct((2, n), jnp.int32), jax.ShapeDtypeStruct((2, n), F32),
                   jax.ShapeDtypeStruct((2, n), jnp.int32),
                   jax.ShapeDtypeStruct((N_EXPERTS, LANES), jnp.int32)),
        grid_spec=pltpu.PrefetchScalarGridSpec(
            num_scalar_prefetch=2,
            grid=(n // tm,),
            in_specs=[tok(d),
                      pl.BlockSpec((POOL_HALO, MIX_W), lambda i, s, e: (jnp.maximum(i * per - 1, 0), 0)),
                      tok(MIX_W),
                      pl.BlockSpec((POOL_HALO, MIX_W),
                                   lambda i, s, e: (jnp.minimum((i + 1) * per, n_halo - 1), 0)),
                      tok(MIX_W), full(pool_w.shape), full((1, MIX_W)),
                      full(w_out.shape), full((1, d)), full((N_EXPERTS, d))],
            out_specs=[tok(d), tok(d), lane_tok, lane_tok, lane_tok,
                       pl.BlockSpec((N_EXPERTS, LANES), lambda i, s, e: (0, 0))],
            scratch_shapes=[pltpu.VMEM((N_EXPERTS, LANES), F32)],
        ),
        compiler_params=_cparams(1),
        name="mix1_route",
    )(jnp.asarray(starts), jnp.asarray(ends), x, pz, pz, pz, att, pool_w.astype(BF), pool_scale.reshape(1, -1),
      w_out, g2.reshape(1, -1), w_router.T)


def _experts_kernel(be_ref, nused_ref, row0_ref, rown_ref, rowp_ref, h_hbm, w1_hbm, w3_hbm, w2_hbm, y_hbm,
                    xbuf, ybuf, w1_v, w3_v, w2_v, stage_in, stage_out, gsem, ssem, wsem, *, mb, f_chunk, n_rows):
    b = pl.program_id(0)
    last = pl.num_programs(0) - 1
    cur = b % 2
    oth = 1 - cur
    e = be_ref[b]
    prev_e = be_ref[jnp.maximum(b - 1, 0)]
    d_e = w1_v.shape[1]
    n_chunks = d_e // f_chunk

    n_tok = n_rows // 2

    def gather_row(row_ref, r, slot):
        v = row_ref[0, 0, r]
        tok = v - jnp.where(v >= n_rows, n_rows, jnp.where(v >= n_tok, n_tok, 0))
        pltpu.make_async_copy(h_hbm.at[pl.ds(tok, 1), :],
                              xbuf.at[slot, pl.ds(r, 1), :], gsem.at[slot]).start()

    def scatter_row(r, slot):
        pltpu.make_async_copy(ybuf.at[slot, pl.ds(r, 1), :],
                              y_hbm.at[pl.ds(rowp_ref[0, 0, r], 1), :], ssem.at[slot]).start()

    def load_weights():
        pieces = []
        for c in range(n_chunks):
            cols = pl.ds(c * f_chunk, f_chunk)
            pieces.append((w1_hbm.at[e, :, cols], stage_in, w1_v.at[:, cols]))
            pieces.append((w3_hbm.at[e, :, cols], stage_in, w3_v.at[:, cols]))
            pieces.append((w2_hbm.at[e, cols, :], stage_out, w2_v.at[cols, :]))

        def copy(i):
            src, stage, _ = pieces[i]
            return pltpu.make_async_copy(src, stage.at[i % 2], wsem.at[i % 2])

        copy(0).start()
        for i, (_, stage, dst) in enumerate(pieces):
            if i + 1 < len(pieces):
                copy(i + 1).start()
            copy(i).wait()
            dst[...] = stage[i % 2].astype(BF)

    @pl.when(b == 0)
    def _():
        ybuf[...] = jnp.zeros_like(ybuf)
        zero_tail = pltpu.make_async_copy(ybuf.at[0], y_hbm.at[pl.ds(n_rows, mb), :], wsem.at[0])
        zero_tail.start()
        zero_tail.wait()
        lax.fori_loop(0, mb, lambda r, c: (gather_row(row0_ref, r, 0), c)[1], 0)

    pltpu.make_async_copy(h_hbm.at[pl.ds(0, mb), :], xbuf.at[cur], gsem.at[cur]).wait()

    @pl.when(b < nused_ref[0])
    def _():
        new_expert = jnp.logical_or(b == 0, e != prev_e)

        @pl.when(new_expert)
        def _():
            load_weights()

        x = xbuf[cur].astype(BF)
        acc = jnp.zeros((mb, D_MODEL), F32)
        for c in range(n_chunks):
            cols = slice(c * f_chunk, (c + 1) * f_chunk)
            hh = _silu(_dot(x, w1_v[:, cols])) * _dot(x, w3_v[:, cols])
            acc = acc + _dot(hh.astype(BF), w2_v[cols, :])
            n_dma = n_chunks - 1
            for r in range(min(c, n_dma) * mb // n_dma, min(c + 1, n_dma) * mb // n_dma):
                gather_row(rown_ref, r, oth)
                scatter_row(r, oth)
        ybuf[cur] = acc

    @pl.when(b >= nused_ref[0])
    def _():
        ybuf[cur] = jnp.zeros((mb, D_MODEL), F32)

        @pl.when(b < last)
        def _():
            lax.fori_loop(0, mb, lambda r, c: (gather_row(rown_ref, r, oth), c)[1], 0)

        lax.fori_loop(0, mb, lambda r, c: (scatter_row(r, oth), c)[1], 0)

    pltpu.make_async_copy(ybuf.at[oth], y_hbm.at[pl.ds(0, mb), :], ssem.at[oth]).wait()


def _experts(h, slot_row, block_e, n_used, w1, w3, w2, mb, n_rows):
    n_steps = block_e.shape[0]
    d = h.shape[1]
    d_e = w1.shape[2]
    f_chunk = 512
    any_spec = pl.BlockSpec(memory_space=pl.ANY)
    smem_blk = lambda fn: pl.BlockSpec((1, 1, mb), fn, memory_space=pltpu.SMEM)
    return pl.pallas_call(
        functools.partial(_experts_kernel, mb=mb, f_chunk=f_chunk, n_rows=n_rows),
        out_shape=jax.ShapeDtypeStruct((n_rows + mb, d), F32),
        grid_spec=pltpu.PrefetchScalarGridSpec(
            num_scalar_prefetch=2,
            grid=(n_steps,),
            in_specs=[smem_blk(lambda b, be, nu: (0, 0, 0)),
                      smem_blk(lambda b, be, nu: (jnp.minimum(b + 1, n_steps - 1), 0, 0)),
                      smem_blk(lambda b, be, nu: (jnp.maximum(b - 1, 0), 0, 0)),
                      any_spec, any_spec, any_spec, any_spec],
            out_specs=any_spec,
            scratch_shapes=[pltpu.VMEM((2, mb, d), F32), pltpu.VMEM((2, mb, d), F32),
                            pltpu.VMEM((d, d_e), BF), pltpu.VMEM((d, d_e), BF), pltpu.VMEM((d_e, d), BF),
                            pltpu.VMEM((2, d, f_chunk), F32), pltpu.VMEM((2, f_chunk, d), F32),
                            pltpu.SemaphoreType.DMA((2,)), pltpu.SemaphoreType.DMA((2,)),
                            pltpu.SemaphoreType.DMA((2,))],
        ),
        compiler_params=_cparams(1),
        name="experts",
    )(block_e, n_used, slot_row, slot_row, slot_row, h, w1, w3, w2)


def _combine_kernel(x_ref, gate_ref, y0_ref, y1_ref, oa_ref, ob_ref, *, tm, na):
    i = pl.program_id(0)
    g = jnp.concatenate([gate_ref[...], jnp.zeros((6, tm), F32)], axis=0).T
    out = x_ref[...] + g[:, 0:1] * y0_ref[...] + g[:, 1:2] * y1_ref[...]

    @pl.when(i < na)
    def _():
        oa_ref[...] = out

    @pl.when(i >= na)
    def _():
        ob_ref[...] = out


def _combine(x, gates, y, n_first, tm):
    n, d = x.shape
    nblk = n // tm
    na = n_first // tm
    return pl.pallas_call(
        functools.partial(_combine_kernel, tm=tm, na=na),
        out_shape=(jax.ShapeDtypeStruct((n_first, d), F32), jax.ShapeDtypeStruct((n - n_first, d), F32)),
        grid=(nblk,),
        in_specs=[pl.BlockSpec((tm, d), lambda i: (i, 0)),
                  pl.BlockSpec((2, tm), lambda i: (0, i)),
                  pl.BlockSpec((tm, d), lambda i: (i, 0)),
                  pl.BlockSpec((tm, d), lambda i: (i + nblk, 0))],
        out_specs=[pl.BlockSpec((tm, d), lambda i: (jnp.minimum(i, na - 1), 0)),
                   pl.BlockSpec((tm, d), lambda i: (jnp.maximum(i - na, 0), 0))],
        compiler_params=_cparams(1),
        name="combine",
    )(x, gates, y, y)


def _moe(x3, h, eidx, gates, rank, counts, w1, w3, w2, n_first, mb, tm):
    n = x3.shape[0]
    n_blocks = -(-2 * n // mb) + N_EXPERTS
    cnt = counts[:, 0]
    padded = ((cnt + mb - 1) // mb) * mb
    pends = jnp.cumsum(padded)
    pstarts = pends - padded
    start_of = sum(jnp.where(eidx == e, pstarts[e], 0) for e in range(N_EXPERTS))
    dest = (start_of + rank).reshape(-1)
    rows = jnp.arange(2 * n, dtype=jnp.int32)
    n_slots = (n_blocks + 1) * mb
    spare = 2 * n + jnp.arange(n_slots, dtype=jnp.int32) % mb
    slot_row = spare.at[dest].set(rows, unique_indices=True)
    blk_start = jnp.arange(n_blocks + 1, dtype=jnp.int32) * mb
    block_e = jnp.minimum(jnp.searchsorted(pends, blk_start, side="right"), N_EXPERTS - 1).astype(jnp.int32)
    n_used = (pends[-1] // mb).astype(jnp.int32).reshape(1)
    y = _experts(h, slot_row.reshape(n_blocks + 1, 1, mb), block_e, n_used, w1, w3, w2, mb, 2 * n)
    return _combine(x3, gates, y, n_first, tm)


def _trunk(xa, xb, groups, p):
    bf = lambda a: a.astype(BF)
    z = _norm_proj(xa, xb, p["e_norm1"][0].reshape(1, -1), bf(p["e_w_in"][0]), tm=512)
    o_f, o_b, sgu = _hgrn_sgu(z, p["hgrn_lb"], p["e_sgu_ln_g"][0], p["e_sgu_ln_b"][0],
                              p["e_sgu_w"][0], p["e_sgu_b"][0], groups, layer=0, t=512)
    x = _mix0_ffn(xa, xb, z, o_f, o_b, sgu, p["e_hgrn_gnorm"][0], bf(p["e_w_out"][0]), p["e_norm2"][0],
                  bf(p["e_ffn_w1"][0]), bf(p["e_ffn_w3"][0]), bf(p["e_ffn_w2"][0]), tm=512)
    layer = 1
    lam_init = 0.8 - 0.6 * math.exp(-0.3 * layer)
    pz, q, k, vt = _proj1(x, p["o_norm1"][0], bf(p["o_w_in"][0]), p["o_q_norm"][0], p["o_k_norm"][0],
                          groups, tm=512)
    tkb = min(4096, math.gcd(*[s for _, s in groups]))
    att = _diff_attn(q, k, vt, p["o_q_norm"][0], p["o_k_norm"][0], p["o_lambda_q1"][0], p["o_lambda_k1"][0], p["o_lambda_q2"][0],
                     p["o_lambda_k2"][0], p["o_subln"][0], groups, lam_init, tq=min(1024, tkb), tkb=tkb,
                     tkc=min(512, tkb))
    x3, h, eidx, gates, rank, counts = _mix1_route(x, pz, att, p["o_pool_w"][0], p["o_pool_scale"][0],
                                                   bf(p["o_w_out"][0]), p["o_norm2"][0], p["o_router"][0],
                                                   groups, tm=512)
    return _moe(x3, h, eidx, gates, rank, counts, p["o_moe_w1"][0], p["o_moe_w3"][0], p["o_moe_w2"][0],
                n_first=xa.shape[0], mb=512, tm=256)


def kernel(x_prompt, x_sample, hgrn_lb, e_norm1, e_w_in, e_hgrn_gnorm, e_sgu_ln_g, e_sgu_ln_b, e_sgu_w, e_sgu_b, e_w_out, e_norm2, e_ffn_w1, e_ffn_w3, e_ffn_w2, o_norm1, o_w_in, o_pool_w, o_pool_scale, o_q_norm, o_k_norm, o_lambda_q1, o_lambda_k1, o_lambda_q2, o_lambda_k2, o_subln, o_w_out, o_norm2, o_router, o_moe_w1, o_moe_w3, o_moe_w2):
    params = dict(
        hgrn_lb=hgrn_lb, e_norm1=e_norm1, e_w_in=e_w_in, e_hgrn_gnorm=e_hgrn_gnorm,
        e_sgu_ln_g=e_sgu_ln_g, e_sgu_ln_b=e_sgu_ln_b, e_sgu_w=e_sgu_w, e_sgu_b=e_sgu_b,
        e_w_out=e_w_out, e_norm2=e_norm2, e_ffn_w1=e_ffn_w1, e_ffn_w3=e_ffn_w3, e_ffn_w2=e_ffn_w2,
        o_norm1=o_norm1, o_w_in=o_w_in, o_pool_w=o_pool_w, o_pool_scale=o_pool_scale,
        o_q_norm=o_q_norm, o_k_norm=o_k_norm, o_lambda_q1=o_lambda_q1, o_lambda_k1=o_lambda_k1,
        o_lambda_q2=o_lambda_q2, o_lambda_k2=o_lambda_k2, o_subln=o_subln, o_w_out=o_w_out,
        o_norm2=o_norm2, o_router=o_router, o_moe_w1=o_moe_w1, o_moe_w3=o_moe_w3, o_moe_w2=o_moe_w2,
    )
    d = x_prompt.shape[-1]
    groups = (x_prompt.shape[:2], x_sample.shape[:2])
    y_p, y_s = _trunk(x_prompt.reshape(-1, d), x_sample.reshape(-1, d), groups, params)
    return (y_p.reshape(x_prompt.shape), y_s.reshape(x_sample.shape))
```

```python
import functools
import math

import numpy as np
import jax
import jax.numpy as jnp
from jax import lax
from jax.experimental import pallas as pl
from jax.experimental.pallas import tpu as pltpu

F32 = jnp.float32
BF = jnp.bfloat16

D_MODEL = 1024
EPS = 1e-6
LANES = 128
HEAD_W = 128
MIX_W = 512
HGRN_CHUNK = 64
HGRN_SUB = 256
SGU_CHUNK = 128
POOL_WINDOWS = (2, 4, 8, 16)
POOL_HALO = 16
POOL_SUB = 128
D_QK = 64
Q_SCALE = D_QK ** -0.5 * math.log2(math.e)
ONES_ROWS = 16
MAX_FIXED_SHIFT = 40.0
ROPE_DIMS = 16
ROPE_THETA = 500000.0
N_EXPERTS = 8
VMEM_LIMIT = 56 * 1024 * 1024


def _cparams(n_axes, vmem=VMEM_LIMIT):
    return pltpu.CompilerParams(dimension_semantics=("arbitrary",) * n_axes, vmem_limit_bytes=vmem)


def _dot(a, b):
    return jnp.dot(a, b, preferred_element_type=F32)


def _dot_nt(a, b):
    return lax.dot_general(a, b, (((1,), (1,)), ((), ())), preferred_element_type=F32)


def _dot_tn(a, b):
    return lax.dot_general(a, b, (((0,), (0,)), ((), ())), preferred_element_type=F32)


def _split2(x):
    hi = x.astype(BF)
    lo = (x - hi.astype(F32)).astype(BF)
    return hi, lo


def _dot01(a01, x):
    hi, lo = _split2(x)
    return _dot(a01, hi) + _dot(a01, lo)


def _rms_rows(x, gain):
    ms = jnp.mean(x * x, axis=-1, keepdims=True)
    return x * lax.rsqrt(ms + EPS) * gain


def _sigmoid(x):
    return 0.5 * jnp.tanh(0.5 * x) + 0.5


def _silu(x):
    return x * _sigmoid(x)


def _block_tables(groups, t):
    starts, ends, pos = [], [], []
    for (b, s) in groups:
        n = s // t
        for _ in range(b):
            for i in range(n):
                starts.append(int(i == 0))
                ends.append(int(i == n - 1))
                pos.append(i)
    return (np.asarray(starts, np.int32), np.asarray(ends, np.int32), np.asarray(pos, np.int32))


def _attn_tables(groups, tq, tk):
    qb, kb, first, last = [], [], [], []
    off = 0
    for (b, s) in groups:
        for bi in range(b):
            base_q = (off + bi * s) // tq
            base_k = (off + bi * s) // tk
            for qi in range(s // tq):
                nk = s // tk
                for ki in range(nk):
                    qb.append(base_q + qi)
                    kb.append(base_k + ki)
                    first.append(int(ki == 0))
                    last.append(int(ki == nk - 1))
        off += b * s
    return tuple(np.asarray(a, np.int32) for a in (qb, kb, first, last))


def _pair_specs(xa, xb, tm):
    d = xa.shape[1]
    na = xa.shape[0] // tm
    return (pl.BlockSpec((tm, d), lambda i: (jnp.minimum(i, na - 1), 0)),
            pl.BlockSpec((tm, d), lambda i: (jnp.maximum(i - na, 0), 0)), na)


def _pair_block(i, na, xa_ref, xb_ref):
    return jnp.where(i < na, xa_ref[...], xb_ref[...])


def _norm_proj_kernel(xa_ref, xb_ref, g_ref, w_ref, o_ref, *, na):
    x = _pair_block(pl.program_id(0), na, xa_ref, xb_ref)
    h = _rms_rows(x, g_ref[...]).astype(BF)
    o_ref[...] = _dot(h, w_ref[...])


def _norm_proj(xa, xb, gain, w, tm):
    n, d = xa.shape[0] + xb.shape[0], xa.shape[1]
    e = w.shape[1]
    spec_a, spec_b, na = _pair_specs(xa, xb, tm)
    return pl.pallas_call(
        functools.partial(_norm_proj_kernel, na=na),
        out_shape=jax.ShapeDtypeStruct((n, e), F32),
        grid=(n // tm,),
        in_specs=[
            spec_a, spec_b,
            pl.BlockSpec((1, d), lambda i: (0, 0)),
            pl.BlockSpec((d, e), lambda i: (0, 0)),
        ],
        out_specs=pl.BlockSpec((tm, e), lambda i: (i, 0)),
        compiler_params=_cparams(1),
        name="norm_proj",
    )(xa, xb, gain, w)


def _hgrn_direction(q_raw, f_raw, v, lb, state_ref, o_ref, reverse, t):
    n_chunks = t // HGRN_CHUNK
    q = _silu(q_raw)
    f = lb + (1.0 - lb) * _sigmoid(f_raw)
    k = 1.0 - f
    g = jnp.log(f)
    row = lax.broadcasted_iota(jnp.int32, (t, t), 0)
    col = lax.broadcasted_iota(jnp.int32, (t, t), 1)
    same_chunk = (row // HGRN_CHUNK) == (col // HGRN_CHUNK)
    causal = (col >= row) if reverse else (col <= row)
    keep = jnp.logical_and(same_chunk, causal)
    b = _dot01(jnp.where(keep, 1.0, 0.0).astype(BF), g)
    q_dec = (q * jnp.exp(b)).astype(BF)
    k_inv = (k * jnp.exp(-b)).astype(BF)
    v_bf = v.astype(BF)
    for h in range(MIX_W // HEAD_W):
        lanes = slice(h * HEAD_W, (h + 1) * HEAD_W)
        scores = jnp.where(keep, _dot_nt(q_dec[:, lanes], k_inv[:, lanes]), 0.0)
        o_ref[:, lanes] = _dot(scores.astype(BF), v_bf[:, lanes])
    order = range(n_chunks - 1, -1, -1) if reverse else range(n_chunks)
    for c in order:
        rows = slice(c * HGRN_CHUNK, (c + 1) * HGRN_CHUNK)
        edge = c * HGRN_CHUNK if reverse else (c + 1) * HGRN_CHUNK - 1
        b_edge = b[edge:edge + 1, :]
        k_end = (k[rows, :] * jnp.exp(b_edge - b[rows, :])).astype(BF)
        decay = jnp.exp(b_edge)
        for h in range(MIX_W // HEAD_W):
            lanes = slice(h * HEAD_W, (h + 1) * HEAD_W)
            s_t = state_ref[h]
            o_ref[rows, lanes] += _dot_nt(q_dec[rows, lanes], s_t.astype(BF))
            state_ref[h] = decay[:, lanes] * s_t + _dot_tn(v_bf[rows, lanes], k_end[:, lanes])


def _hgrn_sgu_kernel(starts_ref, ends_ref,
                     qf_ref, ff_ref, vf_ref, qb_ref, fb_ref, vb_ref, u_ref, v_ref,
                     lbp_ref, lng_ref, lnb_ref, ws_ref, bs_ref,
                     of_ref, ob_ref, sgu_ref, sf_ref, sb_ref, *, layer, t):
    i = pl.program_id(0)
    j = pl.num_programs(0) - 1 - i

    @pl.when(starts_ref[i] == 1)
    def _():
        sf_ref[...] = jnp.zeros_like(sf_ref)

    @pl.when(ends_ref[j] == 1)
    def _():
        sb_ref[...] = jnp.zeros_like(sb_ref)

    lbp = lbp_ref[...]
    e = jnp.exp(lbp - jnp.max(lbp, axis=0, keepdims=True))
    sm = e / jnp.sum(e, axis=0, keepdims=True)
    lb = jnp.sum(sm[:layer + 1, :], axis=0, keepdims=True)

    n_sub = t // HGRN_SUB
    for sb in range(n_sub):
        rows = slice(sb * HGRN_SUB, (sb + 1) * HGRN_SUB)
        _hgrn_direction(qf_ref[rows, :], ff_ref[rows, :], vf_ref[rows, :], lb, sf_ref, of_ref.at[rows, :],
                        False, HGRN_SUB)
    for sb in range(n_sub - 1, -1, -1):
        rows = slice(sb * HGRN_SUB, (sb + 1) * HGRN_SUB)
        _hgrn_direction(qb_ref[rows, :], fb_ref[rows, :], vb_ref[rows, :], lb, sb_ref, ob_ref.at[rows, :],
                        True, HGRN_SUB)

    u = jax.nn.gelu(u_ref[...])
    v = jax.nn.gelu(v_ref[...])
    for gi in range(MIX_W // HEAD_W):
        lanes = slice(gi * HEAD_W, (gi + 1) * HEAD_W)
        vg = v[:, lanes]
        mu = jnp.mean(vg, axis=-1, keepdims=True)
        var = jnp.mean(jnp.square(vg - mu), axis=-1, keepdims=True)
        vln = ((vg - mu) * lax.rsqrt(var + EPS) * lng_ref[:, lanes] + lnb_ref[:, lanes]).astype(BF)
        for c in range(t // SGU_CHUNK):
            rows = slice(c * SGU_CHUNK, (c + 1) * SGU_CHUNK)
            mixed = _dot(ws_ref[gi], vln[rows, :]) + bs_ref[gi]
            sgu_ref[rows, lanes] = u[rows, lanes] * mixed


def _hgrn_sgu(z, hgrn_lb, ln_g, ln_b, w_s, b_s, groups, layer, t):
    n = z.shape[0]
    nblk = n // t
    starts, ends, _ = _block_tables(groups, t)
    fwd = lambda c: pl.BlockSpec((t, MIX_W), lambda i, s, e, c=c: (i, c))
    bwd = lambda c: pl.BlockSpec((t, MIX_W), lambda i, s, e, c=c: (nblk - 1 - i, c))
    full = lambda shape: pl.BlockSpec(shape, lambda i, s, e: (0,) * len(shape))
    bs_b = jnp.broadcast_to(b_s[:, :, None], b_s.shape + (HEAD_W,)).astype(F32)
    out = jax.ShapeDtypeStruct((n, MIX_W), F32)
    return pl.pallas_call(
        functools.partial(_hgrn_sgu_kernel, layer=layer, t=t),
        out_shape=(out, out, out),
        grid_spec=pltpu.PrefetchScalarGridSpec(
            num_scalar_prefetch=2,
            grid=(nblk,),
            in_specs=[fwd(0), fwd(1), fwd(3), bwd(0), bwd(2), bwd(3), fwd(5), fwd(6),
                      full(hgrn_lb.shape), full((1, MIX_W)), full((1, MIX_W)),
                      full(w_s.shape), full(bs_b.shape)],
            out_specs=[fwd(0), bwd(0), fwd(0)],
            scratch_shapes=[pltpu.VMEM((MIX_W // HEAD_W, HEAD_W, HEAD_W), F32),
                            pltpu.VMEM((MIX_W // HEAD_W, HEAD_W, HEAD_W), F32)],
        ),
        compiler_params=_cparams(1),
        name="hgrn_sgu",
    )(jnp.asarray(starts), jnp.asarray(ends), z, z, z, z, z, z, z, z,
      hgrn_lb, ln_g.reshape(1, -1), ln_b.reshape(1, -1), w_s.astype(BF), bs_b)


def _mix0_ffn_kernel(xa_ref, xb_ref, of_ref, ob_ref, gate_ref, sgu_ref, gn_ref, wo_ref, g2_ref,
                     w1_ref, w3_ref, w2_ref, o_ref, hh_ref, *, f_chunk, na):
    o = of_ref[...] + ob_ref[...]
    gate = gate_ref[...]
    parts = []
    for h in range(MIX_W // HEAD_W):
        lanes = slice(h * HEAD_W, (h + 1) * HEAD_W)
        parts.append(_rms_rows(o[:, lanes], gn_ref[...]) * _silu(gate[:, lanes]))
    parts.append(sgu_ref[...])
    mix_in = jnp.concatenate(parts, axis=-1).astype(BF)
    x1 = _pair_block(pl.program_id(0), na, xa_ref, xb_ref) + _dot(mix_in, wo_ref[...])
    h2 = _rms_rows(x1, g2_ref[...]).astype(BF)
    d_ff = w1_ref.shape[1]
    for c in range(d_ff // f_chunk):
        cols = slice(c * f_chunk, (c + 1) * f_chunk)
        hh_ref[:, cols] = (_silu(_dot(h2, w1_ref[:, cols])) * _dot(h2, w3_ref[:, cols])).astype(BF)
    o_ref[...] = x1 + _dot(hh_ref[...], w2_ref[...])


def _resident(shape):
    return pl.BlockSpec(shape, lambda i: (0,) * len(shape), pipeline_mode=pl.Buffered(1))


def _mix0_ffn(xa, xb, z, o_f, o_b, sgu, gnorm, w_out, g2, w1, w3, w2, tm):
    n, d = xa.shape[0] + xb.shape[0], xa.shape[1]
    d_ff = w1.shape[1]
    tok = lambda w, c=0: pl.BlockSpec((tm, w), lambda i, c=c: (i, c))
    spec_a, spec_b, na = _pair_specs(xa, xb, tm)
    return pl.pallas_call(
        functools.partial(_mix0_ffn_kernel, f_chunk=d_ff // 2, na=na),
        out_shape=jax.ShapeDtypeStruct((n, d), F32),
        grid=(n // tm,),
        in_specs=[spec_a, spec_b, tok(MIX_W), tok(MIX_W), tok(MIX_W, 4), tok(MIX_W),
                  _resident((1, HEAD_W)), _resident(w_out.shape), _resident((1, d)),
                  _resident(w1.shape), _resident(w3.shape), _resident(w2.shape)],
        out_specs=tok(d),
        scratch_shapes=[pltpu.VMEM((tm, d_ff), BF)],
        compiler_params=_cparams(1),
        name="mix0_ffn",
    )(xa, xb, o_f, o_b, z, sgu, gnorm.reshape(1, -1), w_out, g2.reshape(1, -1), w1, w3, w2)


def _rope_tables(s_max):
    half = ROPE_DIMS // 2
    dim = np.arange(LANES) % D_QK
    lo = jnp.asarray(dim < half)[None, :]
    hi = jnp.asarray((dim >= half) & (dim < ROPE_DIMS))[None, :]
    inv_freq = ROPE_THETA ** (-jnp.asarray(2 * (dim % half), F32) / ROPE_DIMS)
    ang = jnp.arange(s_max, dtype=F32)[:, None] * inv_freq[None, :]
    cos, sin = jnp.cos(ang), jnp.sin(ang)
    return (jnp.where(lo | hi, cos, 1.0), jnp.where(lo, -sin, 0.0), jnp.where(hi, sin, 0.0))


def _qk_norm_rope(x, gain, seg, cos, s_lo, s_hi, scale):
    half = ROPE_DIMS // 2
    sq_hi, sq_lo = _split2(x * x)
    ms = _dot(jnp.concatenate([sq_hi, sq_lo], axis=1), seg)
    xn = x * lax.rsqrt(ms + EPS) * gain
    up = pltpu.roll(xn, LANES - half, axis=1)
    down = pltpu.roll(xn, half, axis=1)
    return ((xn * cos + up * s_lo + down * s_hi) * scale).astype(BF)


def _proj1_kernel(pos_ref, x_ref, g_ref, w_ref, qg_ref, kg_ref, cos_ref, slo_ref, shi_ref,
                  p_ref, q_ref, k_ref, vt_ref):
    h = _rms_rows(x_ref[...], g_ref[...]).astype(BF)
    vt_ref[...] = _dot(h, w_ref[:, 3 * MIX_W:]).T.astype(BF)
    r = lax.broadcasted_iota(jnp.int32, (2 * LANES, LANES), 0) % LANES
    c = lax.broadcasted_iota(jnp.int32, (2 * LANES, LANES), 1)
    seg = jnp.where((r // D_QK) == (c // D_QK), 1.0 / D_QK, 0.0).astype(BF)
    cos, s_lo, s_hi = cos_ref[...], slo_ref[...], shi_ref[...]
    for out_ref, gain_ref, col0, scale in ((q_ref, qg_ref, MIX_W, Q_SCALE), (k_ref, kg_ref, 2 * MIX_W, 1.0)):
        z = _dot(h, w_ref[:, col0:col0 + MIX_W])
        for hd in range(MIX_W // HEAD_W):
            lanes = slice(hd * HEAD_W, (hd + 1) * HEAD_W)
            out_ref[:, lanes] = _qk_norm_rope(z[:, lanes], gain_ref[...], seg, cos, s_lo, s_hi, scale)
    p_ref[...] = _dot(h, w_ref[:, :MIX_W])


def _proj1(x, gain, w, q_gain, k_gain, groups, tm):
    n, d = x.shape
    e = w.shape[1]
    _, _, pos = _block_tables(groups, tm)
    s_max = max(s for _, s in groups)
    cos, s_lo, s_hi = _rope_tables(s_max)
    tok = lambda width: pl.BlockSpec((tm, width), lambda i, p: (i, 0))
    full = lambda shape: pl.BlockSpec(shape, lambda i, p: (0,) * len(shape))
    rope = pl.BlockSpec((tm, LANES), lambda i, p: (p[i], 0))
    tile2 = lambda g: jnp.concatenate([g, g]).reshape(1, LANES)
    return pl.pallas_call(
        _proj1_kernel,
        out_shape=(jax.ShapeDtypeStruct((n, MIX_W), F32), jax.ShapeDtypeStruct((n, MIX_W), BF),
                   jax.ShapeDtypeStruct((n, MIX_W), BF), jax.ShapeDtypeStruct((MIX_W, n), BF)),
        grid_spec=pltpu.PrefetchScalarGridSpec(
            num_scalar_prefetch=1,
            grid=(n // tm,),
            in_specs=[tok(d), full((1, d)), full((d, e)), full((1, LANES)), full((1, LANES)),
                      rope, rope, rope],
            out_specs=[tok(MIX_W)] * 3 + [pl.BlockSpec((MIX_W, tm), lambda i, p: (0, i))],
        ),
        compiler_params=_cparams(1),
        name="proj1_qk_rope",
    )(jnp.asarray(pos), x, gain.reshape(1, -1), w, tile2(q_gain), tile2(k_gain), cos, s_lo, s_hi)


def _pool_block(is_start, is_end, prev_ref, cur_ref, next_ref, wc_ref, scale_ref, t):
    cur = cur_ref[...]
    prev = jnp.where(is_start, 0.0, prev_ref[...])
    nxt = jnp.where(is_end, 0.0, next_ref[...])
    ext = jnp.concatenate([prev, cur, nxt], axis=0)
    ext_hi, ext_lo = _split2(ext)
    sub = min(t, POOL_SUB)
    r = lax.broadcasted_iota(jnp.int32, (sub, sub + 2 * POOL_HALO), 0) + POOL_HALO
    c = lax.broadcasted_iota(jnp.int32, (sub, sub + 2 * POOL_HALO), 1)
    row = lax.broadcasted_iota(jnp.int32, (t, 1), 0)
    parts = []
    for gi, w in enumerate(POOL_WINDOWS):
        lanes = slice(gi * HEAD_W, (gi + 1) * HEAD_W)
        hw = w // 2
        band = jnp.where(jnp.logical_and(c >= r - hw, c < r + hw), 1.0, 0.0).astype(BF)
        strips = []
        for s0 in range(0, t, sub):
            rows = slice(s0, s0 + sub + 2 * POOL_HALO)
            both = _dot(band, jnp.concatenate([ext_hi[rows, lanes], ext_lo[rows, lanes]], axis=1))
            strips.append(both[:, :HEAD_W] + both[:, HEAD_W:])
        win = jnp.concatenate(strips, axis=0)
        lo_cut = jnp.where(is_start, jnp.maximum(hw - row, 0), 0)
        hi_cut = jnp.where(is_end, jnp.maximum(row + hw - t, 0), 0)
        cnt = (w - lo_cut - hi_cut).astype(F32)
        diff = win / cnt - cur[:, lanes]
        parts.append(_dot(diff.astype(BF), wc_ref[gi]) * scale_ref[:, lanes])
    return jnp.concatenate(parts, axis=-1)


def _diff_attn_kernel(qb_ref, kb_ref, first_ref, last_ref,
                      q_ref, k_ref, vt_ref, lq1_ref, lk1_ref, lq2_ref, lk2_ref, sub_ref, bound_ref,
                      o_ref, qzt_ref, v1_ref, m_ref, acc_ref, *, tq, tkc, n_chunks, unroll, lam_init):
    p = pl.program_id(1)

    @pl.when(first_ref[p] == 1)
    def _():
        qt = q_ref[...].astype(F32).T.astype(BF)
        row = lax.broadcasted_iota(jnp.int32, qt.shape, 0)
        qzt_ref[:, :tq] = jnp.where(row < D_QK, qt, jnp.zeros_like(qt))
        qzt_ref[:, tq:] = jnp.where(row >= D_QK, qt, jnp.zeros_like(qt))
        for slot in range(unroll):
            v1_ref[slot, HEAD_W:, :] = jnp.ones((ONES_ROWS, tkc), BF)
        m_ref[...] = jnp.full_like(m_ref, -jnp.inf)
        acc_ref[...] = jnp.zeros_like(acc_ref)

    def chunk(j, slot):
        off = pl.multiple_of(j * tkc, tkc)
        v1_ref[slot, :HEAD_W, :] = vt_ref[:, pl.ds(off, tkc)]
        st = _dot(k_ref[pl.ds(off, tkc), :], qzt_ref[...])
        m_old = m_ref[...]
        m_new = jnp.maximum(m_old, jnp.max(st, axis=0, keepdims=True))
        alpha = jnp.exp2(m_old - m_new)
        pt = jnp.exp2(st - m_new).astype(BF)
        acc_ref[...] = alpha * acc_ref[...] + _dot(v1_ref[slot], pt)
        m_ref[...] = m_new

    def chunks(i, carry):
        for slot in range(unroll):
            chunk(i * unroll + slot, slot)
        return carry

    bound = bound_ref[0, 0]
    fixed_shift_ok = bound <= MAX_FIXED_SHIFT

    def fixed_chunks(i, carry):
        total = None
        for slot in range(unroll):
            off = pl.multiple_of((i * unroll + slot) * tkc, tkc)
            v1_ref[slot, :HEAD_W, :] = vt_ref[:, pl.ds(off, tkc)]
            st = _dot(k_ref[pl.ds(off, tkc), :], qzt_ref[...])
            part = _dot(v1_ref[slot], jnp.exp2(st - bound).astype(BF))
            total = part if total is None else total + part
        acc_ref[...] += total
        return carry

    @pl.when(fixed_shift_ok)
    def _():
        lax.fori_loop(0, n_chunks // unroll, fixed_chunks, 0)

    @pl.when(jnp.logical_not(fixed_shift_ok))
    def _():
        lax.fori_loop(0, n_chunks // unroll, chunks, 0)

    @pl.when(last_ref[p] == 1)
    def _():
        lam = (jnp.exp(jnp.sum(lq1_ref[...] * lk1_ref[...], keepdims=True))
               - jnp.exp(jnp.sum(lq2_ref[...] * lk2_ref[...], keepdims=True)) + lam_init)
        acc = acc_ref[...]
        o = acc[:HEAD_W, :] / acc[HEAD_W:HEAD_W + 1, :]
        d = o[:, :tq] - lam * o[:, tq:]
        ms = jnp.mean(d * d, axis=0, keepdims=True)
        y = d * lax.rsqrt(ms + EPS) * sub_ref[...] * (1.0 - lam_init)
        o_ref[...] = y.T


def _diff_attn(q, k, vt, q_gain, k_gain, lq1, lk1, lq2, lk2, subln, groups, lam_init, tq, tkb, tkc):
    n = q.shape[0]
    bound = (1.01 * D_QK * Q_SCALE * jnp.max(jnp.abs(q_gain)) * jnp.max(jnp.abs(k_gain))).reshape(1, 1)
    qb, kb, first, last = _attn_tables(groups, tq, tkb)
    n_heads = MIX_W // HEAD_W
    vec = lambda a: a.reshape(1, -1)
    full = lambda shape: pl.BlockSpec(shape, lambda h, p, *_: (0, 0))
    sub_col = jnp.broadcast_to(subln[:, None], (HEAD_W, tq))
    n_chunks = tkb // tkc
    unroll = next(u for u in (4, 2, 1) if n_chunks % u == 0)
    return pl.pallas_call(
        functools.partial(_diff_attn_kernel, tq=tq, tkc=tkc, n_chunks=n_chunks, unroll=unroll,
                          lam_init=lam_init),
        out_shape=jax.ShapeDtypeStruct((n, MIX_W), F32),
        grid_spec=pltpu.PrefetchScalarGridSpec(
            num_scalar_prefetch=4,
            grid=(n_heads, len(qb)),
            in_specs=[
                pl.BlockSpec((tq, HEAD_W), lambda h, p, qb, kb, f, l: (qb[p], h)),
                pl.BlockSpec((tkb, HEAD_W), lambda h, p, qb, kb, f, l: (kb[p], h)),
                pl.BlockSpec((HEAD_W, tkb), lambda h, p, qb, kb, f, l: (h, kb[p])),
                full((1, D_QK)), full((1, D_QK)), full((1, D_QK)), full((1, D_QK)), full((HEAD_W, tq)),
                pl.BlockSpec(memory_space=pltpu.SMEM),
            ],
            out_specs=pl.BlockSpec((tq, HEAD_W), lambda h, p, qb, kb, f, l: (qb[p], h)),
            scratch_shapes=[pltpu.VMEM((HEAD_W, 2 * tq), BF),
                            pltpu.VMEM((unroll, HEAD_W + ONES_ROWS, tkc), BF),
                            pltpu.VMEM((1, 2 * tq), F32), pltpu.VMEM((HEAD_W + ONES_ROWS, 2 * tq), F32)],
        ),
        compiler_params=_cparams(2),
        name="diff_attn",
    )(jnp.asarray(qb), jnp.asarray(kb), jnp.asarray(first), jnp.asarray(last),
      q, k, vt, vec(lq1), vec(lk1), vec(lq2), vec(lk2), sub_col, bound.astype(F32))


def _mix1_route_kernel(starts_ref, ends_ref, x_ref, pprev_ref, p_ref, pnext_ref, a_ref, wc_ref, ps_ref,
                       wo_ref, g_ref, wr_ref,
                       x3_ref, h_ref, eidx_ref, gate_ref, rank_ref, cnt_ref, base_ref, *, tm):
    i = pl.program_id(0)

    @pl.when(i == 0)
    def _():
        base_ref[...] = jnp.zeros_like(base_ref)

    c_out = _pool_block(starts_ref[i] == 1, ends_ref[i] == 1, pprev_ref, p_ref, pnext_ref, wc_ref, ps_ref, tm)
    mix_in = jnp.concatenate([c_out, a_ref[...]], axis=-1).astype(BF)
    x3 = x_ref[...] + _dot(mix_in, wo_ref[...])
    x3_ref[...] = x3
    h = _rms_rows(x3, g_ref[...])
    h_ref[...] = h
    wr = wr_ref[...]
    wr_hi = wr.astype(BF).astype(F32)
    wr_parts = jnp.concatenate([wr_hi, wr - wr_hi], axis=0).astype(BF)
    h_hi, h_lo = _split2(h)
    by_hi = _dot_nt(wr_parts, h_hi)
    logits = by_hi[:N_EXPERTS] + by_hi[N_EXPERTS:] + _dot_nt(wr_hi.astype(BF), h_lo)
    eid = lax.broadcasted_iota(jnp.int32, logits.shape, 0)
    m1 = jnp.max(logits, axis=0, keepdims=True)
    i1 = jnp.min(jnp.where(logits == m1, eid, N_EXPERTS), axis=0, keepdims=True)
    rest = jnp.where(eid == i1, -jnp.inf, logits)
    m2 = jnp.max(rest, axis=0, keepdims=True)
    i2 = jnp.min(jnp.where(rest == m2, eid, N_EXPERTS), axis=0, keepdims=True)
    e2 = jnp.exp(m2 - m1)
    g1 = 1.0 / (1.0 + e2)
    g2 = e2 / (1.0 + e2)
    sel1 = eid == i1
    sel2 = eid == i2
    onehot = jnp.where(jnp.logical_or(sel1, sel2), 1.0, 0.0)
    r = lax.broadcasted_iota(jnp.int32, (tm, tm), 0)
    c = lax.broadcasted_iota(jnp.int32, (tm, tm), 1)
    incl = _dot(onehot.astype(BF), jnp.where(r <= c, 1.0, 0.0).astype(BF))
    before = base_ref[:, :1] + incl - onehot
    eidx_ref[0:1, :] = i1
    eidx_ref[1:2, :] = i2
    gate_ref[0:1, :] = g1
    gate_ref[1:2, :] = g2
    rank_ref[0:1, :] = jnp.sum(jnp.where(sel1, before, 0.0), axis=0, keepdims=True).astype(jnp.int32)
    rank_ref[1:2, :] = jnp.sum(jnp.where(sel2, before, 0.0), axis=0, keepdims=True).astype(jnp.int32)
    base_ref[...] = base_ref[...] + jnp.sum(onehot, axis=1, keepdims=True)
    cnt_ref[...] = base_ref[...].astype(jnp.int32)


def _mix1_route(x, pz, att, pool_w, pool_scale, w_out, g2, w_router, groups, tm):
    n, d = x.shape
    starts, ends, _ = _block_tables(groups, tm)
    per = tm // POOL_HALO
    n_halo = n // POOL_HALO
    tok = lambda width: pl.BlockSpec((tm, width), lambda i, s, e: (i, 0))
    lane_tok = pl.BlockSpec((2, tm), lambda i, s, e: (0, i))
    full = lambda shape: pl.BlockSpec(shape, lambda i, s, e: (0,) * len(shape), pipeline_mode=pl.Buffered(1))
    return pl.pallas_call(
        functools.partial(_mix1_route_kernel, tm=tm),
        out_shape=(jax.ShapeDtypeStruct((n, d), F32), jax.ShapeDtypeStruct((n, d), F32),
                   jax.ShapeDtypeStruct((2, n), jnp.int32), jax.ShapeDtypeStruct((2, n), F32),
                   jax.ShapeDtypeStruct((2, n), jnp.int32),
                   jax.ShapeDtypeStruct((N_EXPERTS, LANES), jnp.int32)),
        grid_spec=pltpu.PrefetchScalarGridSpec(
            num_scalar_prefetch=2,
            grid=(n // tm,),
            in_specs=[tok(d),
                      pl.BlockSpec((POOL_HALO, MIX_W), lambda i, s, e: (jnp.maximum(i * per - 1, 0), 0)),
                      tok(MIX_W),
                      pl.BlockSpec((POOL_HALO, MIX_W),
                                   lambda i, s, e: (jnp.minimum((i + 1) * per, n_halo - 1), 0)),
                      tok(MIX_W), full(pool_w.shape), full((1, MIX_W)),
                      full(w_out.shape), full((1, d)), full((N_EXPERTS, d))],
            out_specs=[tok(d), tok(d), lane_tok, lane_tok, lane_tok,
                       pl.BlockSpec((N_EXPERTS, LANES), lambda i, s, e: (0, 0))],
            scratch_shapes=[pltpu.VMEM((N_EXPERTS, LANES), F32)],
        ),
        compiler_params=_cparams(1),
        name="mix1_route",
    )(jnp.asarray(starts), jnp.asarray(ends), x, pz, pz, pz, att, pool_w.astype(BF), pool_scale.reshape(1, -1),
      w_out, g2.reshape(1, -1), w_router.T)


def _experts_kernel(be_ref, nused_ref, row0_ref, rown_ref, rowp_ref, h_hbm, w1_hbm, w3_hbm, w2_hbm, y_hbm,
                    xbuf, ybuf, w1_v, w3_v, w2_v, stage_in, stage_out, gsem, ssem, wsem, *, mb, f_chunk, n_rows):
    b = pl.program_id(0)
    last = pl.num_programs(0) - 1
    cur = b % 2
    oth = 1 - cur
    e = be_ref[b]
    prev_e = be_ref[jnp.maximum(b - 1, 0)]
    d_e = w1_v.shape[1]
    n_chunks = d_e // f_chunk

    n_tok = n_rows // 2

    def gather_row(row_ref, r, slot):
        v = row_ref[0, 0, r]
        tok = v - jnp.where(v >= n_rows, n_rows, jnp.where(v >= n_tok, n_tok, 0))
        pltpu.make_async_copy(h_hbm.at[pl.ds(tok, 1), :],
                              xbuf.at[slot, pl.ds(r, 1), :], gsem.at[slot]).start()

    def scatter_row(r, slot):
        pltpu.make_async_copy(ybuf.at[slot, pl.ds(r, 1), :],
                              y_hbm.at[pl.ds(rowp_ref[0, 0, r], 1), :], ssem.at[slot]).start()

    def load_weights():
        pieces = []
        for c in range(n_chunks):
            cols = pl.ds(c * f_chunk, f_chunk)
            pieces.append((w1_hbm.at[e, :, cols], stage_in, w1_v.at[:, cols]))
            pieces.append((w3_hbm.at[e, :, cols], stage_in, w3_v.at[:, cols]))
            pieces.append((w2_hbm.at[e, cols, :], stage_out, w2_v.at[cols, :]))

        def copy(i):
            src, stage, _ = pieces[i]
            return pltpu.make_async_copy(src, stage.at[i % 2], wsem.at[i % 2])

        copy(0).start()
        for i, (_, stage, dst) in enumerate(pieces):
            if i + 1 < len(pieces):
                copy(i + 1).start()
            copy(i).wait()
            dst[...] = stage[i % 2].astype(BF)

    @pl.when(b == 0)
    def _():
        ybuf[...] = jnp.zeros_like(ybuf)
        zero_tail = pltpu.make_async_copy(ybuf.at[0], y_hbm.at[pl.ds(n_rows, mb), :], wsem.at[0])
        zero_tail.start()
        zero_tail.wait()
        lax.fori_loop(0, mb, lambda r, c: (gather_row(row0_ref, r, 0), c)[1], 0)

    pltpu.make_async_copy(h_hbm.at[pl.ds(0, mb), :], xbuf.at[cur], gsem.at[cur]).wait()

    @pl.when(b < nused_ref[0])
    def _():
        new_expert = jnp.logical_or(b == 0, e != prev_e)

        @pl.when(new_expert)
        def _():
            load_weights()

        x = xbuf[cur].astype(BF)
        acc = jnp.zeros((mb, D_MODEL), F32)
        for c in range(n_chunks):
            cols = slice(c * f_chunk, (c + 1) * f_chunk)
            hh = _silu(_dot(x, w1_v[:, cols])) * _dot(x, w3_v[:, cols])
            acc = acc + _dot(hh.astype(BF), w2_v[cols, :])
            n_dma = n_chunks - 1
            for r in range(min(c, n_dma) * mb // n_dma, min(c + 1, n_dma) * mb // n_dma):
                gather_row(rown_ref, r, oth)
                scatter_row(r, oth)
        ybuf[cur] = acc

    @pl.when(b >= nused_ref[0])
    def _():
        ybuf[cur] = jnp.zeros((mb, D_MODEL), F32)

        @pl.when(b < last)
        def _():
            lax.fori_loop(0, mb, lambda r, c: (gather_row(rown_ref, r, oth), c)[1], 0)

        lax.fori_loop(0, mb, lambda r, c: (scatter_row(r, oth), c)[1], 0)

    pltpu.make_async_copy(ybuf.at[oth], y_hbm.at[pl.ds(0, mb), :], ssem.at[oth]).wait()


def _experts(h, slot_row, block_e, n_used, w1, w3, w2, mb, n_rows):
    n_steps = block_e.shape[0]
    d = h.shape[1]
    d_e = w1.shape[2]
    f_chunk = 512
    any_spec = pl.BlockSpec(memory_space=pl.ANY)
    smem_blk = lambda fn: pl.BlockSpec((1, 1, mb), fn, memory_space=pltpu.SMEM)
    return pl.pallas_call(
        functools.partial(_experts_kernel, mb=mb, f_chunk=f_chunk, n_rows=n_rows),
        out_shape=jax.ShapeDtypeStruct((n_rows + mb, d), F32),
        grid_spec=pltpu.PrefetchScalarGridSpec(
            num_scalar_prefetch=2,
            grid=(n_steps,),
            in_specs=[smem_blk(lambda b, be, nu: (0, 0, 0)),
                      smem_blk(lambda b, be, nu: (jnp.minimum(b + 1, n_steps - 1), 0, 0)),
                      smem_blk(lambda b, be, nu: (jnp.maximum(b - 1, 0), 0, 0)),
                      any_spec, any_spec, any_spec, any_spec],
            out_specs=any_spec,
            scratch_shapes=[pltpu.VMEM((2, mb, d), F32), pltpu.VMEM((2, mb, d), F32),
                            pltpu.VMEM((d, d_e), BF), pltpu.VMEM((d, d_e), BF), pltpu.VMEM((d_e, d), BF),
                            pltpu.VMEM((2, d, f_chunk), F32), pltpu.VMEM((2, f_chunk, d), F32),
                            pltpu.SemaphoreType.DMA((2,)), pltpu.SemaphoreType.DMA((2,)),
                            pltpu.SemaphoreType.DMA((2,))],
        ),
        compiler_params=_cparams(1),
        name="experts",
    )(block_e, n_used, slot_row, slot_row, slot_row, h, w1, w3, w2)


def _combine_kernel(x_ref, gate_ref, y0_ref, y1_ref, oa_ref, ob_ref, *, tm, na):
    i = pl.program_id(0)
    g = jnp.concatenate([gate_ref[...], jnp.zeros((6, tm), F32)], axis=0).T
    out = x_ref[...] + g[:, 0:1] * y0_ref[...] + g[:, 1:2] * y1_ref[...]

    @pl.when(i < na)
    def _():
        oa_ref[...] = out

    @pl.when(i >= na)
    def _():
        ob_ref[...] = out


def _combine(x, gates, y, n_first, tm):
    n, d = x.shape
    nblk = n // tm
    na = n_first // tm
    return pl.pallas_call(
        functools.partial(_combine_kernel, tm=tm, na=na),
        out_shape=(jax.ShapeDtypeStruct((n_first, d), F32), jax.ShapeDtypeStruct((n - n_first, d), F32)),
        grid=(nblk,),
        in_specs=[pl.BlockSpec((tm, d), lambda i: (i, 0)),
                  pl.BlockSpec((2, tm), lambda i: (0, i)),
                  pl.BlockSpec((tm, d), lambda i: (i, 0)),
                  pl.BlockSpec((tm, d), lambda i: (i + nblk, 0))],
        out_specs=[pl.BlockSpec((tm, d), lambda i: (jnp.minimum(i, na - 1), 0)),
                   pl.BlockSpec((tm, d), lambda i: (jnp.maximum(i - na, 0), 0))],
        compiler_params=_cparams(1),
        name="combine",
    )(x, gates, y, y)


def _moe(x3, h, eidx, gates, rank, counts, w1, w3, w2, n_first, mb, tm):
    n = x3.shape[0]
    n_blocks = -(-2 * n // mb) + N_EXPERTS
    cnt = counts[:, 0]
    padded = ((cnt + mb - 1) // mb) * mb
    pends = jnp.cumsum(padded)
    pstarts = pends - padded
    start_of = sum(jnp.where(eidx == e, pstarts[e], 0) for e in range(N_EXPERTS))
    dest = (start_of + rank).reshape(-1)
    rows = jnp.arange(2 * n, dtype=jnp.int32)
    n_slots = (n_blocks + 1) * mb
    spare = 2 * n + jnp.arange(n_slots, dtype=jnp.int32) % mb
    slot_row = spare.at[dest].set(rows, unique_indices=True)
    blk_start = jnp.arange(n_blocks + 1, dtype=jnp.int32) * mb
    block_e = jnp.minimum(jnp.searchsorted(pends, blk_start, side="right"), N_EXPERTS - 1).astype(jnp.int32)
    n_used = (pends[-1] // mb).astype(jnp.int32).reshape(1)
    y = _experts(h, slot_row.reshape(n_blocks + 1, 1, mb), block_e, n_used, w1, w3, w2, mb, 2 * n)
    return _combine(x3, gates, y, n_first, tm)


def _trunk(xa, xb, groups, p):
    bf = lambda a: a.astype(BF)
    z = _norm_proj(xa, xb, p["e_norm1"][0].reshape(1, -1), bf(p["e_w_in"][0]), tm=512)
    o_f, o_b, sgu = _hgrn_sgu(z, p["hgrn_lb"], p["e_sgu_ln_g"][0], p["e_sgu_ln_b"][0],
                              p["e_sgu_w"][0], p["e_sgu_b"][0], groups, layer=0, t=512)
    x = _mix0_ffn(xa, xb, z, o_f, o_b, sgu, p["e_hgrn_gnorm"][0], bf(p["e_w_out"][0]), p["e_norm2"][0],
                  bf(p["e_ffn_w1"][0]), bf(p["e_ffn_w3"][0]), bf(p["e_ffn_w2"][0]), tm=512)
    layer = 1
    lam_init = 0.8 - 0.6 * math.exp(-0.3 * layer)
    pz, q, k, vt = _proj1(x, p["o_norm1"][0], bf(p["o_w_in"][0]), p["o_q_norm"][0], p["o_k_norm"][0],
                          groups, tm=512)
    tkb = min(4096, math.gcd(*[s for _, s in groups]))
    att = _diff_attn(q, k, vt, p["o_q_norm"][0], p["o_k_norm"][0], p["o_lambda_q1"][0], p["o_lambda_k1"][0], p["o_lambda_q2"][0],
                     p["o_lambda_k2"][0], p["o_subln"][0], groups, lam_init, tq=min(1024, tkb), tkb=tkb,
                     tkc=min(512, tkb))
    x3, h, eidx, gates, rank, counts = _mix1_route(x, pz, att, p["o_pool_w"][0], p["o_pool_scale"][0],
                                                   bf(p["o_w_out"][0]), p["o_norm2"][0], p["o_router"][0],
                                                   groups, tm=512)
    return _moe(x3, h, eidx, gates, rank, counts, p["o_moe_w1"][0], p["o_moe_w3"][0], p["o_moe_w2"][0],
                n_first=xa.shape[0], mb=512, tm=256)


def kernel(x_prompt, x_sample, hgrn_lb, e_norm1, e_w_in, e_hgrn_gnorm, e_sgu_ln_g, e_sgu_ln_b, e_sgu_w, e_sgu_b, e_w_out, e_norm2, e_ffn_w1, e_ffn_w3, e_ffn_w2, o_norm1, o_w_in, o_pool_w, o_pool_scale, o_q_norm, o_k_norm, o_lambda_q1, o_lambda_k1, o_lambda_q2, o_lambda_k2, o_subln, o_w_out, o_norm2, o_router, o_moe_w1, o_moe_w3, o_moe_w2):
    params = dict(
        hgrn_lb=hgrn_lb, e_norm1=e_norm1, e_w_in=e_w_in, e_hgrn_gnorm=e_hgrn_gnorm,
        e_sgu_ln_g=e_sgu_ln_g, e_sgu_ln_b=e_sgu_ln_b, e_sgu_w=e_sgu_w, e_sgu_b=e_sgu_b,
        e_w_out=e_w_out, e_norm2=e_norm2, e_ffn_w1=e_ffn_w1, e_ffn_w3=e_ffn_w3, e_ffn_w2=e_ffn_w2,
        o_norm1=o_norm1, o_w_in=o_w_in, o_pool_w=o_pool_w, o_pool_scale=o_pool_scale,
        o_q_norm=o_q_norm, o_k_norm=o_k_norm, o_lambda_q1=o_lambda_q1, o_lambda_k1=o_lambda_k1,
        o_lambda_q2=o_lambda_q2, o_lambda_k2=o_lambda_k2, o_subln=o_subln, o_w_out=o_w_out,
        o_norm2=o_norm2, o_router=o_router, o_moe_w1=o_moe_w1, o_moe_w3=o_moe_w3, o_moe_w2=o_moe_w2,
    )
    d = x_prompt.shape[-1]
    groups = (x_prompt.shape[:2], x_sample.shape[:2])
    y_p, y_s = _trunk(x_prompt.reshape(-1, d), x_sample.reshape(-1, d), groups, params)
    return (y_p.reshape(x_prompt.shape), y_s.reshape(x_sample.shape))
```

```python
import functools
import math

import numpy as np
import jax
import jax.numpy as jnp
from jax import lax
from jax.experimental import pallas as pl
from jax.experimental.pallas import tpu as pltpu

F32 = jnp.float32
BF = jnp.bfloat16

D_MODEL = 1024
EPS = 1e-6
LANES = 128
HEAD_W = 128
MIX_W = 512
HGRN_CHUNK = 64
HGRN_SUB = 256
SGU_CHUNK = 128
POOL_WINDOWS = (2, 4, 8, 16)
POOL_HALO = 16
POOL_SUB = 128
D_QK = 64
Q_SCALE = D_QK ** -0.5 * math.log2(math.e)
ONES_ROWS = 16
MAX_FIXED_SHIFT = 40.0
ROPE_DIMS = 16
ROPE_THETA = 500000.0
N_EXPERTS = 8
VMEM_LIMIT = 56 * 1024 * 1024


def _cparams(n_axes, vmem=VMEM_LIMIT):
    return pltpu.CompilerParams(dimension_semantics=("arbitrary",) * n_axes, vmem_limit_bytes=vmem)


def _dot(a, b):
    return jnp.dot(a, b, preferred_element_type=F32)


def _dot_nt(a, b):
    return lax.dot_general(a, b, (((1,), (1,)), ((), ())), preferred_element_type=F32)


def _dot_tn(a, b):
    return lax.dot_general(a, b, (((0,), (0,)), ((), ())), preferred_element_type=F32)


def _split2(x):
    hi = x.astype(BF)
    lo = (x - hi.astype(F32)).astype(BF)
    return hi, lo


def _dot01(a01, x):
    hi, lo = _split2(x)
    return _dot(a01, hi) + _dot(a01, lo)


def _rms_rows(x, gain):
    ms = jnp.mean(x * x, axis=-1, keepdims=True)
    return x * lax.rsqrt(ms + EPS) * gain


def _sigmoid(x):
    return 0.5 * jnp.tanh(0.5 * x) + 0.5


def _silu(x):
    return x * _sigmoid(x)


def _block_tables(groups, t):
    starts, ends, pos = [], [], []
    for (b, s) in groups:
        n = s // t
        for _ in range(b):
            for i in range(n):
                starts.append(int(i == 0))
                ends.append(int(i == n - 1))
                pos.append(i)
    return (np.asarray(starts, np.int32), np.asarray(ends, np.int32), np.asarray(pos, np.int32))


def _attn_tables(groups, tq, tk):
    qb, kb, first, last = [], [], [], []
    off = 0
    for (b, s) in groups:
        for bi in range(b):
            base_q = (off + bi * s) // tq
            base_k = (off + bi * s) // tk
            for qi in range(s // tq):
                nk = s // tk
                for ki in range(nk):
                    qb.append(base_q + qi)
                    kb.append(base_k + ki)
                    first.append(int(ki == 0))
                    last.append(int(ki == nk - 1))
        off += b * s
    return tuple(np.asarray(a, np.int32) for a in (qb, kb, first, last))


def _pair_specs(xa, xb, tm):
    d = xa.shape[1]
    na = xa.shape[0] // tm
    return (pl.BlockSpec((tm, d), lambda i: (jnp.minimum(i, na - 1), 0)),
            pl.BlockSpec((tm, d), lambda i: (jnp.maximum(i - na, 0), 0)), na)


def _pair_block(i, na, xa_ref, xb_ref):
    return jnp.where(i < na, xa_ref[...], xb_ref[...])


def _norm_proj_kernel(xa_ref, xb_ref, g_ref, w_ref, o_ref, *, na):
    x = _pair_block(pl.program_id(0), na, xa_ref, xb_ref)
    h = _rms_rows(x, g_ref[...]).astype(BF)
    o_ref[...] = _dot(h, w_ref[...])


def _norm_proj(xa, xb, gain, w, tm):
    n, d = xa.shape[0] + xb.shape[0], xa.shape[1]
    e = w.shape[1]
    spec_a, spec_b, na = _pair_specs(xa, xb, tm)
    return pl.pallas_call(
        functools.partial(_norm_proj_kernel, na=na),
        out_shape=jax.ShapeDtypeStruct((n, e), F32),
        grid=(n // tm,),
        in_specs=[
            spec_a, spec_b,
            pl.BlockSpec((1, d), lambda i: (0, 0)),
            pl.BlockSpec((d, e), lambda i: (0, 0)),
        ],
        out_specs=pl.BlockSpec((tm, e), lambda i: (i, 0)),
        compiler_params=_cparams(1),
        name="norm_proj",
    )(xa, xb, gain, w)


def _hgrn_direction(q_raw, f_raw, v, lb, state_ref, o_ref, reverse, t):
    n_chunks = t // HGRN_CHUNK
    q = _silu(q_raw)
    f = lb + (1.0 - lb) * _sigmoid(f_raw)
    k = 1.0 - f
    g = jnp.log(f)
    row = lax.broadcasted_iota(jnp.int32, (t, t), 0)
    col = lax.broadcasted_iota(jnp.int32, (t, t), 1)
    same_chunk = (row // HGRN_CHUNK) == (col // HGRN_CHUNK)
    causal = (col >= row) if reverse else (col <= row)
    keep = jnp.logical_and(same_chunk, causal)
    b = _dot01(jnp.where(keep, 1.0, 0.0).astype(BF), g)
    q_dec = (q * jnp.exp(b)).astype(BF)
    k_inv = (k * jnp.exp(-b)).astype(BF)
    v_bf = v.astype(BF)
    for h in range(MIX_W // HEAD_W):
        lanes = slice(h * HEAD_W, (h + 1) * HEAD_W)
        scores = jnp.where(keep, _dot_nt(q_dec[:, lanes], k_inv[:, lanes]), 0.0)
        o_ref[:, lanes] = _dot(scores.astype(BF), v_bf[:, lanes])
    order = range(n_chunks - 1, -1, -1) if reverse else range(n_chunks)
    for c in order:
        rows = slice(c * HGRN_CHUNK, (c + 1) * HGRN_CHUNK)
        edge = c * HGRN_CHUNK if reverse else (c + 1) * HGRN_CHUNK - 1
        b_edge = b[edge:edge + 1, :]
        k_end = (k[rows, :] * jnp.exp(b_edge - b[rows, :])).astype(BF)
        decay = jnp.exp(b_edge)
        for h in range(MIX_W // HEAD_W):
            lanes = slice(h * HEAD_W, (h + 1) * HEAD_W)
            s_t = state_ref[h]
            o_ref[rows, lanes] += _dot_nt(q_dec[rows, lanes], s_t.astype(BF))
            state_ref[h] = decay[:, lanes] * s_t + _dot_tn(v_bf[rows, lanes], k_end[:, lanes])


def _hgrn_sgu_kernel(starts_ref, ends_ref,
                     qf_ref, ff_ref, vf_ref, qb_ref, fb_ref, vb_ref, u_ref, v_ref,
                     lbp_ref, lng_ref, lnb_ref, ws_ref, bs_ref,
                     of_ref, ob_ref, sgu_ref, sf_ref, sb_ref, *, layer, t):
    i = pl.program_id(0)
    j = pl.num_programs(0) - 1 - i

    @pl.when(starts_ref[i] == 1)
    def _():
        sf_ref[...] = jnp.zeros_like(sf_ref)

    @pl.when(ends_ref[j] == 1)
    def _():
        sb_ref[...] = jnp.zeros_like(sb_ref)

    lbp = lbp_ref[...]
    e = jnp.exp(lbp - jnp.max(lbp, axis=0, keepdims=True))
    sm = e / jnp.sum(e, axis=0, keepdims=True)
    lb = jnp.sum(sm[:layer + 1, :], axis=0, keepdims=True)

    n_sub = t // HGRN_SUB
    for sb in range(n_sub):
        rows = slice(sb * HGRN_SUB, (sb + 1) * HGRN_SUB)
        _hgrn_direction(qf_ref[rows, :], ff_ref[rows, :], vf_ref[rows, :], lb, sf_ref, of_ref.at[rows, :],
                        False, HGRN_SUB)
    for sb in range(n_sub - 1, -1, -1):
        rows = slice(sb * HGRN_SUB, (sb + 1) * HGRN_SUB)
        _hgrn_direction(qb_ref[rows, :], fb_ref[rows, :], vb_ref[rows, :], lb, sb_ref, ob_ref.at[rows, :],
                        True, HGRN_SUB)

    u = jax.nn.gelu(u_ref[...])
    v = jax.nn.gelu(v_ref[...])
    for gi in range(MIX_W // HEAD_W):
        lanes = slice(gi * HEAD_W, (gi + 1) * HEAD_W)
        vg = v[:, lanes]
        mu = jnp.mean(vg, axis=-1, keepdims=True)
        var = jnp.mean(jnp.square(vg - mu), axis=-1, keepdims=True)
        vln = ((vg - mu) * lax.rsqrt(var + EPS) * lng_ref[:, lanes] + lnb_ref[:, lanes]).astype(BF)
        for c in range(t // SGU_CHUNK):
            rows = slice(c * SGU_CHUNK, (c + 1) * SGU_CHUNK)
            mixed = _dot(ws_ref[gi], vln[rows, :]) + bs_ref[gi]
            sgu_ref[rows, lanes] = u[rows, lanes] * mixed


def _hgrn_sgu(z, hgrn_lb, ln_g, ln_b, w_s, b_s, groups, layer, t):
    n = z.shape[0]
    nblk = n // t
    starts, ends, _ = _block_tables(groups, t)
    fwd = lambda c: pl.BlockSpec((t, MIX_W), lambda i, s, e, c=c: (i, c))
    bwd = lambda c: pl.BlockSpec((t, MIX_W), lambda i, s, e, c=c: (nblk - 1 - i, c))
    full = lambda shape: pl.BlockSpec(shape, lambda i, s, e: (0,) * len(shape))
    bs_b = jnp.broadcast_to(b_s[:, :, None], b_s.shape + (HEAD_W,)).astype(F32)
    out = jax.ShapeDtypeStruct((n, MIX_W), F32)
    return pl.pallas_call(
        functools.partial(_hgrn_sgu_kernel, layer=layer, t=t),
        out_shape=(out, out, out),
        grid_spec=pltpu.PrefetchScalarGridSpec(
            num_scalar_prefetch=2,
            grid=(nblk,),
            in_specs=[fwd(0), fwd(1), fwd(3), bwd(0), bwd(2), bwd(3), fwd(5), fwd(6),
                      full(hgrn_lb.shape), full((1, MIX_W)), full((1, MIX_W)),
                      full(w_s.shape), full(bs_b.shape)],
            out_specs=[fwd(0), bwd(0), fwd(0)],
            scratch_shapes=[pltpu.VMEM((MIX_W // HEAD_W, HEAD_W, HEAD_W), F32),
                            pltpu.VMEM((MIX_W // HEAD_W, HEAD_W, HEAD_W), F32)],
        ),
        compiler_params=_cparams(1),
        name="hgrn_sgu",
    )(jnp.asarray(starts), jnp.asarray(ends), z, z, z, z, z, z, z, z,
      hgrn_lb, ln_g.reshape(1, -1), ln_b.reshape(1, -1), w_s.astype(BF), bs_b)


def _mix0_ffn_kernel(xa_ref, xb_ref, of_ref, ob_ref, gate_ref, sgu_ref, gn_ref, wo_ref, g2_ref,
                     w1_ref, w3_ref, w2_ref, o_ref, hh_ref, *, f_chunk, na):
    o = of_ref[...] + ob_ref[...]
    gate = gate_ref[...]
    parts = []
    for h in range(MIX_W // HEAD_W):
        lanes = slice(h * HEAD_W, (h + 1) * HEAD_W)
        parts.append(_rms_rows(o[:, lanes], gn_ref[...]) * _silu(gate[:, lanes]))
    parts.append(sgu_ref[...])
    mix_in = jnp.concatenate(parts, axis=-1).astype(BF)
    x1 = _pair_block(pl.program_id(0), na, xa_ref, xb_ref) + _dot(mix_in, wo_ref[...])
    h2 = _rms_rows(x1, g2_ref[...]).astype(BF)
    d_ff = w1_ref.shape[1]
    for c in range(d_ff // f_chunk):
        cols = slice(c * f_chunk, (c + 1) * f_chunk)
        hh_ref[:, cols] = (_silu(_dot(h2, w1_ref[:, cols])) * _dot(h2, w3_ref[:, cols])).astype(BF)
    o_ref[...] = x1 + _dot(hh_ref[...], w2_ref[...])


def _resident(shape):
    return pl.BlockSpec(shape, lambda i: (0,) * len(shape), pipeline_mode=pl.Buffered(1))


def _mix0_ffn(xa, xb, z, o_f, o_b, sgu, gnorm, w_out, g2, w1, w3, w2, tm):
    n, d = xa.shape[0] + xb.shape[0], xa.shape[1]
    d_ff = w1.shape[1]
    tok = lambda w, c=0: pl.BlockSpec((tm, w), lambda i, c=c: (i, c))
    spec_a, spec_b, na = _pair_specs(xa, xb, tm)
    return pl.pallas_call(
        functools.partial(_mix0_ffn_kernel, f_chunk=d_ff // 2, na=na),
        out_shape=jax.ShapeDtypeStruct((n, d), F32),
        grid=(n // tm,),
        in_specs=[spec_a, spec_b, tok(MIX_W), tok(MIX_W), tok(MIX_W, 4), tok(MIX_W),
                  _resident((1, HEAD_W)), _resident(w_out.shape), _resident((1, d)),
                  _resident(w1.shape), _resident(w3.shape), _resident(w2.shape)],
        out_specs=tok(d),
        scratch_shapes=[pltpu.VMEM((tm, d_ff), BF)],
        compiler_params=_cparams(1),
        name="mix0_ffn",
    )(xa, xb, o_f, o_b, z, sgu, gnorm.reshape(1, -1), w_out, g2.reshape(1, -1), w1, w3, w2)


def _rope_tables(s_max):
    half = ROPE_DIMS // 2
    dim = np.arange(LANES) % D_QK
    lo = jnp.asarray(dim < half)[None, :]
    hi = jnp.asarray((dim >= half) & (dim < ROPE_DIMS))[None, :]
    inv_freq = ROPE_THETA ** (-jnp.asarray(2 * (dim % half), F32) / ROPE_DIMS)
    ang = jnp.arange(s_max, dtype=F32)[:, None] * inv_freq[None, :]
    cos, sin = jnp.cos(ang), jnp.sin(ang)
    return (jnp.where(lo | hi, cos, 1.0), jnp.where(lo, -sin, 0.0), jnp.where(hi, sin, 0.0))


def _qk_norm_rope(x, gain, seg, cos, s_lo, s_hi, scale):
    half = ROPE_DIMS // 2
    sq_hi, sq_lo = _split2(x * x)
    ms = _dot(jnp.concatenate([sq_hi, sq_lo], axis=1), seg)
    xn = x * lax.rsqrt(ms + EPS) * gain
    up = pltpu.roll(xn, LANES - half, axis=1)
    down = pltpu.roll(xn, half, axis=1)
    return ((xn * cos + up * s_lo + down * s_hi) * scale).astype(BF)


def _proj1_kernel(pos_ref, x_ref, g_ref, w_ref, qg_ref, kg_ref, cos_ref, slo_ref, shi_ref,
                  p_ref, q_ref, k_ref, vt_ref):
    h = _rms_rows(x_ref[...], g_ref[...]).astype(BF)
    vt_ref[...] = _dot(h, w_ref[:, 3 * MIX_W:]).T.astype(BF)
    r = lax.broadcasted_iota(jnp.int32, (2 * LANES, LANES), 0) % LANES
    c = lax.broadcasted_iota(jnp.int32, (2 * LANES, LANES), 1)
    seg = jnp.where((r // D_QK) == (c // D_QK), 1.0 / D_QK, 0.0).astype(BF)
    cos, s_lo, s_hi = cos_ref[...], slo_ref[...], shi_ref[...]
    for out_ref, gain_ref, col0, scale in ((q_ref, qg_ref, MIX_W, Q_SCALE), (k_ref, kg_ref, 2 * MIX_W, 1.0)):
        z = _dot(h, w_ref[:, col0:col0 + MIX_W])
        for hd in range(MIX_W // HEAD_W):
            lanes = slice(hd * HEAD_W, (hd + 1) * HEAD_W)
            out_ref[:, lanes] = _qk_norm_rope(z[:, lanes], gain_ref[...], seg, cos, s_lo, s_hi, scale)
    p_ref[...] = _dot(h, w_ref[:, :MIX_W])


def _proj1(x, gain, w, q_gain, k_gain, groups, tm):
    n, d = x.shape
    e = w.shape[1]
    _, _, pos = _block_tables(groups, tm)
    s_max = max(s for _, s in groups)
    cos, s_lo, s_hi = _rope_tables(s_max)
    tok = lambda width: pl.BlockSpec((tm, width), lambda i, p: (i, 0))
    full = lambda shape: pl.BlockSpec(shape, lambda i, p: (0,) * len(shape))
    rope = pl.BlockSpec((tm, LANES), lambda i, p: (p[i], 0))
    tile2 = lambda g: jnp.concatenate([g, g]).reshape(1, LANES)
    return pl.pallas_call(
        _proj1_kernel,
        out_shape=(jax.ShapeDtypeStruct((n, MIX_W), F32), jax.ShapeDtypeStruct((n, MIX_W), BF),
                   jax.ShapeDtypeStruct((n, MIX_W), BF), jax.ShapeDtypeStruct((MIX_W, n), BF)),
        grid_spec=pltpu.PrefetchScalarGridSpec(
            num_scalar_prefetch=1,
            grid=(n // tm,),
            in_specs=[tok(d), full((1, d)), full((d, e)), full((1, LANES)), full((1, LANES)),
                      rope, rope, rope],
            out_specs=[tok(MIX_W)] * 3 + [pl.BlockSpec((MIX_W, tm), lambda i, p: (0, i))],
        ),
        compiler_params=_cparams(1),
        name="proj1_qk_rope",
    )(jnp.asarray(pos), x, gain.reshape(1, -1), w, tile2(q_gain), tile2(k_gain), cos, s_lo, s_hi)


def _pool_block(is_start, is_end, prev_ref, cur_ref, next_ref, wc_ref, scale_ref, t):
    cur = cur_ref[...]
    prev = jnp.where(is_start, 0.0, prev_ref[...])
    nxt = jnp.where(is_end, 0.0, next_ref[...])
    ext = jnp.concatenate([prev, cur, nxt], axis=0)
    ext_hi, ext_lo = _split2(ext)
    sub = min(t, POOL_SUB)
    r = lax.broadcasted_iota(jnp.int32, (sub, sub + 2 * POOL_HALO), 0) + POOL_HALO
    c = lax.broadcasted_iota(jnp.int32, (sub, sub + 2 * POOL_HALO), 1)
    row = lax.broadcasted_iota(jnp.int32, (t, 1), 0)
    parts = []
    for gi, w in enumerate(POOL_WINDOWS):
        lanes = slice(gi * HEAD_W, (gi + 1) * HEAD_W)
        hw = w // 2
        band = jnp.where(jnp.logical_and(c >= r - hw, c < r + hw), 1.0, 0.0).astype(BF)
        strips = []
        for s0 in range(0, t, sub):
            rows = slice(s0, s0 + sub + 2 * POOL_HALO)
            both = _dot(band, jnp.concatenate([ext_hi[rows, lanes], ext_lo[rows, lanes]], axis=1))
            strips.append(both[:, :HEAD_W] + both[:, HEAD_W:])
        win = jnp.concatenate(strips, axis=0)
        lo_cut = jnp.where(is_start, jnp.maximum(hw - row, 0), 0)
        hi_cut = jnp.where(is_end, jnp.maximum(row + hw - t, 0), 0)
        cnt = (w - lo_cut - hi_cut).astype(F32)
        diff = win / cnt - cur[:, lanes]
        parts.append(_dot(diff.astype(BF), wc_ref[gi]) * scale_ref[:, lanes])
    return jnp.concatenate(parts, axis=-1)


def _diff_attn_kernel(qb_ref, kb_ref, first_ref, last_ref,
                      q_ref, k_ref, vt_ref, lq1_ref, lk1_ref, lq2_ref, lk2_ref, sub_ref, bound_ref,
                      o_ref, qzt_ref, v1_ref, m_ref, acc_ref, *, tq, tkc, n_chunks, unroll, lam_init):
    p = pl.program_id(1)

    @pl.when(first_ref[p] == 1)
    def _():
        qt = q_ref[...].astype(F32).T.astype(BF)
        row = lax.broadcasted_iota(jnp.int32, qt.shape, 0)
        qzt_ref[:, :tq] = jnp.where(row < D_QK, qt, jnp.zeros_like(qt))
        qzt_ref[:, tq:] = jnp.where(row >= D_QK, qt, jnp.zeros_like(qt))
        for slot in range(unroll):
            v1_ref[slot, HEAD_W:, :] = jnp.ones((ONES_ROWS, tkc), BF)
        m_ref[...] = jnp.full_like(m_ref, -jnp.inf)
        acc_ref[...] = jnp.zeros_like(acc_ref)

    def chunk(j, slot):
        off = pl.multiple_of(j * tkc, tkc)
        v1_ref[slot, :HEAD_W, :] = vt_ref[:, pl.ds(off, tkc)]
        st = _dot(k_ref[pl.ds(off, tkc), :], qzt_ref[...])
        m_old = m_ref[...]
        m_new = jnp.maximum(m_old, jnp.max(st, axis=0, keepdims=True))
        alpha = jnp.exp2(m_old - m_new)
        pt = jnp.exp2(st - m_new).astype(BF)
        acc_ref[...] = alpha * acc_ref[...] + _dot(v1_ref[slot], pt)
        m_ref[...] = m_new

    def chunks(i, carry):
        for slot in range(unroll):
            chunk(i * unroll + slot, slot)
        return carry

    bound = bound_ref[0, 0]
    fixed_shift_ok = bound <= MAX_FIXED_SHIFT

    def fixed_chunks(i, carry):
        total = None
        for slot in range(unroll):
            off = pl.multiple_of((i * unroll + slot) * tkc, tkc)
            v1_ref[slot, :HEAD_W, :] = vt_ref[:, pl.ds(off, tkc)]
            st = _dot(k_ref[pl.ds(off, tkc), :], qzt_ref[...])
            part = _dot(v1_ref[slot], jnp.exp2(st - bound).astype(BF))
            total = part if total is None else total + part
        acc_ref[...] += total
        return carry

    @pl.when(fixed_shift_ok)
    def _():
        lax.fori_loop(0, n_chunks // unroll, fixed_chunks, 0)

    @pl.when(jnp.logical_not(fixed_shift_ok))
    def _():
        lax.fori_loop(0, n_chunks // unroll, chunks, 0)

    @pl.when(last_ref[p] == 1)
    def _():
        lam = (jnp.exp(jnp.sum(lq1_ref[...] * lk1_ref[...], keepdims=True))
               - jnp.exp(jnp.sum(lq2_ref[...] * lk2_ref[...], keepdims=True)) + lam_init)
        acc = acc_ref[...]
        o = acc[:HEAD_W, :] / acc[HEAD_W:HEAD_W + 1, :]
        d = o[:, :tq] - lam * o[:, tq:]
        ms = jnp.mean(d * d, axis=0, keepdims=True)
        y = d * lax.rsqrt(ms + EPS) * sub_ref[...] * (1.0 - lam_init)
        o_ref[...] = y.T


def _diff_attn(q, k, vt, q_gain, k_gain, lq1, lk1, lq2, lk2, subln, groups, lam_init, tq, tkb, tkc):
    n = q.shape[0]
    bound = (1.01 * D_QK * Q_SCALE * jnp.max(jnp.abs(q_gain)) * jnp.max(jnp.abs(k_gain))).reshape(1, 1)
    qb, kb, first, last = _attn_tables(groups, tq, tkb)
    n_heads = MIX_W // HEAD_W
    vec = lambda a: a.reshape(1, -1)
    full = lambda shape: pl.BlockSpec(shape, lambda h, p, *_: (0, 0))
    sub_col = jnp.broadcast_to(subln[:, None], (HEAD_W, tq))
    n_chunks = tkb // tkc
    unroll = next(u for u in (4, 2, 1) if n_chunks % u == 0)
    return pl.pallas_call(
        functools.partial(_diff_attn_kernel, tq=tq, tkc=tkc, n_chunks=n_chunks, unroll=unroll,
                          lam_init=lam_init),
        out_shape=jax.ShapeDtypeStruct((n, MIX_W), F32),
        grid_spec=pltpu.PrefetchScalarGridSpec(
            num_scalar_prefetch=4,
            grid=(n_heads, len(qb)),
            in_specs=[
                pl.BlockSpec((tq, HEAD_W), lambda h, p, qb, kb, f, l: (qb[p], h)),
                pl.BlockSpec((tkb, HEAD_W), lambda h, p, qb, kb, f, l: (kb[p], h)),
                pl.BlockSpec((HEAD_W, tkb), lambda h, p, qb, kb, f, l: (h, kb[p])),
                full((1, D_QK)), full((1, D_QK)), full((1, D_QK)), full((1, D_QK)), full((HEAD_W, tq)),
                pl.BlockSpec(memory_space=pltpu.SMEM),
            ],
            out_specs=pl.BlockSpec((tq, HEAD_W), lambda h, p, qb, kb, f, l: (qb[p], h)),
            scratch_shapes=[pltpu.VMEM((HEAD_W, 2 * tq), BF),
                            pltpu.VMEM((unroll, HEAD_W + ONES_ROWS, tkc), BF),
                            pltpu.VMEM((1, 2 * tq), F32), pltpu.VMEM((HEAD_W + ONES_ROWS, 2 * tq), F32)],
        ),
        compiler_params=_cparams(2),
        name="diff_attn",
    )(jnp.asarray(qb), jnp.asarray(kb), jnp.asarray(first), jnp.asarray(last),
      q, k, vt, vec(lq1), vec(lk1), vec(lq2), vec(lk2), sub_col, bound.astype(F32))


def _mix1_route_kernel(starts_ref, ends_ref, x_ref, pprev_ref, p_ref, pnext_ref, a_ref, wc_ref, ps_ref,
                       wo_ref, g_ref, wr_ref,
                       x3_ref, h_ref, eidx_ref, gate_ref, rank_ref, cnt_ref, base_ref, *, tm):
    i = pl.program_id(0)

    @pl.when(i == 0)
    def _():
        base_ref[...] = jnp.zeros_like(base_ref)

    c_out = _pool_block(starts_ref[i] == 1, ends_ref[i] == 1, pprev_ref, p_ref, pnext_ref, wc_ref, ps_ref, tm)
    mix_in = jnp.concatenate([c_out, a_ref[...]], axis=-1).astype(BF)
    x3 = x_ref[...] + _dot(mix_in, wo_ref[...])
    x3_ref[...] = x3
    h = _rms_rows(x3, g_ref[...])
    h_ref[...] = h
    wr = wr_ref[...]
    wr_hi = wr.astype(BF).astype(F32)
    wr_parts = jnp.concatenate([wr_hi, wr - wr_hi], axis=0).astype(BF)
    h_hi, h_lo = _split2(h)
    by_hi = _dot_nt(wr_parts, h_hi)
    logits = by_hi[:N_EXPERTS] + by_hi[N_EXPERTS:] + _dot_nt(wr_hi.astype(BF), h_lo)
    eid = lax.broadcasted_iota(jnp.int32, logits.shape, 0)
    m1 = jnp.max(logits, axis=0, keepdims=True)
    i1 = jnp.min(jnp.where(logits == m1, eid, N_EXPERTS), axis=0, keepdims=True)
    rest = jnp.where(eid == i1, -jnp.inf, logits)
    m2 = jnp.max(rest, axis=0, keepdims=True)
    i2 = jnp.min(jnp.where(rest == m2, eid, N_EXPERTS), axis=0, keepdims=True)
    e2 = jnp.exp(m2 - m1)
    g1 = 1.0 / (1.0 + e2)
    g2 = e2 / (1.0 + e2)
    sel1 = eid == i1
    sel2 = eid == i2
    onehot = jnp.where(jnp.logical_or(sel1, sel2), 1.0, 0.0)
    r = lax.broadcasted_iota(jnp.int32, (tm, tm), 0)
    c = lax.broadcasted_iota(jnp.int32, (tm, tm), 1)
    incl = _dot(onehot.astype(BF), jnp.where(r <= c, 1.0, 0.0).astype(BF))
    before = base_ref[:, :1] + incl - onehot
    eidx_ref[0:1, :] = i1
    eidx_ref[1:2, :] = i2
    gate_ref[0:1, :] = g1
    gate_ref[1:2, :] = g2
    rank_ref[0:1, :] = jnp.sum(jnp.where(sel1, before, 0.0), axis=0, keepdims=True).astype(jnp.int32)
    rank_ref[1:2, :] = jnp.sum(jnp.where(sel2, before, 0.0), axis=0, keepdims=True).astype(jnp.int32)
    base_ref[...] = base_ref[...] + jnp.sum(onehot, axis=1, keepdims=True)
    cnt_ref[...] = base_ref[...].astype(jnp.int32)


def _mix1_route(x, pz, att, pool_w, pool_scale, w_out, g2, w_router, groups, tm):
    n, d = x.shape
    starts, ends, _ = _block_tables(groups, tm)
    per = tm // POOL_HALO
    n_halo = n // POOL_HALO
    tok = lambda width: pl.BlockSpec((tm, width), lambda i, s, e: (i, 0))
    lane_tok = pl.BlockSpec((2, tm), lambda i, s, e: (0, i))
    full = lambda shape: pl.BlockSpec(shape, lambda i, s, e: (0,) * len(shape), pipeline_mode=pl.Buffered(1))
    return pl.pallas_call(
        functools.partial(_mix1_route_kernel, tm=tm),
        out_shape=(jax.ShapeDtypeStruct((n, d), F32), jax.ShapeDtypeStruct((n, d), F32),
                   jax.ShapeDtypeStruct((2, n), jnp.int32), jax.ShapeDtypeStruct((2, n), F32),
                   jax.ShapeDtypeStruct((2, n), jnp.int32),
                   jax.ShapeDtypeStruct((N_EXPERTS, LANES), jnp.int32)),
        grid_spec=pltpu.PrefetchScalarGridSpec(
            num_scalar_prefetch=2,
            grid=(n // tm,),
            in_specs=[tok(d),
                      pl.BlockSpec((POOL_HALO, MIX_W), lambda i, s, e: (jnp.maximum(i * per - 1, 0), 0)),
                      tok(MIX_W),
                      pl.BlockSpec((POOL_HALO, MIX_W),
                                   lambda i, s, e: (jnp.minimum((i + 1) * per, n_halo - 1), 0)),
                      tok(MIX_W), full(pool_w.shape), full((1, MIX_W)),
                      full(w_out.shape), full((1, d)), full((N_EXPERTS, d))],
            out_specs=[tok(d), tok(d), lane_tok, lane_tok, lane_tok,
                       pl.BlockSpec((N_EXPERTS, LANES), lambda i, s, e: (0, 0))],
            scratch_shapes=[pltpu.VMEM((N_EXPERTS, LANES), F32)],
        ),
        compiler_params=_cparams(1),
        name="mix1_route",
    )(jnp.asarray(starts), jnp.asarray(ends), x, pz, pz, pz, att, pool_w.astype(BF), pool_scale.reshape(1, -1),
      w_out, g2.reshape(1, -1), w_router.T)


def _experts_kernel(be_ref, nused_ref, row0_ref, rown_ref, rowp_ref, h_hbm, w1_hbm, w3_hbm, w2_hbm, y_hbm,
                    xbuf, ybuf, w1_v, w3_v, w2_v, stage_in, stage_out, gsem, ssem, wsem, *, mb, f_chunk, n_rows):
    b = pl.program_id(0)
    cur = b % 2
    oth = 1 - cur
    e = be_ref[b]
    prev_e = be_ref[jnp.maximum(b - 1, 0)]
    d_e = w1_v.shape[1]
    n_chunks = d_e // f_chunk

    n_tok = n_rows // 2

    def gather_row(row_ref, r, slot):
        v = row_ref[0, 0, r]
        tok = v - jnp.where(v >= n_rows, n_rows, jnp.where(v >= n_tok, n_tok, 0))
        pltpu.make_async_copy(h_hbm.at[pl.ds(tok, 1), :],
                              xbuf.at[slot, pl.ds(r, 1), :], gsem.at[slot]).start()

    def scatter_row(r, slot):
        pltpu.make_async_copy(ybuf.at[slot, pl.ds(r, 1), :],
                              y_hbm.at[pl.ds(rowp_ref[0, 0, r], 1), :], ssem.at[slot]).start()

    def load_weights():
        pieces = []
        for c in range(n_chunks):
            cols = pl.ds(c * f_chunk, f_chunk)
            pieces.append((w1_hbm.at[e, :, cols], stage_in, w1_v.at[:, cols]))
            pieces.append((w3_hbm.at[e, :, cols], stage_in, w3_v.at[:, cols]))
            pieces.append((w2_hbm.at[e, cols, :], stage_out, w2_v.at[cols, :]))

        def copy(i):
            src, stage, _ = pieces[i]
            return pltpu.make_async_copy(src, stage.at[i % 2], wsem.at[i % 2])

        copy(0).start()
        for i, (_, stage, dst) in enumerate(pieces):
            if i + 1 < len(pieces):
                copy(i + 1).start()
            copy(i).wait()
            dst[...] = stage[i % 2].astype(BF)

    @pl.when(b == 0)
    def _():
        ybuf[...] = jnp.zeros_like(ybuf)
        zero_tail = pltpu.make_async_copy(ybuf.at[0], y_hbm.at[pl.ds(n_rows, mb), :], wsem.at[0])
        zero_tail.start()
        zero_tail.wait()
        lax.fori_loop(0, mb, lambda r, c: (gather_row(row0_ref, r, 0), c)[1], 0)

    live = b <= nused_ref[0]

    @pl.when(live)
    def _():
        pltpu.make_async_copy(h_hbm.at[pl.ds(0, mb), :], xbuf.at[cur], gsem.at[cur]).wait()

    @pl.when(b < nused_ref[0])
    def _():
        new_expert = jnp.logical_or(b == 0, e != prev_e)

        @pl.when(new_expert)
        def _():
            load_weights()

        x = xbuf[cur].astype(BF)
        acc = jnp.zeros((mb, D_MODEL), F32)
        for c in range(n_chunks):
            cols = slice(c * f_chunk, (c + 1) * f_chunk)
            hh = _silu(_dot(x, w1_v[:, cols])) * _dot(x, w3_v[:, cols])
            acc = acc + _dot(hh.astype(BF), w2_v[cols, :])
            n_dma = n_chunks - 1
            for r in range(min(c, n_dma) * mb // n_dma, min(c + 1, n_dma) * mb // n_dma):
                gather_row(rown_ref, r, oth)
                scatter_row(r, oth)
        ybuf[cur] = acc

    @pl.when(b == nused_ref[0])
    def _():
        lax.fori_loop(0, mb, lambda r, c: (scatter_row(r, oth), c)[1], 0)

    @pl.when(live)
    def _():
        pltpu.make_async_copy(ybuf.at[oth], y_hbm.at[pl.ds(0, mb), :], ssem.at[oth]).wait()


def _experts(h, slot_row, block_e, n_used, w1, w3, w2, mb, n_rows):
    n_steps = block_e.shape[0]
    d = h.shape[1]
    d_e = w1.shape[2]
    f_chunk = 512
    any_spec = pl.BlockSpec(memory_space=pl.ANY)
    smem_blk = lambda fn: pl.BlockSpec((1, 1, mb), fn, memory_space=pltpu.SMEM)
    return pl.pallas_call(
        functools.partial(_experts_kernel, mb=mb, f_chunk=f_chunk, n_rows=n_rows),
        out_shape=jax.ShapeDtypeStruct((n_rows + mb, d), F32),
        grid_spec=pltpu.PrefetchScalarGridSpec(
            num_scalar_prefetch=2,
            grid=(n_steps,),
            in_specs=[smem_blk(lambda b, be, nu: (0, 0, 0)),
                      smem_blk(lambda b, be, nu: (jnp.minimum(b + 1, n_steps - 1), 0, 0)),
                      smem_blk(lambda b, be, nu: (jnp.maximum(b - 1, 0), 0, 0)),
                      any_spec, any_spec, any_spec, any_spec],
            out_specs=any_spec,
            scratch_shapes=[pltpu.VMEM((2, mb, d), F32), pltpu.VMEM((2, mb, d), F32),
                            pltpu.VMEM((d, d_e), BF), pltpu.VMEM((d, d_e), BF), pltpu.VMEM((d_e, d), BF),
                            pltpu.VMEM((2, d, f_chunk), F32), pltpu.VMEM((2, f_chunk, d), F32),
                            pltpu.SemaphoreType.DMA((2,)), pltpu.SemaphoreType.DMA((2,)),
                            pltpu.SemaphoreType.DMA((2,))],
        ),
        compiler_params=_cparams(1),
        name="experts",
    )(block_e, n_used, slot_row, slot_row, slot_row, h, w1, w3, w2)


def _combine_kernel(x_ref, gate_ref, y0_ref, y1_ref, oa_ref, ob_ref, *, tm, na):
    i = pl.program_id(0)
    g = jnp.concatenate([gate_ref[...], jnp.zeros((6, tm), F32)], axis=0).T
    out = x_ref[...] + g[:, 0:1] * y0_ref[...] + g[:, 1:2] * y1_ref[...]

    @pl.when(i < na)
    def _():
        oa_ref[...] = out

    @pl.when(i >= na)
    def _():
        ob_ref[...] = out


def _combine(x, gates, y, n_first, tm):
    n, d = x.shape
    nblk = n // tm
    na = n_first // tm
    return pl.pallas_call(
        functools.partial(_combine_kernel, tm=tm, na=na),
        out_shape=(jax.ShapeDtypeStruct((n_first, d), F32), jax.ShapeDtypeStruct((n - n_first, d), F32)),
        grid=(nblk,),
        in_specs=[pl.BlockSpec((tm, d), lambda i: (i, 0)),
                  pl.BlockSpec((2, tm), lambda i: (0, i)),
                  pl.BlockSpec((tm, d), lambda i: (i, 0)),
                  pl.BlockSpec((tm, d), lambda i: (i + nblk, 0))],
        out_specs=[pl.BlockSpec((tm, d), lambda i: (jnp.minimum(i, na - 1), 0)),
                   pl.BlockSpec((tm, d), lambda i: (jnp.maximum(i - na, 0), 0))],
        compiler_params=_cparams(1),
        name="combine",
    )(x, gates, y, y)


def _moe(x3, h, eidx, gates, rank, counts, w1, w3, w2, n_first, mb, tm):
    n = x3.shape[0]
    n_blocks = -(-2 * n // mb) + N_EXPERTS
    cnt = counts[:, 0]
    padded = ((cnt + mb - 1) // mb) * mb
    pends = jnp.cumsum(padded)
    pstarts = pends - padded
    start_of = sum(jnp.where(eidx == e, pstarts[e], 0) for e in range(N_EXPERTS))
    dest = (start_of + rank).reshape(-1)
    rows = jnp.arange(2 * n, dtype=jnp.int32)
    n_slots = (n_blocks + 1) * mb
    spare = 2 * n + jnp.arange(n_slots, dtype=jnp.int32) % mb
    slot_row = spare.at[dest].set(rows, unique_indices=True)
    blk_start = jnp.arange(n_blocks + 1, dtype=jnp.int32) * mb
    block_e = jnp.minimum(jnp.searchsorted(pends, blk_start, side="right"), N_EXPERTS - 1).astype(jnp.int32)
    n_used = (pends[-1] // mb).astype(jnp.int32).reshape(1)
    y = _experts(h, slot_row.reshape(n_blocks + 1, 1, mb), block_e, n_used, w1, w3, w2, mb, 2 * n)
    return _combine(x3, gates, y, n_first, tm)


def _trunk(xa, xb, groups, p):
    bf = lambda a: a.astype(BF)
    z = _norm_proj(xa, xb, p["e_norm1"][0].reshape(1, -1), bf(p["e_w_in"][0]), tm=512)
    o_f, o_b, sgu = _hgrn_sgu(z, p["hgrn_lb"], p["e_sgu_ln_g"][0], p["e_sgu_ln_b"][0],
                              p["e_sgu_w"][0], p["e_sgu_b"][0], groups, layer=0, t=512)
    x = _mix0_ffn(xa, xb, z, o_f, o_b, sgu, p["e_hgrn_gnorm"][0], bf(p["e_w_out"][0]), p["e_norm2"][0],
                  bf(p["e_ffn_w1"][0]), bf(p["e_ffn_w3"][0]), bf(p["e_ffn_w2"][0]), tm=512)
    layer = 1
    lam_init = 0.8 - 0.6 * math.exp(-0.3 * layer)
    pz, q, k, vt = _proj1(x, p["o_norm1"][0], bf(p["o_w_in"][0]), p["o_q_norm"][0], p["o_k_norm"][0],
                          groups, tm=512)
    tkb = min(4096, math.gcd(*[s for _, s in groups]))
    att = _diff_attn(q, k, vt, p["o_q_norm"][0], p["o_k_norm"][0], p["o_lambda_q1"][0], p["o_lambda_k1"][0], p["o_lambda_q2"][0],
                     p["o_lambda_k2"][0], p["o_subln"][0], groups, lam_init, tq=min(1024, tkb), tkb=tkb,
                     tkc=min(512, tkb))
    x3, h, eidx, gates, rank, counts = _mix1_route(x, pz, att, p["o_pool_w"][0], p["o_pool_scale"][0],
                                                   bf(p["o_w_out"][0]), p["o_norm2"][0], p["o_router"][0],
                                                   groups, tm=512)
    return _moe(x3, h, eidx, gates, rank, counts, p["o_moe_w1"][0], p["o_moe_w3"][0], p["o_moe_w2"][0],
                n_first=xa.shape[0], mb=512, tm=256)


def kernel(x_prompt, x_sample, hgrn_lb, e_norm1, e_w_in, e_hgrn_gnorm, e_sgu_ln_g, e_sgu_ln_b, e_sgu_w, e_sgu_b, e_w_out, e_norm2, e_ffn_w1, e_ffn_w3, e_ffn_w2, o_norm1, o_w_in, o_pool_w, o_pool_scale, o_q_norm, o_k_norm, o_lambda_q1, o_lambda_k1, o_lambda_q2, o_lambda_k2, o_subln, o_w_out, o_norm2, o_router, o_moe_w1, o_moe_w3, o_moe_w2):
    params = dict(
        hgrn_lb=hgrn_lb, e_norm1=e_norm1, e_w_in=e_w_in, e_hgrn_gnorm=e_hgrn_gnorm,
        e_sgu_ln_g=e_sgu_ln_g, e_sgu_ln_b=e_sgu_ln_b, e_sgu_w=e_sgu_w, e_sgu_b=e_sgu_b,
        e_w_out=e_w_out, e_norm2=e_norm2, e_ffn_w1=e_ffn_w1, e_ffn_w3=e_ffn_w3, e_ffn_w2=e_ffn_w2,
        o_norm1=o_norm1, o_w_in=o_w_in, o_pool_w=o_pool_w, o_pool_scale=o_pool_scale,
        o_q_norm=o_q_norm, o_k_norm=o_k_norm, o_lambda_q1=o_lambda_q1, o_lambda_k1=o_lambda_k1,
        o_lambda_q2=o_lambda_q2, o_lambda_k2=o_lambda_k2, o_subln=o_subln, o_w_out=o_w_out,
        o_norm2=o_norm2, o_router=o_router, o_moe_w1=o_moe_w1, o_moe_w3=o_moe_w3, o_moe_w2=o_moe_w2,
    )
    d = x_prompt.shape[-1]
    groups = (x_prompt.shape[:2], x_sample.shape[:2])
    y_p, y_s = _trunk(x_prompt.reshape(-1, d), x_sample.reshape(-1, d), groups, params)
    return (y_p.reshape(x_prompt.shape), y_s.reshape(x_sample.shape))
```

```python
import functools
import math

import numpy as np
import jax
import jax.numpy as jnp
from jax import lax
from jax.experimental import pallas as pl
from jax.experimental.pallas import tpu as pltpu

F32 = jnp.float32
BF = jnp.bfloat16

D_MODEL = 1024
EPS = 1e-6
LANES = 128
HEAD_W = 128
MIX_W = 512
HGRN_CHUNK = 64
HGRN_SUB = 256
SGU_CHUNK = 128
POOL_WINDOWS = (2, 4, 8, 16)
POOL_HALO = 16
POOL_SUB = 128
D_QK = 64
Q_SCALE = D_QK ** -0.5 * math.log2(math.e)
ONES_ROWS = 16
MAX_FIXED_SHIFT = 40.0
ROPE_DIMS = 16
ROPE_THETA = 500000.0
N_EXPERTS = 8
VMEM_LIMIT = 56 * 1024 * 1024


def _cparams(n_axes, vmem=VMEM_LIMIT):
    return pltpu.CompilerParams(dimension_semantics=("arbitrary",) * n_axes, vmem_limit_bytes=vmem)


def _dot(a, b):
    return jnp.dot(a, b, preferred_element_type=F32)


def _dot_nt(a, b):
    return lax.dot_general(a, b, (((1,), (1,)), ((), ())), preferred_element_type=F32)


def _dot_tn(a, b):
    return lax.dot_general(a, b, (((0,), (0,)), ((), ())), preferred_element_type=F32)


def _split2(x):
    hi = x.astype(BF)
    lo = (x - hi.astype(F32)).astype(BF)
    return hi, lo


def _dot01(a01, x):
    hi, lo = _split2(x)
    return _dot(a01, hi) + _dot(a01, lo)


def _rms_rows(x, gain):
    ms = jnp.mean(x * x, axis=-1, keepdims=True)
    return x * lax.rsqrt(ms + EPS) * gain


def _silu(x):
    h = 0.5 * x
    return h * jnp.tanh(h) + h


def _block_tables(groups, t):
    starts, ends, pos = [], [], []
    for (b, s) in groups:
        n = s // t
        for _ in range(b):
            for i in range(n):
                starts.append(int(i == 0))
                ends.append(int(i == n - 1))
                pos.append(i)
    return (np.asarray(starts, np.int32), np.asarray(ends, np.int32), np.asarray(pos, np.int32))


def _attn_tables(groups, tq, tk):
    qb, kb, first, last = [], [], [], []
    off = 0
    for (b, s) in groups:
        for bi in range(b):
            base_q = (off + bi * s) // tq
            base_k = (off + bi * s) // tk
            for qi in range(s // tq):
                nk = s // tk
                for ki in range(nk):
                    qb.append(base_q + qi)
                    kb.append(base_k + ki)
                    first.append(int(ki == 0))
                    last.append(int(ki == nk - 1))
        off += b * s
    return tuple(np.asarray(a, np.int32) for a in (qb, kb, first, last))


def _pair_specs(xa, xb, tm):
    d = xa.shape[1]
    na = xa.shape[0] // tm
    return (pl.BlockSpec((tm, d), lambda i: (jnp.minimum(i, na - 1), 0)),
            pl.BlockSpec((tm, d), lambda i: (jnp.maximum(i - na, 0), 0)), na)


def _pair_block(i, na, xa_ref, xb_ref):
    return jnp.where(i < na, xa_ref[...], xb_ref[...])


def _norm_proj_kernel(xa_ref, xb_ref, g_ref, w_ref, o_ref, *, na):
    x = _pair_block(pl.program_id(0), na, xa_ref, xb_ref)
    h = _rms_rows(x, g_ref[...]).astype(BF)
    o_ref[...] = _dot(h, w_ref[...])


def _norm_proj(xa, xb, gain, w, tm):
    n, d = xa.shape[0] + xb.shape[0], xa.shape[1]
    e = w.shape[1]
    spec_a, spec_b, na = _pair_specs(xa, xb, tm)
    return pl.pallas_call(
        functools.partial(_norm_proj_kernel, na=na),
        out_shape=jax.ShapeDtypeStruct((n, e), F32),
        grid=(n // tm,),
        in_specs=[
            spec_a, spec_b,
            pl.BlockSpec((1, d), lambda i: (0, 0)),
            pl.BlockSpec((d, e), lambda i: (0, 0)),
        ],
        out_specs=pl.BlockSpec((tm, e), lambda i: (i, 0)),
        compiler_params=_cparams(1),
        name="norm_proj",
    )(xa, xb, gain, w)


def _hgrn_direction(q_raw, f_raw, v, lb, state_ref, o_ref, reverse, t):
    n_chunks = t // HGRN_CHUNK
    q = _silu(q_raw)
    swing = (0.5 * (1.0 - lb)) * jnp.tanh(0.5 * f_raw)
    f = (0.5 * (1.0 + lb)) + swing
    k = (0.5 * (1.0 - lb)) - swing
    g = jnp.log(f)
    row = lax.broadcasted_iota(jnp.int32, (t, t), 0)
    col = lax.broadcasted_iota(jnp.int32, (t, t), 1)
    same_chunk = (row // HGRN_CHUNK) == (col // HGRN_CHUNK)
    causal = (col >= row) if reverse else (col <= row)
    keep = jnp.logical_and(same_chunk, causal)
    b = _dot01(jnp.where(keep, 1.0, 0.0).astype(BF), g)
    q_dec = (q * jnp.exp(b)).astype(BF)
    k_inv = (k * jnp.exp(-b)).astype(BF)
    v_bf = v.astype(BF)
    for h in range(MIX_W // HEAD_W):
        lanes = slice(h * HEAD_W, (h + 1) * HEAD_W)
        scores = jnp.where(keep, _dot_nt(q_dec[:, lanes], k_inv[:, lanes]), 0.0)
        o_ref[:, lanes] = _dot(scores.astype(BF), v_bf[:, lanes])
    order = range(n_chunks - 1, -1, -1) if reverse else range(n_chunks)
    for c in order:
        rows = slice(c * HGRN_CHUNK, (c + 1) * HGRN_CHUNK)
        edge = c * HGRN_CHUNK if reverse else (c + 1) * HGRN_CHUNK - 1
        b_edge = b[edge:edge + 1, :]
        k_end = (k[rows, :] * jnp.exp(b_edge - b[rows, :])).astype(BF)
        decay = jnp.exp(b_edge)
        for h in range(MIX_W // HEAD_W):
            lanes = slice(h * HEAD_W, (h + 1) * HEAD_W)
            s_t = state_ref[h]
            o_ref[rows, lanes] += _dot_nt(q_dec[rows, lanes], s_t.astype(BF))
            state_ref[h] = decay[:, lanes] * s_t + _dot_tn(v_bf[rows, lanes], k_end[:, lanes])


def _hgrn_sgu_kernel(starts_ref, ends_ref,
                     qf_ref, ff_ref, vf_ref, qb_ref, fb_ref, vb_ref, u_ref, v_ref,
                     lbp_ref, lng_ref, lnb_ref, ws_ref, bs_ref,
                     of_ref, ob_ref, sgu_ref, sf_ref, sb_ref, *, layer, t):
    i = pl.program_id(0)
    j = pl.num_programs(0) - 1 - i

    @pl.when(starts_ref[i] == 1)
    def _():
        sf_ref[...] = jnp.zeros_like(sf_ref)

    @pl.when(ends_ref[j] == 1)
    def _():
        sb_ref[...] = jnp.zeros_like(sb_ref)

    lbp = lbp_ref[...]
    e = jnp.exp(lbp - jnp.max(lbp, axis=0, keepdims=True))
    sm = e / jnp.sum(e, axis=0, keepdims=True)
    lb = jnp.sum(sm[:layer + 1, :], axis=0, keepdims=True)

    n_sub = t // HGRN_SUB
    for sb in range(n_sub):
        rows = slice(sb * HGRN_SUB, (sb + 1) * HGRN_SUB)
        _hgrn_direction(qf_ref[rows, :], ff_ref[rows, :], vf_ref[rows, :], lb, sf_ref, of_ref.at[rows, :],
                        False, HGRN_SUB)
    for sb in range(n_sub - 1, -1, -1):
        rows = slice(sb * HGRN_SUB, (sb + 1) * HGRN_SUB)
        _hgrn_direction(qb_ref[rows, :], fb_ref[rows, :], vb_ref[rows, :], lb, sb_ref, ob_ref.at[rows, :],
                        True, HGRN_SUB)

    u = jax.nn.gelu(u_ref[...])
    v = jax.nn.gelu(v_ref[...])
    for gi in range(MIX_W // HEAD_W):
        lanes = slice(gi * HEAD_W, (gi + 1) * HEAD_W)
        vg = v[:, lanes]
        mu = jnp.mean(vg, axis=-1, keepdims=True)
        var = jnp.mean(jnp.square(vg - mu), axis=-1, keepdims=True)
        vln = ((vg - mu) * lax.rsqrt(var + EPS) * lng_ref[:, lanes] + lnb_ref[:, lanes]).astype(BF)
        for c in range(t // SGU_CHUNK):
            rows = slice(c * SGU_CHUNK, (c + 1) * SGU_CHUNK)
            mixed = _dot(ws_ref[gi], vln[rows, :]) + bs_ref[gi]
            sgu_ref[rows, lanes] = u[rows, lanes] * mixed


def _hgrn_sgu(z, hgrn_lb, ln_g, ln_b, w_s, b_s, groups, layer, t):
    n = z.shape[0]
    nblk = n // t
    starts, ends, _ = _block_tables(groups, t)
    fwd = lambda c: pl.BlockSpec((t, MIX_W), lambda i, s, e, c=c: (i, c))
    bwd = lambda c: pl.BlockSpec((t, MIX_W), lambda i, s, e, c=c: (nblk - 1 - i, c))
    full = lambda shape: pl.BlockSpec(shape, lambda i, s, e: (0,) * len(shape))
    bs_b = jnp.broadcast_to(b_s[:, :, None], b_s.shape + (HEAD_W,)).astype(F32)
    out = jax.ShapeDtypeStruct((n, MIX_W), F32)
    return pl.pallas_call(
        functools.partial(_hgrn_sgu_kernel, layer=layer, t=t),
        out_shape=(out, out, out),
        grid_spec=pltpu.PrefetchScalarGridSpec(
            num_scalar_prefetch=2,
            grid=(nblk,),
            in_specs=[fwd(0), fwd(1), fwd(3), bwd(0), bwd(2), bwd(3), fwd(5), fwd(6),
                      full(hgrn_lb.shape), full((1, MIX_W)), full((1, MIX_W)),
                      full(w_s.shape), full(bs_b.shape)],
            out_specs=[fwd(0), bwd(0), fwd(0)],
            scratch_shapes=[pltpu.VMEM((MIX_W // HEAD_W, HEAD_W, HEAD_W), F32),
                            pltpu.VMEM((MIX_W // HEAD_W, HEAD_W, HEAD_W), F32)],
        ),
        compiler_params=_cparams(1),
        name="hgrn_sgu",
    )(jnp.asarray(starts), jnp.asarray(ends), z, z, z, z, z, z, z, z,
      hgrn_lb, ln_g.reshape(1, -1), ln_b.reshape(1, -1), w_s.astype(BF), bs_b)


def _mix0_ffn_kernel(xa_ref, xb_ref, of_ref, ob_ref, gate_ref, sgu_ref, gn_ref, wo_ref, g2_ref,
                     w1_ref, w3_ref, w2_ref, o_ref, hh_ref, *, f_chunk, na):
    o = of_ref[...] + ob_ref[...]
    gate = gate_ref[...]
    parts = []
    for h in range(MIX_W // HEAD_W):
        lanes = slice(h * HEAD_W, (h + 1) * HEAD_W)
        parts.append(_rms_rows(o[:, lanes], gn_ref[...]) * _silu(gate[:, lanes]))
    parts.append(sgu_ref[...])
    mix_in = jnp.concatenate(parts, axis=-1).astype(BF)
    x1 = _pair_block(pl.program_id(0), na, xa_ref, xb_ref) + _dot(mix_in, wo_ref[...])
    h2 = _rms_rows(x1, g2_ref[...]).astype(BF)
    d_ff = w1_ref.shape[1]
    for c in range(d_ff // f_chunk):
        cols = slice(c * f_chunk, (c + 1) * f_chunk)
        hh_ref[:, cols] = (_silu(_dot(h2, w1_ref[:, cols])) * _dot(h2, w3_ref[:, cols])).astype(BF)
    o_ref[...] = x1 + _dot(hh_ref[...], w2_ref[...])


def _resident(shape):
    return pl.BlockSpec(shape, lambda i: (0,) * len(shape), pipeline_mode=pl.Buffered(1))


def _mix0_ffn(xa, xb, z, o_f, o_b, sgu, gnorm, w_out, g2, w1, w3, w2, tm):
    n, d = xa.shape[0] + xb.shape[0], xa.shape[1]
    d_ff = w1.shape[1]
    tok = lambda w, c=0: pl.BlockSpec((tm, w), lambda i, c=c: (i, c))
    spec_a, spec_b, na = _pair_specs(xa, xb, tm)
    return pl.pallas_call(
        functools.partial(_mix0_ffn_kernel, f_chunk=d_ff // 2, na=na),
        out_shape=jax.ShapeDtypeStruct((n, d), F32),
        grid=(n // tm,),
        in_specs=[spec_a, spec_b, tok(MIX_W), tok(MIX_W), tok(MIX_W, 4), tok(MIX_W),
                  _resident((1, HEAD_W)), _resident(w_out.shape), _resident((1, d)),
                  _resident(w1.shape), _resident(w3.shape), _resident(w2.shape)],
        out_specs=tok(d),
        scratch_shapes=[pltpu.VMEM((tm, d_ff), BF)],
        compiler_params=_cparams(1),
        name="mix0_ffn",
    )(xa, xb, o_f, o_b, z, sgu, gnorm.reshape(1, -1), w_out, g2.reshape(1, -1), w1, w3, w2)


def _rope_tables(s_max):
    half = ROPE_DIMS // 2
    dim = np.arange(LANES) % D_QK
    lo = jnp.asarray(dim < half)[None, :]
    hi = jnp.asarray((dim >= half) & (dim < ROPE_DIMS))[None, :]
    inv_freq = ROPE_THETA ** (-jnp.asarray(2 * (dim % half), F32) / ROPE_DIMS)
    ang = jnp.arange(s_max, dtype=F32)[:, None] * inv_freq[None, :]
    cos, sin = jnp.cos(ang), jnp.sin(ang)
    return (jnp.where(lo | hi, cos, 1.0), jnp.where(lo, -sin, 0.0), jnp.where(hi, sin, 0.0))


def _qk_norm_rope(x, gain, seg, cos, s_lo, s_hi, scale):
    half = ROPE_DIMS // 2
    sq_hi, sq_lo = _split2(x * x)
    ms = _dot(jnp.concatenate([sq_hi, sq_lo], axis=1), seg)
    xn = x * lax.rsqrt(ms + EPS) * gain
    up = pltpu.roll(xn, LANES - half, axis=1)
    down = pltpu.roll(xn, half, axis=1)
    return ((xn * cos + up * s_lo + down * s_hi) * scale).astype(BF)


def _proj1_kernel(pos_ref, x_ref, g_ref, w_ref, qg_ref, kg_ref, cos_ref, slo_ref, shi_ref,
                  p_ref, q_ref, k_ref, vt_ref):
    h = _rms_rows(x_ref[...], g_ref[...]).astype(BF)
    vt_ref[...] = _dot(h, w_ref[:, 3 * MIX_W:]).T.astype(BF)
    r = lax.broadcasted_iota(jnp.int32, (2 * LANES, LANES), 0) % LANES
    c = lax.broadcasted_iota(jnp.int32, (2 * LANES, LANES), 1)
    seg = jnp.where((r // D_QK) == (c // D_QK), 1.0 / D_QK, 0.0).astype(BF)
    cos, s_lo, s_hi = cos_ref[...], slo_ref[...], shi_ref[...]
    for out_ref, gain_ref, col0, scale in ((q_ref, qg_ref, MIX_W, Q_SCALE), (k_ref, kg_ref, 2 * MIX_W, 1.0)):
        z = _dot(h, w_ref[:, col0:col0 + MIX_W])
        for hd in range(MIX_W // HEAD_W):
            lanes = slice(hd * HEAD_W, (hd + 1) * HEAD_W)
            out_ref[:, lanes] = _qk_norm_rope(z[:, lanes], gain_ref[...], seg, cos, s_lo, s_hi, scale)
    p_ref[...] = _dot(h, w_ref[:, :MIX_W])


def _proj1(x, gain, w, q_gain, k_gain, groups, tm):
    n, d = x.shape
    e = w.shape[1]
    _, _, pos = _block_tables(groups, tm)
    s_max = max(s for _, s in groups)
    cos, s_lo, s_hi = _rope_tables(s_max)
    tok = lambda width: pl.BlockSpec((tm, width), lambda i, p: (i, 0))
    full = lambda shape: pl.BlockSpec(shape, lambda i, p: (0,) * len(shape))
    rope = pl.BlockSpec((tm, LANES), lambda i, p: (p[i], 0))
    tile2 = lambda g: jnp.concatenate([g, g]).reshape(1, LANES)
    return pl.pallas_call(
        _proj1_kernel,
        out_shape=(jax.ShapeDtypeStruct((n, MIX_W), F32), jax.ShapeDtypeStruct((n, MIX_W), BF),
                   jax.ShapeDtypeStruct((n, MIX_W), BF), jax.ShapeDtypeStruct((MIX_W, n), BF)),
        grid_spec=pltpu.PrefetchScalarGridSpec(
            num_scalar_prefetch=1,
            grid=(n // tm,),
            in_specs=[tok(d), full((1, d)), full((d, e)), full((1, LANES)), full((1, LANES)),
                      rope, rope, rope],
            out_specs=[tok(MIX_W)] * 3 + [pl.BlockSpec((MIX_W, tm), lambda i, p: (0, i))],
        ),
        compiler_params=_cparams(1),
        name="proj1_qk_rope",
    )(jnp.asarray(pos), x, gain.reshape(1, -1), w, tile2(q_gain), tile2(k_gain), cos, s_lo, s_hi)


def _pool_block(is_start, is_end, prev_ref, cur_ref, next_ref, wc_ref, scale_ref, t):
    cur = cur_ref[...]
    prev = jnp.where(is_start, 0.0, prev_ref[...])
    nxt = jnp.where(is_end, 0.0, next_ref[...])
    ext = jnp.concatenate([prev, cur, nxt], axis=0)
    ext_hi, ext_lo = _split2(ext)
    sub = min(t, POOL_SUB)
    r = lax.broadcasted_iota(jnp.int32, (sub, sub + 2 * POOL_HALO), 0) + POOL_HALO
    c = lax.broadcasted_iota(jnp.int32, (sub, sub + 2 * POOL_HALO), 1)
    row = lax.broadcasted_iota(jnp.int32, (t, 1), 0)
    parts = []
    for gi, w in enumerate(POOL_WINDOWS):
        lanes = slice(gi * HEAD_W, (gi + 1) * HEAD_W)
        hw = w // 2
        band = jnp.where(jnp.logical_and(c >= r - hw, c < r + hw), 1.0, 0.0).astype(BF)
        strips = []
        for s0 in range(0, t, sub):
            rows = slice(s0, s0 + sub + 2 * POOL_HALO)
            both = _dot(band, jnp.concatenate([ext_hi[rows, lanes], ext_lo[rows, lanes]], axis=1))
            strips.append(both[:, :HEAD_W] + both[:, HEAD_W:])
        win = jnp.concatenate(strips, axis=0)
        lo_cut = jnp.where(is_start, jnp.maximum(hw - row, 0), 0)
        hi_cut = jnp.where(is_end, jnp.maximum(row + hw - t, 0), 0)
        cnt = (w - lo_cut - hi_cut).astype(F32)
        diff = win / cnt - cur[:, lanes]
        parts.append(_dot(diff.astype(BF), wc_ref[gi]) * scale_ref[:, lanes])
    return jnp.concatenate(parts, axis=-1)


def _diff_attn_kernel(qb_ref, kb_ref, first_ref, last_ref,
                      q_ref, k_ref, vt_ref, lq1_ref, lk1_ref, lq2_ref, lk2_ref, sub_ref, bound_ref,
                      o_ref, qzt_ref, v1_ref, m_ref, acc_ref, *, tq, tkc, n_chunks, unroll, lam_init):
    p = pl.program_id(1)

    @pl.when(first_ref[p] == 1)
    def _():
        qt = q_ref[...].astype(F32).T.astype(BF)
        row = lax.broadcasted_iota(jnp.int32, qt.shape, 0)
        qzt_ref[:, :tq] = jnp.where(row < D_QK, qt, jnp.zeros_like(qt))
        qzt_ref[:, tq:] = jnp.where(row >= D_QK, qt, jnp.zeros_like(qt))
        for slot in range(unroll):
            v1_ref[slot, HEAD_W:, :] = jnp.ones((ONES_ROWS, tkc), BF)
        m_ref[...] = jnp.full_like(m_ref, -jnp.inf)
        acc_ref[...] = jnp.zeros_like(acc_ref)

    def chunk(j, slot):
        off = pl.multiple_of(j * tkc, tkc)
        v1_ref[slot, :HEAD_W, :] = vt_ref[:, pl.ds(off, tkc)]
        st = _dot(k_ref[pl.ds(off, tkc), :], qzt_ref[...])
        m_old = m_ref[...]
        m_new = jnp.maximum(m_old, jnp.max(st, axis=0, keepdims=True))
        alpha = jnp.exp2(m_old - m_new)
        pt = jnp.exp2(st - m_new).astype(BF)
        acc_ref[...] = alpha * acc_ref[...] + _dot(v1_ref[slot], pt)
        m_ref[...] = m_new

    def chunks(i, carry):
        for slot in range(unroll):
            chunk(i * unroll + slot, slot)
        return carry

    bound = bound_ref[0, 0]
    fixed_shift_ok = bound <= MAX_FIXED_SHIFT

    def fixed_chunks(i, carry):
        total = None
        for slot in range(unroll):
            off = pl.multiple_of((i * unroll + slot) * tkc, tkc)
            v1_ref[slot, :HEAD_W, :] = vt_ref[:, pl.ds(off, tkc)]
            st = _dot(k_ref[pl.ds(off, tkc), :], qzt_ref[...])
            part = _dot(v1_ref[slot], jnp.exp2(st - bound).astype(BF))
            total = part if total is None else total + part
        acc_ref[...] += total
        return carry

    @pl.when(fixed_shift_ok)
    def _():
        lax.fori_loop(0, n_chunks // unroll, fixed_chunks, 0)

    @pl.when(jnp.logical_not(fixed_shift_ok))
    def _():
        lax.fori_loop(0, n_chunks // unroll, chunks, 0)

    @pl.when(last_ref[p] == 1)
    def _():
        lam = (jnp.exp(jnp.sum(lq1_ref[...] * lk1_ref[...], keepdims=True))
               - jnp.exp(jnp.sum(lq2_ref[...] * lk2_ref[...], keepdims=True)) + lam_init)
        acc = acc_ref[...]
        o = acc[:HEAD_W, :] / acc[HEAD_W:HEAD_W + 1, :]
        d = o[:, :tq] - lam * o[:, tq:]
        ms = jnp.mean(d * d, axis=0, keepdims=True)
        y = d * lax.rsqrt(ms + EPS) * sub_ref[...] * (1.0 - lam_init)
        o_ref[...] = y.T


def _diff_attn(q, k, vt, q_gain, k_gain, lq1, lk1, lq2, lk2, subln, groups, lam_init, tq, tkb, tkc):
    n = q.shape[0]
    bound = (1.01 * D_QK * Q_SCALE * jnp.max(jnp.abs(q_gain)) * jnp.max(jnp.abs(k_gain))).reshape(1, 1)
    qb, kb, first, last = _attn_tables(groups, tq, tkb)
    n_heads = MIX_W // HEAD_W
    vec = lambda a: a.reshape(1, -1)
    full = lambda shape: pl.BlockSpec(shape, lambda h, p, *_: (0, 0))
    sub_col = jnp.broadcast_to(subln[:, None], (HEAD_W, tq))
    n_chunks = tkb // tkc
    unroll = next(u for u in (8, 4, 2, 1) if n_chunks % u == 0)
    return pl.pallas_call(
        functools.partial(_diff_attn_kernel, tq=tq, tkc=tkc, n_chunks=n_chunks, unroll=unroll,
                          lam_init=lam_init),
        out_shape=jax.ShapeDtypeStruct((n, MIX_W), F32),
        grid_spec=pltpu.PrefetchScalarGridSpec(
            num_scalar_prefetch=4,
            grid=(n_heads, len(qb)),
            in_specs=[
                pl.BlockSpec((tq, HEAD_W), lambda h, p, qb, kb, f, l: (qb[p], h)),
                pl.BlockSpec((tkb, HEAD_W), lambda h, p, qb, kb, f, l: (kb[p], h)),
                pl.BlockSpec((HEAD_W, tkb), lambda h, p, qb, kb, f, l: (h, kb[p])),
                full((1, D_QK)), full((1, D_QK)), full((1, D_QK)), full((1, D_QK)), full((HEAD_W, tq)),
                pl.BlockSpec(memory_space=pltpu.SMEM),
            ],
            out_specs=pl.BlockSpec((tq, HEAD_W), lambda h, p, qb, kb, f, l: (qb[p], h)),
            scratch_shapes=[pltpu.VMEM((HEAD_W, 2 * tq), BF),
                            pltpu.VMEM((unroll, HEAD_W + ONES_ROWS, tkc), BF),
                            pltpu.VMEM((1, 2 * tq), F32), pltpu.VMEM((HEAD_W + ONES_ROWS, 2 * tq), F32)],
        ),
        compiler_params=_cparams(2),
        name="diff_attn",
    )(jnp.asarray(qb), jnp.asarray(kb), jnp.asarray(first), jnp.asarray(last),
      q, k, vt, vec(lq1), vec(lk1), vec(lq2), vec(lk2), sub_col, bound.astype(F32))


def _mix1_route_kernel(starts_ref, ends_ref, x_ref, pprev_ref, p_ref, pnext_ref, a_ref, wc_ref, ps_ref,
                       wo_ref, g_ref, wr_ref,
                       x3_ref, h_ref, eidx_ref, gate_ref, rank_ref, cnt_ref, base_ref, *, tm):
    i = pl.program_id(0)

    @pl.when(i == 0)
    def _():
        base_ref[...] = jnp.zeros_like(base_ref)

    c_out = _pool_block(starts_ref[i] == 1, ends_ref[i] == 1, pprev_ref, p_ref, pnext_ref, wc_ref, ps_ref, tm)
    mix_in = jnp.concatenate([c_out, a_ref[...]], axis=-1).astype(BF)
    x3 = x_ref[...] + _dot(mix_in, wo_ref[...])
    x3_ref[...] = x3
    h = _rms_rows(x3, g_ref[...])
    h_ref[...] = h
    wr = wr_ref[...]
    wr_hi = wr.astype(BF).astype(F32)
    wr_parts = jnp.concatenate([wr_hi, wr - wr_hi], axis=0).astype(BF)
    h_hi, h_lo = _split2(h)
    by_hi = _dot_nt(wr_parts, h_hi)
    logits = by_hi[:N_EXPERTS] + by_hi[N_EXPERTS:] + _dot_nt(wr_hi.astype(BF), h_lo)
    eid = lax.broadcasted_iota(jnp.int32, logits.shape, 0)
    m1 = jnp.max(logits, axis=0, keepdims=True)
    i1 = jnp.min(jnp.where(logits == m1, eid, N_EXPERTS), axis=0, keepdims=True)
    rest = jnp.where(eid == i1, -jnp.inf, logits)
    m2 = jnp.max(rest, axis=0, keepdims=True)
    i2 = jnp.min(jnp.where(rest == m2, eid, N_EXPERTS), axis=0, keepdims=True)
    e2 = jnp.exp(m2 - m1)
    g1 = 1.0 / (1.0 + e2)
    g2 = e2 / (1.0 + e2)
    sel1 = eid == i1
    sel2 = eid == i2
    onehot = jnp.where(jnp.logical_or(sel1, sel2), 1.0, 0.0)
    r = lax.broadcasted_iota(jnp.int32, (tm, tm), 0)
    c = lax.broadcasted_iota(jnp.int32, (tm, tm), 1)
    incl = _dot(onehot.astype(BF), jnp.where(r <= c, 1.0, 0.0).astype(BF))
    before = base_ref[:, :1] + incl - onehot
    eidx_ref[0:1, :] = i1
    eidx_ref[1:2, :] = i2
    gate_ref[0:1, :] = g1
    gate_ref[1:2, :] = g2
    rank_ref[0:1, :] = jnp.sum(jnp.where(sel1, before, 0.0), axis=0, keepdims=True).astype(jnp.int32)
    rank_ref[1:2, :] = jnp.sum(jnp.where(sel2, before, 0.0), axis=0, keepdims=True).astype(jnp.int32)
    base_ref[...] = base_ref[...] + jnp.sum(onehot, axis=1, keepdims=True)
    cnt_ref[...] = base_ref[...].astype(jnp.int32)


def _mix1_route(x, pz, att, pool_w, pool_scale, w_out, g2, w_router, groups, tm):
    n, d = x.shape
    starts, ends, _ = _block_tables(groups, tm)
    per = tm // POOL_HALO
    n_halo = n // POOL_HALO
    tok = lambda width: pl.BlockSpec((tm, width), lambda i, s, e: (i, 0))
    lane_tok = pl.BlockSpec((2, tm), lambda i, s, e: (0, i))
    full = lambda shape: pl.BlockSpec(shape, lambda i, s, e: (0,) * len(shape), pipeline_mode=pl.Buffered(1))
    return pl.pallas_call(
        functools.partial(_mix1_route_kernel, tm=tm),
        out_shape=(jax.ShapeDtypeStruct((n, d), F32), jax.ShapeDtypeStruct((n, d), F32),
                   jax.ShapeDtypeStruct((2, n), jnp.int32), jax.ShapeDtypeStruct((2, n), F32),
                   jax.ShapeDtypeStruct((2, n), jnp.int32),
                   jax.ShapeDtypeStruct((N_EXPERTS, LANES), jnp.int32)),
        grid_spec=pltpu.PrefetchScalarGridSpec(
            num_scalar_prefetch=2,
            grid=(n // tm,),
            in_specs=[tok(d),
                      pl.BlockSpec((POOL_HALO, MIX_W), lambda i, s, e: (jnp.maximum(i * per - 1, 0), 0)),
                      tok(MIX_W),
                      pl.BlockSpec((POOL_HALO, MIX_W),
                                   lambda i, s, e: (jnp.minimum((i + 1) * per, n_halo - 1), 0)),
                      tok(MIX_W), full(pool_w.shape), full((1, MIX_W)),
                      full(w_out.shape), full((1, d)), full((N_EXPERTS, d))],
            out_specs=[tok(d), tok(d), lane_tok, lane_tok, lane_tok,
                       pl.BlockSpec((N_EXPERTS, LANES), lambda i, s, e: (0, 0))],
            scratch_shapes=[pltpu.VMEM((N_EXPERTS, LANES), F32)],
        ),
        compiler_params=_cparams(1),
        name="mix1_route",
    )(jnp.asarray(starts), jnp.asarray(ends), x, pz, pz, pz, att, pool_w.astype(BF), pool_scale.reshape(1, -1),
      w_out, g2.reshape(1, -1), w_router.T)


def _experts_kernel(be_ref, nused_ref, row0_ref, rown_ref, rowp_ref, h_hbm, w1_hbm, w3_hbm, w2_hbm, y_hbm,
                    xbuf, ybuf, w1_v, w3_v, w2_v, stage_in, stage_out, gsem, ssem, wsem, *, mb, f_chunk, n_rows):
    b = pl.program_id(0)
    cur = b % 2
    oth = 1 - cur
    e = be_ref[b]
    prev_e = be_ref[jnp.maximum(b - 1, 0)]
    d_e = w1_v.shape[1]
    n_chunks = d_e // f_chunk

    n_tok = n_rows // 2

    def gather_row(row_ref, r, slot):
        v = row_ref[0, 0, r]
        tok = v - jnp.where(v >= n_rows, n_rows, jnp.where(v >= n_tok, n_tok, 0))
        pltpu.make_async_copy(h_hbm.at[pl.ds(tok, 1), :],
                              xbuf.at[slot, pl.ds(r, 1), :], gsem.at[slot]).start()

    def scatter_row(r, slot):
        pltpu.make_async_copy(ybuf.at[slot, pl.ds(r, 1), :],
                              y_hbm.at[pl.ds(rowp_ref[0, 0, r], 1), :], ssem.at[slot]).start()

    def load_weights():
        pieces = []
        for c in range(n_chunks):
            cols = pl.ds(c * f_chunk, f_chunk)
            pieces.append((w1_hbm.at[e, :, cols], stage_in, w1_v.at[:, cols]))
            pieces.append((w3_hbm.at[e, :, cols], stage_in, w3_v.at[:, cols]))
            pieces.append((w2_hbm.at[e, cols, :], stage_out, w2_v.at[cols, :]))

        def copy(i):
            src, stage, _ = pieces[i]
            return pltpu.make_async_copy(src, stage.at[i % 2], wsem.at[i % 2])

        copy(0).start()
        for i, (_, stage, dst) in enumerate(pieces):
            if i + 1 < len(pieces):
                copy(i + 1).start()
            copy(i).wait()
            dst[...] = stage[i % 2].astype(BF)

    @pl.when(b == 0)
    def _():
        ybuf[...] = jnp.zeros_like(ybuf)
        zero_tail = pltpu.make_async_copy(ybuf.at[0], y_hbm.at[pl.ds(n_rows, mb), :], wsem.at[0])
        zero_tail.start()
        zero_tail.wait()
        lax.fori_loop(0, mb, lambda r, c: (gather_row(row0_ref, r, 0), c)[1], 0)

    live = b <= nused_ref[0]

    @pl.when(live)
    def _():
        pltpu.make_async_copy(h_hbm.at[pl.ds(0, mb), :], xbuf.at[cur], gsem.at[cur]).wait()

    @pl.when(b < nused_ref[0])
    def _():
        new_expert = jnp.logical_or(b == 0, e != prev_e)

        @pl.when(new_expert)
        def _():
            load_weights()

        x = xbuf[cur].astype(BF)
        acc = jnp.zeros((mb, D_MODEL), F32)
        for c in range(n_chunks):
            cols = slice(c * f_chunk, (c + 1) * f_chunk)
            hh = _silu(_dot(x, w1_v[:, cols])) * _dot(x, w3_v[:, cols])
            acc = acc + _dot(hh.astype(BF), w2_v[cols, :])
            n_dma = n_chunks - 1
            for r in range(min(c, n_dma) * mb // n_dma, min(c + 1, n_dma) * mb // n_dma):
                gather_row(rown_ref, r, oth)
                scatter_row(r, oth)
        ybuf[cur] = acc

    @pl.when(b == nused_ref[0])
    def _():
        lax.fori_loop(0, mb, lambda r, c: (scatter_row(r, oth), c)[1], 0)

    @pl.when(live)
    def _():
        pltpu.make_async_copy(ybuf.at[oth], y_hbm.at[pl.ds(0, mb), :], ssem.at[oth]).wait()


def _experts(h, slot_row, block_e, n_used, w1, w3, w2, mb, n_rows):
    n_steps = block_e.shape[0]
    d = h.shape[1]
    d_e = w1.shape[2]
    f_chunk = 512
    any_spec = pl.BlockSpec(memory_space=pl.ANY)
    smem_blk = lambda fn: pl.BlockSpec((1, 1, mb), fn, memory_space=pltpu.SMEM)
    return pl.pallas_call(
        functools.partial(_experts_kernel, mb=mb, f_chunk=f_chunk, n_rows=n_rows),
        out_shape=jax.ShapeDtypeStruct((n_rows + mb, d), F32),
        grid_spec=pltpu.PrefetchScalarGridSpec(
            num_scalar_prefetch=2,
            grid=(n_steps,),
            in_specs=[smem_blk(lambda b, be, nu: (0, 0, 0)),
                      smem_blk(lambda b, be, nu: (jnp.minimum(b + 1, n_steps - 1), 0, 0)),
                      smem_blk(lambda b, be, nu: (jnp.maximum(b - 1, 0), 0, 0)),
                      any_spec, any_spec, any_spec, any_spec],
            out_specs=any_spec,
            scratch_shapes=[pltpu.VMEM((2, mb, d), F32), pltpu.VMEM((2, mb, d), F32),
                            pltpu.VMEM((d, d_e), BF), pltpu.VMEM((d, d_e), BF), pltpu.VMEM((d_e, d), BF),
                            pltpu.VMEM((2, d, f_chunk), F32), pltpu.VMEM((2, f_chunk, d), F32),
                            pltpu.SemaphoreType.DMA((2,)), pltpu.SemaphoreType.DMA((2,)),
                            pltpu.SemaphoreType.DMA((2,))],
        ),
        compiler_params=_cparams(1),
        name="experts",
    )(block_e, n_used, slot_row, slot_row, slot_row, h, w1, w3, w2)


def _combine_kernel(x_ref, gate_ref, y0_ref, y1_ref, oa_ref, ob_ref, *, tm, na):
    i = pl.program_id(0)
    g = jnp.concatenate([gate_ref[...], jnp.zeros((6, tm), F32)], axis=0).T
    out = x_ref[...] + g[:, 0:1] * y0_ref[...] + g[:, 1:2] * y1_ref[...]

    @pl.when(i < na)
    def _():
        oa_ref[...] = out

    @pl.when(i >= na)
    def _():
        ob_ref[...] = out


def _combine(x, gates, y, n_first, tm):
    n, d = x.shape
    nblk = n // tm
    na = n_first // tm
    return pl.pallas_call(
        functools.partial(_combine_kernel, tm=tm, na=na),
        out_shape=(jax.ShapeDtypeStruct((n_first, d), F32), jax.ShapeDtypeStruct((n - n_first, d), F32)),
        grid=(nblk,),
        in_specs=[pl.BlockSpec((tm, d), lambda i: (i, 0)),
                  pl.BlockSpec((2, tm), lambda i: (0, i)),
                  pl.BlockSpec((tm, d), lambda i: (i, 0)),
                  pl.BlockSpec((tm, d), lambda i: (i + nblk, 0))],
        out_specs=[pl.BlockSpec((tm, d), lambda i: (jnp.minimum(i, na - 1), 0)),
                   pl.BlockSpec((tm, d), lambda i: (jnp.maximum(i - na, 0), 0))],
        compiler_params=_cparams(1),
        name="combine",
    )(x, gates, y, y)


def _moe(x3, h, eidx, gates, rank, counts, w1, w3, w2, n_first, mb, tm):
    n = x3.shape[0]
    n_blocks = -(-2 * n // mb) + N_EXPERTS
    cnt = counts[:, 0]
    padded = ((cnt + mb - 1) // mb) * mb
    pends = jnp.cumsum(padded)
    pstarts = pends - padded
    start_of = sum(jnp.where(eidx == e, pstarts[e], 0) for e in range(N_EXPERTS))
    dest = (start_of + rank).reshape(-1)
    rows = jnp.arange(2 * n, dtype=jnp.int32)
    n_slots = (n_blocks + 1) * mb
    spare = 2 * n + jnp.arange(n_slots, dtype=jnp.int32) % mb
    slot_row = spare.at[dest].set(rows, unique_indices=True)
    blk_start = jnp.arange(n_blocks + 1, dtype=jnp.int32) * mb
    block_e = jnp.minimum(jnp.searchsorted(pends, blk_start, side="right"), N_EXPERTS - 1).astype(jnp.int32)
    n_used = (pends[-1] // mb).astype(jnp.int32).reshape(1)
    y = _experts(h, slot_row.reshape(n_blocks + 1, 1, mb), block_e, n_used, w1, w3, w2, mb, 2 * n)
    return _combine(x3, gates, y, n_first, tm)


def _trunk(xa, xb, groups, p):
    bf = lambda a: a.astype(BF)
    z = _norm_proj(xa, xb, p["e_norm1"][0].reshape(1, -1), bf(p["e_w_in"][0]), tm=512)
    o_f, o_b, sgu = _hgrn_sgu(z, p["hgrn_lb"], p["e_sgu_ln_g"][0], p["e_sgu_ln_b"][0],
                              p["e_sgu_w"][0], p["e_sgu_b"][0], groups, layer=0, t=512)
    x = _mix0_ffn(xa, xb, z, o_f, o_b, sgu, p["e_hgrn_gnorm"][0], bf(p["e_w_out"][0]), p["e_norm2"][0],
                  bf(p["e_ffn_w1"][0]), bf(p["e_ffn_w3"][0]), bf(p["e_ffn_w2"][0]), tm=512)
    layer = 1
    lam_init = 0.8 - 0.6 * math.exp(-0.3 * layer)
    pz, q, k, vt = _proj1(x, p["o_norm1"][0], bf(p["o_w_in"][0]), p["o_q_norm"][0], p["o_k_norm"][0],
                          groups, tm=512)
    tkb = min(4096, math.gcd(*[s for _, s in groups]))
    att = _diff_attn(q, k, vt, p["o_q_norm"][0], p["o_k_norm"][0], p["o_lambda_q1"][0], p["o_lambda_k1"][0], p["o_lambda_q2"][0],
                     p["o_lambda_k2"][0], p["o_subln"][0], groups, lam_init, tq=min(1024, tkb), tkb=tkb,
                     tkc=min(512, tkb))
    x3, h, eidx, gates, rank, counts = _mix1_route(x, pz, att, p["o_pool_w"][0], p["o_pool_scale"][0],
                                                   bf(p["o_w_out"][0]), p["o_norm2"][0], p["o_router"][0],
                                                   groups, tm=512)
    return _moe(x3, h, eidx, gates, rank, counts, p["o_moe_w1"][0], p["o_moe_w3"][0], p["o_moe_w2"][0],
                n_first=xa.shape[0], mb=512, tm=512)


def kernel(x_prompt, x_sample, hgrn_lb, e_norm1, e_w_in, e_hgrn_gnorm, e_sgu_ln_g, e_sgu_ln_b, e_sgu_w, e_sgu_b, e_w_out, e_norm2, e_ffn_w1, e_ffn_w3, e_ffn_w2, o_norm1, o_w_in, o_pool_w, o_pool_scale, o_q_norm, o_k_norm, o_lambda_q1, o_lambda_k1, o_lambda_q2, o_lambda_k2, o_subln, o_w_out, o_norm2, o_router, o_moe_w1, o_moe_w3, o_moe_w2):
    params = dict(
        hgrn_lb=hgrn_lb, e_norm1=e_norm1, e_w_in=e_w_in, e_hgrn_gnorm=e_hgrn_gnorm,
        e_sgu_ln_g=e_sgu_ln_g, e_sgu_ln_b=e_sgu_ln_b, e_sgu_w=e_sgu_w, e_sgu_b=e_sgu_b,
        e_w_out=e_w_out, e_norm2=e_norm2, e_ffn_w1=e_ffn_w1, e_ffn_w3=e_ffn_w3, e_ffn_w2=e_ffn_w2,
        o_norm1=o_norm1, o_w_in=o_w_in, o_pool_w=o_pool_w, o_pool_scale=o_pool_scale,
        o_q_norm=o_q_norm, o_k_norm=o_k_norm, o_lambda_q1=o_lambda_q1, o_lambda_k1=o_lambda_k1,
        o_lambda_q2=o_lambda_q2, o_lambda_k2=o_lambda_k2, o_subln=o_subln, o_w_out=o_w_out,
        o_norm2=o_norm2, o_router=o_router, o_moe_w1=o_moe_w1, o_moe_w3=o_moe_w3, o_moe_w2=o_moe_w2,
    )
    d = x_prompt.shape[-1]
    groups = (x_prompt.shape[:2], x_sample.shape[:2])
    y_p, y_s = _trunk(x_prompt.reshape(-1, d), x_sample.reshape(-1, d), groups, params)
    return (y_p.reshape(x_prompt.shape), y_s.reshape(x_sample.shape))
```

```python
import functools
import math

import numpy as np
import jax
import jax.numpy as jnp
from jax import lax
from jax.experimental import pallas as pl
from jax.experimental.pallas import tpu as pltpu

F32 = jnp.float32
BF = jnp.bfloat16

D_MODEL = 1024
EPS = 1e-6
LANES = 128
HEAD_W = 128
MIX_W = 512
HGRN_CHUNK = 64
HGRN_SUB = 256
SGU_CHUNK = 128
POOL_WINDOWS = (2, 4, 8, 16)
POOL_HALO = 16
POOL_SUB = 128
D_QK = 64
Q_SCALE = D_QK ** -0.5 * math.log2(math.e)
ONES_ROWS = 16
MAX_FIXED_SHIFT = 40.0
ROPE_DIMS = 16
ROPE_THETA = 500000.0
N_EXPERTS = 8
VMEM_LIMIT = 56 * 1024 * 1024


def _cparams(n_axes, vmem=VMEM_LIMIT):
    return pltpu.CompilerParams(dimension_semantics=("arbitrary",) * n_axes, vmem_limit_bytes=vmem)


def _dot(a, b):
    return jnp.dot(a, b, preferred_element_type=F32)


def _dot_nt(a, b):
    return lax.dot_general(a, b, (((1,), (1,)), ((), ())), preferred_element_type=F32)


def _dot_tn(a, b):
    return lax.dot_general(a, b, (((0,), (0,)), ((), ())), preferred_element_type=F32)


def _split2(x):
    hi = x.astype(BF)
    lo = (x - hi.astype(F32)).astype(BF)
    return hi, lo


def _dot01(a01, x):
    hi, lo = _split2(x)
    return _dot(a01, hi) + _dot(a01, lo)


def _rms_rows(x, gain):
    ms = jnp.mean(x * x, axis=-1, keepdims=True)
    return x * lax.rsqrt(ms + EPS) * gain


def _silu(x):
    h = 0.5 * x
    return h * jnp.tanh(h) + h


def _block_tables(groups, t):
    starts, ends, pos = [], [], []
    for (b, s) in groups:
        n = s // t
        for _ in range(b):
            for i in range(n):
                starts.append(int(i == 0))
                ends.append(int(i == n - 1))
                pos.append(i)
    return (np.asarray(starts, np.int32), np.asarray(ends, np.int32), np.asarray(pos, np.int32))


def _attn_tables(groups, tq, tk):
    qb, kb, first, last = [], [], [], []
    off = 0
    for (b, s) in groups:
        for bi in range(b):
            base_q = (off + bi * s) // tq
            base_k = (off + bi * s) // tk
            for qi in range(s // tq):
                nk = s // tk
                for ki in range(nk):
                    qb.append(base_q + qi)
                    kb.append(base_k + ki)
                    first.append(int(ki == 0))
                    last.append(int(ki == nk - 1))
        off += b * s
    return tuple(np.asarray(a, np.int32) for a in (qb, kb, first, last))


def _pair_specs(xa, xb, tm):
    d = xa.shape[1]
    na = xa.shape[0] // tm
    return (pl.BlockSpec((tm, d), lambda i: (jnp.minimum(i, na - 1), 0)),
            pl.BlockSpec((tm, d), lambda i: (jnp.maximum(i - na, 0), 0)), na)


def _pair_block(i, na, xa_ref, xb_ref):
    return jnp.where(i < na, xa_ref[...], xb_ref[...])


def _norm_proj_kernel(xa_ref, xb_ref, g_ref, w_ref, o_ref, *, na):
    x = _pair_block(pl.program_id(0), na, xa_ref, xb_ref)
    h = _rms_rows(x, g_ref[...]).astype(BF)
    o_ref[...] = _dot(h, w_ref[...])


def _norm_proj(xa, xb, gain, w, tm):
    n, d = xa.shape[0] + xb.shape[0], xa.shape[1]
    e = w.shape[1]
    spec_a, spec_b, na = _pair_specs(xa, xb, tm)
    return pl.pallas_call(
        functools.partial(_norm_proj_kernel, na=na),
        out_shape=jax.ShapeDtypeStruct((n, e), F32),
        grid=(n // tm,),
        in_specs=[
            spec_a, spec_b,
            pl.BlockSpec((1, d), lambda i: (0, 0)),
            pl.BlockSpec((d, e), lambda i: (0, 0)),
        ],
        out_specs=pl.BlockSpec((tm, e), lambda i: (i, 0)),
        compiler_params=_cparams(1),
        name="norm_proj",
    )(xa, xb, gain, w)


def _hgrn_direction(q_raw, f_raw, v, lb, state_ref, o_ref, reverse, t):
    n_chunks = t // HGRN_CHUNK
    q = _silu(q_raw)
    swing = (0.5 * (1.0 - lb)) * jnp.tanh(0.5 * f_raw)
    f = (0.5 * (1.0 + lb)) + swing
    k = (0.5 * (1.0 - lb)) - swing
    g = jnp.log(f)
    row = lax.broadcasted_iota(jnp.int32, (t, t), 0)
    col = lax.broadcasted_iota(jnp.int32, (t, t), 1)
    same_chunk = (row // HGRN_CHUNK) == (col // HGRN_CHUNK)
    causal = (col >= row) if reverse else (col <= row)
    keep = jnp.logical_and(same_chunk, causal)
    b = _dot01(jnp.where(keep, 1.0, 0.0).astype(BF), g)
    q_dec = (q * jnp.exp(b)).astype(BF)
    k_inv = (k * jnp.exp(-b)).astype(BF)
    v_bf = v.astype(BF)
    for h in range(MIX_W // HEAD_W):
        lanes = slice(h * HEAD_W, (h + 1) * HEAD_W)
        scores = jnp.where(keep, _dot_nt(q_dec[:, lanes], k_inv[:, lanes]), 0.0)
        o_ref[:, lanes] = _dot(scores.astype(BF), v_bf[:, lanes])
    order = range(n_chunks - 1, -1, -1) if reverse else range(n_chunks)
    for c in order:
        rows = slice(c * HGRN_CHUNK, (c + 1) * HGRN_CHUNK)
        edge = c * HGRN_CHUNK if reverse else (c + 1) * HGRN_CHUNK - 1
        b_edge = b[edge:edge + 1, :]
        k_end = (k[rows, :] * jnp.exp(b_edge - b[rows, :])).astype(BF)
        decay = jnp.exp(b_edge)
        for h in range(MIX_W // HEAD_W):
            lanes = slice(h * HEAD_W, (h + 1) * HEAD_W)
            s_t = state_ref[h]
            o_ref[rows, lanes] += _dot_nt(q_dec[rows, lanes], s_t.astype(BF))
            state_ref[h] = decay[:, lanes] * s_t + _dot_tn(v_bf[rows, lanes], k_end[:, lanes])


def _hgrn_sgu_kernel(starts_ref, ends_ref,
                     qf_ref, ff_ref, vf_ref, qb_ref, fb_ref, vb_ref, u_ref, v_ref,
                     lbp_ref, lng_ref, lnb_ref, ws_ref, bs_ref,
                     of_ref, ob_ref, sgu_ref, sf_ref, sb_ref, *, layer, t):
    i = pl.program_id(0)
    j = pl.num_programs(0) - 1 - i

    @pl.when(starts_ref[i] == 1)
    def _():
        sf_ref[...] = jnp.zeros_like(sf_ref)

    @pl.when(ends_ref[j] == 1)
    def _():
        sb_ref[...] = jnp.zeros_like(sb_ref)

    lbp = lbp_ref[...]
    e = jnp.exp(lbp - jnp.max(lbp, axis=0, keepdims=True))
    sm = e / jnp.sum(e, axis=0, keepdims=True)
    lb = jnp.sum(sm[:layer + 1, :], axis=0, keepdims=True)

    n_sub = t // HGRN_SUB
    for sb in range(n_sub):
        rows = slice(sb * HGRN_SUB, (sb + 1) * HGRN_SUB)
        _hgrn_direction(qf_ref[rows, :], ff_ref[rows, :], vf_ref[rows, :], lb, sf_ref, of_ref.at[rows, :],
                        False, HGRN_SUB)
    for sb in range(n_sub - 1, -1, -1):
        rows = slice(sb * HGRN_SUB, (sb + 1) * HGRN_SUB)
        _hgrn_direction(qb_ref[rows, :], fb_ref[rows, :], vb_ref[rows, :], lb, sb_ref, ob_ref.at[rows, :],
                        True, HGRN_SUB)

    u = jax.nn.gelu(u_ref[...])
    v = jax.nn.gelu(v_ref[...])
    for gi in range(MIX_W // HEAD_W):
        lanes = slice(gi * HEAD_W, (gi + 1) * HEAD_W)
        vg = v[:, lanes]
        mu = jnp.mean(vg, axis=-1, keepdims=True)
        var = jnp.mean(jnp.square(vg - mu), axis=-1, keepdims=True)
        vln = ((vg - mu) * lax.rsqrt(var + EPS) * lng_ref[:, lanes] + lnb_ref[:, lanes]).astype(BF)
        for c in range(t // SGU_CHUNK):
            rows = slice(c * SGU_CHUNK, (c + 1) * SGU_CHUNK)
            mixed = _dot(ws_ref[gi], vln[rows, :]) + bs_ref[gi]
            sgu_ref[rows, lanes] = u[rows, lanes] * mixed


def _hgrn_sgu(z, hgrn_lb, ln_g, ln_b, w_s, b_s, groups, layer, t):
    n = z.shape[0]
    nblk = n // t
    starts, ends, _ = _block_tables(groups, t)
    fwd = lambda c: pl.BlockSpec((t, MIX_W), lambda i, s, e, c=c: (i, c))
    bwd = lambda c: pl.BlockSpec((t, MIX_W), lambda i, s, e, c=c: (nblk - 1 - i, c))
    full = lambda shape: pl.BlockSpec(shape, lambda i, s, e: (0,) * len(shape))
    bs_b = jnp.broadcast_to(b_s[:, :, None], b_s.shape + (HEAD_W,)).astype(F32)
    out = jax.ShapeDtypeStruct((n, MIX_W), F32)
    return pl.pallas_call(
        functools.partial(_hgrn_sgu_kernel, layer=layer, t=t),
        out_shape=(out, out, out),
        grid_spec=pltpu.PrefetchScalarGridSpec(
            num_scalar_prefetch=2,
            grid=(nblk,),
            in_specs=[fwd(0), fwd(1), fwd(3), bwd(0), bwd(2), bwd(3), fwd(5), fwd(6),
                      full(hgrn_lb.shape), full((1, MIX_W)), full((1, MIX_W)),
                      full(w_s.shape), full(bs_b.shape)],
            out_specs=[fwd(0), bwd(0), fwd(0)],
            scratch_shapes=[pltpu.VMEM((MIX_W // HEAD_W, HEAD_W, HEAD_W), F32),
                            pltpu.VMEM((MIX_W // HEAD_W, HEAD_W, HEAD_W), F32)],
        ),
        compiler_params=_cparams(1),
        name="hgrn_sgu",
    )(jnp.asarray(starts), jnp.asarray(ends), z, z, z, z, z, z, z, z,
      hgrn_lb, ln_g.reshape(1, -1), ln_b.reshape(1, -1), w_s.astype(BF), bs_b)


def _mix0_ffn_kernel(xa_ref, xb_ref, of_ref, ob_ref, gate_ref, sgu_ref, gn_ref, wo_ref, g2_ref,
                     w1_ref, w3_ref, w2_ref, o_ref, hh_ref, *, f_chunk, na):
    o = of_ref[...] + ob_ref[...]
    gate = gate_ref[...]
    parts = []
    for h in range(MIX_W // HEAD_W):
        lanes = slice(h * HEAD_W, (h + 1) * HEAD_W)
        parts.append(_rms_rows(o[:, lanes], gn_ref[...]) * _silu(gate[:, lanes]))
    parts.append(sgu_ref[...])
    mix_in = jnp.concatenate(parts, axis=-1).astype(BF)
    x1 = _pair_block(pl.program_id(0), na, xa_ref, xb_ref) + _dot(mix_in, wo_ref[...])
    h2 = _rms_rows(x1, g2_ref[...]).astype(BF)
    d_ff = w1_ref.shape[1]
    for c in range(d_ff // f_chunk):
        cols = slice(c * f_chunk, (c + 1) * f_chunk)
        hh_ref[:, cols] = (_silu(_dot(h2, w1_ref[:, cols])) * _dot(h2, w3_ref[:, cols])).astype(BF)
    o_ref[...] = x1 + _dot(hh_ref[...], w2_ref[...])


def _resident(shape):
    return pl.BlockSpec(shape, lambda i: (0,) * len(shape), pipeline_mode=pl.Buffered(1))


def _mix0_ffn(xa, xb, z, o_f, o_b, sgu, gnorm, w_out, g2, w1, w3, w2, tm):
    n, d = xa.shape[0] + xb.shape[0], xa.shape[1]
    d_ff = w1.shape[1]
    tok = lambda w, c=0: pl.BlockSpec((tm, w), lambda i, c=c: (i, c))
    spec_a, spec_b, na = _pair_specs(xa, xb, tm)
    return pl.pallas_call(
        functools.partial(_mix0_ffn_kernel, f_chunk=d_ff // 2, na=na),
        out_shape=jax.ShapeDtypeStruct((n, d), F32),
        grid=(n // tm,),
        in_specs=[spec_a, spec_b, tok(MIX_W), tok(MIX_W), tok(MIX_W, 4), tok(MIX_W),
                  _resident((1, HEAD_W)), _resident(w_out.shape), _resident((1, d)),
                  _resident(w1.shape), _resident(w3.shape), _resident(w2.shape)],
        out_specs=tok(d),
        scratch_shapes=[pltpu.VMEM((tm, d_ff), BF)],
        compiler_params=_cparams(1),
        name="mix0_ffn",
    )(xa, xb, o_f, o_b, z, sgu, gnorm.reshape(1, -1), w_out, g2.reshape(1, -1), w1, w3, w2)


def _rope_tables(s_max):
    half = ROPE_DIMS // 2
    dim = np.arange(LANES) % D_QK
    lo = jnp.asarray(dim < half)[None, :]
    hi = jnp.asarray((dim >= half) & (dim < ROPE_DIMS))[None, :]
    inv_freq = ROPE_THETA ** (-jnp.asarray(2 * (dim % half), F32) / ROPE_DIMS)
    ang = jnp.arange(s_max, dtype=F32)[:, None] * inv_freq[None, :]
    cos, sin = jnp.cos(ang), jnp.sin(ang)
    return (jnp.where(lo | hi, cos, 1.0), jnp.where(lo, -sin, 0.0), jnp.where(hi, sin, 0.0))


def _qk_norm_rope(x, gain, seg, cos, s_lo, s_hi, scale):
    half = ROPE_DIMS // 2
    sq_hi, sq_lo = _split2(x * x)
    ms = _dot(jnp.concatenate([sq_hi, sq_lo], axis=1), seg)
    xn = x * lax.rsqrt(ms + EPS) * gain
    up = pltpu.roll(xn, LANES - half, axis=1)
    down = pltpu.roll(xn, half, axis=1)
    return ((xn * cos + up * s_lo + down * s_hi) * scale).astype(BF)


def _proj1_kernel(pos_ref, x_ref, g_ref, w_ref, qg_ref, kg_ref, cos_ref, slo_ref, shi_ref,
                  p_ref, q_ref, k_ref, vt_ref):
    h = _rms_rows(x_ref[...], g_ref[...]).astype(BF)
    vt_ref[...] = _dot(h, w_ref[:, 3 * MIX_W:]).T.astype(BF)
    r = lax.broadcasted_iota(jnp.int32, (2 * LANES, LANES), 0) % LANES
    c = lax.broadcasted_iota(jnp.int32, (2 * LANES, LANES), 1)
    seg = jnp.where((r // D_QK) == (c // D_QK), 1.0 / D_QK, 0.0).astype(BF)
    cos, s_lo, s_hi = cos_ref[...], slo_ref[...], shi_ref[...]
    for out_ref, gain_ref, col0, scale in ((q_ref, qg_ref, MIX_W, Q_SCALE), (k_ref, kg_ref, 2 * MIX_W, 1.0)):
        z = _dot(h, w_ref[:, col0:col0 + MIX_W])
        for hd in range(MIX_W // HEAD_W):
            lanes = slice(hd * HEAD_W, (hd + 1) * HEAD_W)
            out_ref[:, lanes] = _qk_norm_rope(z[:, lanes], gain_ref[...], seg, cos, s_lo, s_hi, scale)
    p_ref[...] = _dot(h, w_ref[:, :MIX_W])


def _proj1(x, gain, w, q_gain, k_gain, groups, tm):
    n, d = x.shape
    e = w.shape[1]
    _, _, pos = _block_tables(groups, tm)
    s_max = max(s for _, s in groups)
    cos, s_lo, s_hi = _rope_tables(s_max)
    tok = lambda width: pl.BlockSpec((tm, width), lambda i, p: (i, 0))
    full = lambda shape: pl.BlockSpec(shape, lambda i, p: (0,) * len(shape))
    rope = pl.BlockSpec((tm, LANES), lambda i, p: (p[i], 0))
    tile2 = lambda g: jnp.concatenate([g, g]).reshape(1, LANES)
    return pl.pallas_call(
        _proj1_kernel,
        out_shape=(jax.ShapeDtypeStruct((n, MIX_W), F32), jax.ShapeDtypeStruct((n, MIX_W), BF),
                   jax.ShapeDtypeStruct((n, MIX_W), BF), jax.ShapeDtypeStruct((MIX_W, n), BF)),
        grid_spec=pltpu.PrefetchScalarGridSpec(
            num_scalar_prefetch=1,
            grid=(n // tm,),
            in_specs=[tok(d), full((1, d)), full((d, e)), full((1, LANES)), full((1, LANES)),
                      rope, rope, rope],
            out_specs=[tok(MIX_W)] * 3 + [pl.BlockSpec((MIX_W, tm), lambda i, p: (0, i))],
        ),
        compiler_params=_cparams(1),
        name="proj1_qk_rope",
    )(jnp.asarray(pos), x, gain.reshape(1, -1), w, tile2(q_gain), tile2(k_gain), cos, s_lo, s_hi)


def _pool_block(is_start, is_end, prev_ref, cur_ref, next_ref, wc_ref, scale_ref, t):
    cur = cur_ref[...]
    prev = jnp.where(is_start, 0.0, prev_ref[...])
    nxt = jnp.where(is_end, 0.0, next_ref[...])
    ext = jnp.concatenate([prev, cur, nxt], axis=0)
    ext_hi, ext_lo = _split2(ext)
    sub = min(t, POOL_SUB)
    r = lax.broadcasted_iota(jnp.int32, (sub, sub + 2 * POOL_HALO), 0) + POOL_HALO
    c = lax.broadcasted_iota(jnp.int32, (sub, sub + 2 * POOL_HALO), 1)
    row = lax.broadcasted_iota(jnp.int32, (t, 1), 0)
    parts = []
    for gi, w in enumerate(POOL_WINDOWS):
        lanes = slice(gi * HEAD_W, (gi + 1) * HEAD_W)
        hw = w // 2
        band = jnp.where(jnp.logical_and(c >= r - hw, c < r + hw), 1.0, 0.0).astype(BF)
        strips = []
        for s0 in range(0, t, sub):
            rows = slice(s0, s0 + sub + 2 * POOL_HALO)
            both = _dot(band, jnp.concatenate([ext_hi[rows, lanes], ext_lo[rows, lanes]], axis=1))
            strips.append(both[:, :HEAD_W] + both[:, HEAD_W:])
        win = jnp.concatenate(strips, axis=0)
        lo_cut = jnp.where(is_start, jnp.maximum(hw - row, 0), 0)
        hi_cut = jnp.where(is_end, jnp.maximum(row + hw - t, 0), 0)
        cnt = (w - lo_cut - hi_cut).astype(F32)
        diff = win / cnt - cur[:, lanes]
        parts.append(_dot(diff.astype(BF), wc_ref[gi]) * scale_ref[:, lanes])
    return jnp.concatenate(parts, axis=-1)


def _diff_attn_kernel(qb_ref, kb_ref, first_ref, last_ref,
                      q_ref, k_ref, vt_ref, lq1_ref, lk1_ref, lq2_ref, lk2_ref, sub_ref, bound_ref,
                      o_ref, qzt_ref, v1_ref, m_ref, acc_ref, *, tq, tkc, n_chunks, unroll, lam_init):
    p = pl.program_id(1)

    @pl.when(first_ref[p] == 1)
    def _():
        qt = q_ref[...].astype(F32).T.astype(BF)
        row = lax.broadcasted_iota(jnp.int32, qt.shape, 0)
        qzt_ref[:, :tq] = jnp.where(row < D_QK, qt, jnp.zeros_like(qt))
        qzt_ref[:, tq:] = jnp.where(row >= D_QK, qt, jnp.zeros_like(qt))
        for slot in range(unroll):
            v1_ref[slot, HEAD_W:, :] = jnp.ones((ONES_ROWS, tkc), BF)
        m_ref[...] = jnp.full_like(m_ref, -jnp.inf)
        acc_ref[...] = jnp.zeros_like(acc_ref)

    def chunk(j, slot):
        off = pl.multiple_of(j * tkc, tkc)
        v1_ref[slot, :HEAD_W, :] = vt_ref[:, pl.ds(off, tkc)]
        st = _dot(k_ref[pl.ds(off, tkc), :], qzt_ref[...])
        m_old = m_ref[...]
        m_new = jnp.maximum(m_old, jnp.max(st, axis=0, keepdims=True))
        alpha = jnp.exp2(m_old - m_new)
        pt = jnp.exp2(st - m_new).astype(BF)
        acc_ref[...] = alpha * acc_ref[...] + _dot(v1_ref[slot], pt)
        m_ref[...] = m_new

    def chunks(i, carry):
        for slot in range(unroll):
            chunk(i * unroll + slot, slot)
        return carry

    bound = bound_ref[0, 0]
    fixed_shift_ok = bound <= MAX_FIXED_SHIFT

    def fixed_chunks(i, carry):
        total = None
        for slot in range(unroll):
            off = pl.multiple_of((i * unroll + slot) * tkc, tkc)
            v1_ref[slot, :HEAD_W, :] = vt_ref[:, pl.ds(off, tkc)]
            st = _dot(k_ref[pl.ds(off, tkc), :], qzt_ref[...])
            part = _dot(v1_ref[slot], jnp.exp2(st - bound).astype(BF))
            total = part if total is None else total + part
        acc_ref[...] += total
        return carry

    @pl.when(fixed_shift_ok)
    def _():
        lax.fori_loop(0, n_chunks // unroll, fixed_chunks, 0)

    @pl.when(jnp.logical_not(fixed_shift_ok))
    def _():
        lax.fori_loop(0, n_chunks // unroll, chunks, 0)

    @pl.when(last_ref[p] == 1)
    def _():
        lam = (jnp.exp(jnp.sum(lq1_ref[...] * lk1_ref[...], keepdims=True))
               - jnp.exp(jnp.sum(lq2_ref[...] * lk2_ref[...], keepdims=True)) + lam_init)
        acc = acc_ref[...]
        o = acc[:HEAD_W, :] / acc[HEAD_W:HEAD_W + 1, :]
        d = o[:, :tq] - lam * o[:, tq:]
        ms = jnp.mean(d * d, axis=0, keepdims=True)
        y = d * lax.rsqrt(ms + EPS) * sub_ref[...] * (1.0 - lam_init)
        o_ref[...] = y.T


def _diff_attn(q, k, vt, q_gain, k_gain, lq1, lk1, lq2, lk2, subln, groups, lam_init, tq, tkb, tkc):
    n = q.shape[0]
    bound = (1.01 * D_QK * Q_SCALE * jnp.max(jnp.abs(q_gain)) * jnp.max(jnp.abs(k_gain))).reshape(1, 1)
    qb, kb, first, last = _attn_tables(groups, tq, tkb)
    n_heads = MIX_W // HEAD_W
    vec = lambda a: a.reshape(1, -1)
    full = lambda shape: pl.BlockSpec(shape, lambda h, p, *_: (0, 0))
    sub_col = jnp.broadcast_to(subln[:, None], (HEAD_W, tq))
    n_chunks = tkb // tkc
    unroll = next(u for u in (8, 4, 2, 1) if n_chunks % u == 0)
    return pl.pallas_call(
        functools.partial(_diff_attn_kernel, tq=tq, tkc=tkc, n_chunks=n_chunks, unroll=unroll,
                          lam_init=lam_init),
        out_shape=jax.ShapeDtypeStruct((n, MIX_W), F32),
        grid_spec=pltpu.PrefetchScalarGridSpec(
            num_scalar_prefetch=4,
            grid=(n_heads, len(qb)),
            in_specs=[
                pl.BlockSpec((tq, HEAD_W), lambda h, p, qb, kb, f, l: (qb[p], h)),
                pl.BlockSpec((tkb, HEAD_W), lambda h, p, qb, kb, f, l: (kb[p], h)),
                pl.BlockSpec((HEAD_W, tkb), lambda h, p, qb, kb, f, l: (h, kb[p])),
                full((1, D_QK)), full((1, D_QK)), full((1, D_QK)), full((1, D_QK)), full((HEAD_W, tq)),
                pl.BlockSpec(memory_space=pltpu.SMEM),
            ],
            out_specs=pl.BlockSpec((tq, HEAD_W), lambda h, p, qb, kb, f, l: (qb[p], h)),
            scratch_shapes=[pltpu.VMEM((HEAD_W, 2 * tq), BF),
                            pltpu.VMEM((unroll, HEAD_W + ONES_ROWS, tkc), BF),
                            pltpu.VMEM((1, 2 * tq), F32), pltpu.VMEM((HEAD_W + ONES_ROWS, 2 * tq), F32)],
        ),
        compiler_params=_cparams(2),
        name="diff_attn",
    )(jnp.asarray(qb), jnp.asarray(kb), jnp.asarray(first), jnp.asarray(last),
      q, k, vt, vec(lq1), vec(lk1), vec(lq2), vec(lk2), sub_col, bound.astype(F32))


def _mix1_route_kernel(starts_ref, ends_ref, x_ref, pprev_ref, p_ref, pnext_ref, a_ref, wc_ref, ps_ref,
                       wo_ref, g_ref, wr_ref,
                       x3_ref, h_ref, eidx_ref, gate_ref, rank_ref, cnt_ref, base_ref, *, tm):
    i = pl.program_id(0)

    @pl.when(i == 0)
    def _():
        base_ref[...] = jnp.zeros_like(base_ref)

    c_out = _pool_block(starts_ref[i] == 1, ends_ref[i] == 1, pprev_ref, p_ref, pnext_ref, wc_ref, ps_ref, tm)
    mix_in = jnp.concatenate([c_out, a_ref[...]], axis=-1).astype(BF)
    x3 = x_ref[...] + _dot(mix_in, wo_ref[...])
    x3_ref[...] = x3
    h = _rms_rows(x3, g_ref[...])
    h_ref[...] = h
    wr = wr_ref[...]
    wr_hi = wr.astype(BF).astype(F32)
    wr_parts = jnp.concatenate([wr_hi, wr - wr_hi], axis=0).astype(BF)
    h_hi, h_lo = _split2(h)
    by_hi = _dot_nt(wr_parts, h_hi)
    logits = by_hi[:N_EXPERTS] + by_hi[N_EXPERTS:] + _dot_nt(wr_hi.astype(BF), h_lo)
    eid = lax.broadcasted_iota(jnp.int32, logits.shape, 0)
    m1 = jnp.max(logits, axis=0, keepdims=True)
    i1 = jnp.min(jnp.where(logits == m1, eid, N_EXPERTS), axis=0, keepdims=True)
    rest = jnp.where(eid == i1, -jnp.inf, logits)
    m2 = jnp.max(rest, axis=0, keepdims=True)
    i2 = jnp.min(jnp.where(rest == m2, eid, N_EXPERTS), axis=0, keepdims=True)
    e2 = jnp.exp(m2 - m1)
    g1 = 1.0 / (1.0 + e2)
    g2 = e2 / (1.0 + e2)
    sel1 = eid == i1
    sel2 = eid == i2
    onehot = jnp.where(jnp.logical_or(sel1, sel2), 1.0, 0.0)
    r = lax.broadcasted_iota(jnp.int32, (tm, tm), 0)
    c = lax.broadcasted_iota(jnp.int32, (tm, tm), 1)
    incl = _dot(onehot.astype(BF), jnp.where(r <= c, 1.0, 0.0).astype(BF))
    before = base_ref[:, :1] + incl - onehot
    eidx_ref[0:1, :] = i1
    eidx_ref[1:2, :] = i2
    gate_ref[0:1, :] = g1
    gate_ref[1:2, :] = g2
    rank_ref[0:1, :] = jnp.sum(jnp.where(sel1, before, 0.0), axis=0, keepdims=True).astype(jnp.int32)
    rank_ref[1:2, :] = jnp.sum(jnp.where(sel2, before, 0.0), axis=0, keepdims=True).astype(jnp.int32)
    base_ref[...] = base_ref[...] + jnp.sum(onehot, axis=1, keepdims=True)
    cnt_ref[...] = base_ref[...].astype(jnp.int32)


def _mix1_route(x, pz, att, pool_w, pool_scale, w_out, g2, w_router, groups, tm):
    n, d = x.shape
    starts, ends, _ = _block_tables(groups, tm)
    per = tm // POOL_HALO
    n_halo = n // POOL_HALO
    tok = lambda width: pl.BlockSpec((tm, width), lambda i, s, e: (i, 0))
    lane_tok = pl.BlockSpec((2, tm), lambda i, s, e: (0, i))
    full = lambda shape: pl.BlockSpec(shape, lambda i, s, e: (0,) * len(shape), pipeline_mode=pl.Buffered(1))
    return pl.pallas_call(
        functools.partial(_mix1_route_kernel, tm=tm),
        out_shape=(jax.ShapeDtypeStruct((n, d), F32), jax.ShapeDtypeStruct((n, d), F32),
                   jax.ShapeDtypeStruct((2, n), jnp.int32), jax.ShapeDtypeStruct((2, n), F32),
                   jax.ShapeDtypeStruct((2, n), jnp.int32),
                   jax.ShapeDtypeStruct((N_EXPERTS, LANES), jnp.int32)),
        grid_spec=pltpu.PrefetchScalarGridSpec(
            num_scalar_prefetch=2,
            grid=(n // tm,),
            in_specs=[tok(d),
                      pl.BlockSpec((POOL_HALO, MIX_W), lambda i, s, e: (jnp.maximum(i * per - 1, 0), 0)),
                      tok(MIX_W),
                      pl.BlockSpec((POOL_HALO, MIX_W),
                                   lambda i, s, e: (jnp.minimum((i + 1) * per, n_halo - 1), 0)),
                      tok(MIX_W), full(pool_w.shape), full((1, MIX_W)),
                      full(w_out.shape), full((1, d)), full((N_EXPERTS, d))],
            out_specs=[tok(d), tok(d), lane_tok, lane_tok, lane_tok,
                       pl.BlockSpec((N_EXPERTS, LANES), lambda i, s, e: (0, 0))],
            scratch_shapes=[pltpu.VMEM((N_EXPERTS, LANES), F32)],
        ),
        compiler_params=_cparams(1),
        name="mix1_route",
    )(jnp.asarray(starts), jnp.asarray(ends), x, pz, pz, pz, att, pool_w.astype(BF), pool_scale.reshape(1, -1),
      w_out, g2.reshape(1, -1), w_router.T)


def _experts_kernel(be_ref, nused_ref, row0_ref, rown_ref, rowp_ref, h_hbm, w1_hbm, w3_hbm, w2_hbm, y_hbm,
                    xbuf, ybuf, w1_v, w3_v, w2_v, stage_in, stage_out, gsem, ssem, wsem, *, mb, f_chunk, n_rows):
    b = pl.program_id(0)
    cur = b % 2
    oth = 1 - cur
    e = be_ref[b]
    prev_e = be_ref[jnp.maximum(b - 1, 0)]
    d_e = w1_v.shape[1]
    n_chunks = d_e // f_chunk

    n_tok = n_rows // 2

    def gather_row(row_ref, r, slot):
        v = row_ref[0, 0, r]
        tok = v - jnp.where(v >= n_rows, n_rows, jnp.where(v >= n_tok, n_tok, 0))
        pltpu.make_async_copy(h_hbm.at[pl.ds(tok, 1), :],
                              xbuf.at[slot, pl.ds(r, 1), :], gsem.at[slot]).start()

    def scatter_row(r, slot):
        pltpu.make_async_copy(ybuf.at[slot, pl.ds(r, 1), :],
                              y_hbm.at[pl.ds(rowp_ref[0, 0, r], 1), :], ssem.at[slot]).start(priority=1)

    def load_weights():
        pieces = []
        for c in range(n_chunks):
            cols = pl.ds(c * f_chunk, f_chunk)
            pieces.append((w1_hbm.at[e, :, cols], stage_in, w1_v.at[:, cols]))
            pieces.append((w3_hbm.at[e, :, cols], stage_in, w3_v.at[:, cols]))
            pieces.append((w2_hbm.at[e, cols, :], stage_out, w2_v.at[cols, :]))

        def copy(i):
            src, stage, _ = pieces[i]
            return pltpu.make_async_copy(src, stage.at[i % 2], wsem.at[i % 2])

        copy(0).start()
        for i, (_, stage, dst) in enumerate(pieces):
            if i + 1 < len(pieces):
                copy(i + 1).start()
            copy(i).wait()
            dst[...] = stage[i % 2].astype(BF)

    @pl.when(b == 0)
    def _():
        ybuf[...] = jnp.zeros_like(ybuf)
        zero_tail = pltpu.make_async_copy(ybuf.at[0], y_hbm.at[pl.ds(n_rows, mb), :], wsem.at[0])
        zero_tail.start()
        zero_tail.wait()
        lax.fori_loop(0, mb, lambda r, c: (gather_row(row0_ref, r, 0), c)[1], 0)

    live = b <= nused_ref[0]

    @pl.when(live)
    def _():
        pltpu.make_async_copy(h_hbm.at[pl.ds(0, mb), :], xbuf.at[cur], gsem.at[cur]).wait()

    @pl.when(b < nused_ref[0])
    def _():
        new_expert = jnp.logical_or(b == 0, e != prev_e)

        @pl.when(new_expert)
        def _():
            load_weights()

        x = xbuf[cur].astype(BF)
        acc = jnp.zeros((mb, D_MODEL), F32)
        for c in range(n_chunks):
            cols = slice(c * f_chunk, (c + 1) * f_chunk)
            hh = _silu(_dot(x, w1_v[:, cols])) * _dot(x, w3_v[:, cols])
            acc = acc + _dot(hh.astype(BF), w2_v[cols, :])
            n_dma = n_chunks - 1
            for r in range(min(c, n_dma) * mb // n_dma, min(c + 1, n_dma) * mb // n_dma):
                gather_row(rown_ref, r, oth)
                scatter_row(r, oth)
        ybuf[cur] = acc

    @pl.when(b == nused_ref[0])
    def _():
        lax.fori_loop(0, mb, lambda r, c: (scatter_row(r, oth), c)[1], 0)

    @pl.when(live)
    def _():
        pltpu.make_async_copy(ybuf.at[oth], y_hbm.at[pl.ds(0, mb), :], ssem.at[oth]).wait()


def _experts(h, slot_row, block_e, n_used, w1, w3, w2, mb, n_rows):
    n_steps = block_e.shape[0]
    d = h.shape[1]
    d_e = w1.shape[2]
    f_chunk = 512
    any_spec = pl.BlockSpec(memory_space=pl.ANY)
    smem_blk = lambda fn: pl.BlockSpec((1, 1, mb), fn, memory_space=pltpu.SMEM)
    return pl.pallas_call(
        functools.partial(_experts_kernel, mb=mb, f_chunk=f_chunk, n_rows=n_rows),
        out_shape=jax.ShapeDtypeStruct((n_rows + mb, d), F32),
        grid_spec=pltpu.PrefetchScalarGridSpec(
            num_scalar_prefetch=2,
            grid=(n_steps,),
            in_specs=[smem_blk(lambda b, be, nu: (0, 0, 0)),
                      smem_blk(lambda b, be, nu: (jnp.minimum(b + 1, n_steps - 1), 0, 0)),
                      smem_blk(lambda b, be, nu: (jnp.maximum(b - 1, 0), 0, 0)),
                      any_spec, any_spec, any_spec, any_spec],
            out_specs=any_spec,
            scratch_shapes=[pltpu.VMEM((2, mb, d), F32), pltpu.VMEM((2, mb, d), F32),
                            pltpu.VMEM((d, d_e), BF), pltpu.VMEM((d, d_e), BF), pltpu.VMEM((d_e, d), BF),
                            pltpu.VMEM((2, d, f_chunk), F32), pltpu.VMEM((2, f_chunk, d), F32),
                            pltpu.SemaphoreType.DMA((2,)), pltpu.SemaphoreType.DMA((2,)),
                            pltpu.SemaphoreType.DMA((2,))],
        ),
        compiler_params=_cparams(1),
        name="experts",
    )(block_e, n_used, slot_row, slot_row, slot_row, h, w1, w3, w2)


def _combine_kernel(x_ref, gate_ref, y0_ref, y1_ref, oa_ref, ob_ref, *, tm, na):
    i = pl.program_id(0)
    g = jnp.concatenate([gate_ref[...], jnp.zeros((6, tm), F32)], axis=0).T
    out = x_ref[...] + g[:, 0:1] * y0_ref[...] + g[:, 1:2] * y1_ref[...]

    @pl.when(i < na)
    def _():
        oa_ref[...] = out

    @pl.when(i >= na)
    def _():
        ob_ref[...] = out


def _combine(x, gates, y, n_first, tm):
    n, d = x.shape
    nblk = n // tm
    na = n_first // tm
    return pl.pallas_call(
        functools.partial(_combine_kernel, tm=tm, na=na),
        out_shape=(jax.ShapeDtypeStruct((n_first, d), F32), jax.ShapeDtypeStruct((n - n_first, d), F32)),
        grid=(nblk,),
        in_specs=[pl.BlockSpec((tm, d), lambda i: (i, 0)),
                  pl.BlockSpec((2, tm), lambda i: (0, i)),
                  pl.BlockSpec((tm, d), lambda i: (i, 0)),
                  pl.BlockSpec((tm, d), lambda i: (i + nblk, 0))],
        out_specs=[pl.BlockSpec((tm, d), lambda i: (jnp.minimum(i, na - 1), 0)),
                   pl.BlockSpec((tm, d), lambda i: (jnp.maximum(i - na, 0), 0))],
        compiler_params=_cparams(1),
        name="combine",
    )(x, gates, y, y)


def _moe(x3, h, eidx, gates, rank, counts, w1, w3, w2, n_first, mb, tm):
    n = x3.shape[0]
    n_blocks = -(-2 * n // mb) + N_EXPERTS
    cnt = counts[:, 0]
    padded = ((cnt + mb - 1) // mb) * mb
    pends = jnp.cumsum(padded)
    pstarts = pends - padded
    start_of = sum(jnp.where(eidx == e, pstarts[e], 0) for e in range(N_EXPERTS))
    dest = (start_of + rank).reshape(-1)
    rows = jnp.arange(2 * n, dtype=jnp.int32)
    n_slots = (n_blocks + 1) * mb
    spare = 2 * n + jnp.arange(n_slots, dtype=jnp.int32) % mb
    slot_row = spare.at[dest].set(rows, unique_indices=True)
    blk_start = jnp.arange(n_blocks + 1, dtype=jnp.int32) * mb
    block_e = jnp.minimum(jnp.searchsorted(pends, blk_start, side="right"), N_EXPERTS - 1).astype(jnp.int32)
    n_used = (pends[-1] // mb).astype(jnp.int32).reshape(1)
    y = _experts(h, slot_row.reshape(n_blocks + 1, 1, mb), block_e, n_used, w1, w3, w2, mb, 2 * n)
    return _combine(x3, gates, y, n_first, tm)


def _trunk(xa, xb, groups, p):
    bf = lambda a: a.astype(BF)
    z = _norm_proj(xa, xb, p["e_norm1"][0].reshape(1, -1), bf(p["e_w_in"][0]), tm=512)
    o_f, o_b, sgu = _hgrn_sgu(z, p["hgrn_lb"], p["e_sgu_ln_g"][0], p["e_sgu_ln_b"][0],
                              p["e_sgu_w"][0], p["e_sgu_b"][0], groups, layer=0, t=512)
    x = _mix0_ffn(xa, xb, z, o_f, o_b, sgu, p["e_hgrn_gnorm"][0], bf(p["e_w_out"][0]), p["e_norm2"][0],
                  bf(p["e_ffn_w1"][0]), bf(p["e_ffn_w3"][0]), bf(p["e_ffn_w2"][0]), tm=512)
    layer = 1
    lam_init = 0.8 - 0.6 * math.exp(-0.3 * layer)
    pz, q, k, vt = _proj1(x, p["o_norm1"][0], bf(p["o_w_in"][0]), p["o_q_norm"][0], p["o_k_norm"][0],
                          groups, tm=512)
    tkb = min(4096, math.gcd(*[s for _, s in groups]))
    att = _diff_attn(q, k, vt, p["o_q_norm"][0], p["o_k_norm"][0], p["o_lambda_q1"][0], p["o_lambda_k1"][0], p["o_lambda_q2"][0],
                     p["o_lambda_k2"][0], p["o_subln"][0], groups, lam_init, tq=min(1024, tkb), tkb=tkb,
                     tkc=min(512, tkb))
    x3, h, eidx, gates, rank, counts = _mix1_route(x, pz, att, p["o_pool_w"][0], p["o_pool_scale"][0],
                                                   bf(p["o_w_out"][0]), p["o_norm2"][0], p["o_router"][0],
                                                   groups, tm=512)
    return _moe(x3, h, eidx, gates, rank, counts, p["o_moe_w1"][0], p["o_moe_w3"][0], p["o_moe_w2"][0],
                n_first=xa.shape[0], mb=512, tm=512)


def kernel(x_prompt, x_sample, hgrn_lb, e_norm1, e_w_in, e_hgrn_gnorm, e_sgu_ln_g, e_sgu_ln_b, e_sgu_w, e_sgu_b, e_w_out, e_norm2, e_ffn_w1, e_ffn_w3, e_ffn_w2, o_norm1, o_w_in, o_pool_w, o_pool_scale, o_q_norm, o_k_norm, o_lambda_q1, o_lambda_k1, o_lambda_q2, o_lambda_k2, o_subln, o_w_out, o_norm2, o_router, o_moe_w1, o_moe_w3, o_moe_w2):
    params = dict(
        hgrn_lb=hgrn_lb, e_norm1=e_norm1, e_w_in=e_w_in, e_hgrn_gnorm=e_hgrn_gnorm,
        e_sgu_ln_g=e_sgu_ln_g, e_sgu_ln_b=e_sgu_ln_b, e_sgu_w=e_sgu_w, e_sgu_b=e_sgu_b,
        e_w_out=e_w_out, e_norm2=e_norm2, e_ffn_w1=e_ffn_w1, e_ffn_w3=e_ffn_w3, e_ffn_w2=e_ffn_w2,
        o_norm1=o_norm1, o_w_in=o_w_in, o_pool_w=o_pool_w, o_pool_scale=o_pool_scale,
        o_q_norm=o_q_norm, o_k_norm=o_k_norm, o_lambda_q1=o_lambda_q1, o_lambda_k1=o_lambda_k1,
        o_lambda_q2=o_lambda_q2, o_lambda_k2=o_lambda_k2, o_subln=o_subln, o_w_out=o_w_out,
        o_norm2=o_norm2, o_router=o_router, o_moe_w1=o_moe_w1, o_moe_w3=o_moe_w3, o_moe_w2=o_moe_w2,
    )
    d = x_prompt.shape[-1]
    groups = (x_prompt.shape[:2], x_sample.shape[:2])
    y_p, y_s = _trunk(x_prompt.reshape(-1, d), x_sample.reshape(-1, d), groups, params)
    return (y_p.reshape(x_prompt.shape), y_s.reshape(x_sample.shape))
```
